```python
import math
import jax, jax.numpy as jnp
from jax import lax
import numpy as np

D_MODEL = 1024
BATCH = 2
SEQ = 8192
DEPTH = 4
DEC_BATCH = 128
DEC_SEQ = 1
PAST_LEN = 2048
PAGE_SIZE = 128

HEAD_DIM = 64
N_MIX_HEADS = D_MODEL // HEAD_DIM
NSA_HEADS = N_MIX_HEADS // 2
NSA_KV_HEADS = 2
SB_HEADS = N_MIX_HEADS - NSA_HEADS
SB_KV_HEADS = 4
MOBA_HEADS = N_MIX_HEADS
MOBA_KV_HEADS = 4
G_NSA = NSA_HEADS // NSA_KV_HEADS
G_SB = SB_HEADS // SB_KV_HEADS
G_MOBA = MOBA_HEADS // MOBA_KV_HEADS
CMP_LEN = 32
CMP_STRIDE = 16
SEL_BLOCK = 64
N_SEL = 16
WINDOW = 512
MOBA_BLOCK = 256
MOBA_TOPK = 3
N_BUCKETS = 32
REL_MAX_DIST = 1024
D_FF = 2816
N_EXPERTS = 8
TOP_K = 2
MOE_D_FF = 3584
P_DIM = 256
Q_BLOCK = 128
MOBA_Q_BLOCK = 32
N_EVEN = (DEPTH + 1) // 2
N_ODD = DEPTH // 2
SCALE = HEAD_DIM ** -0.5
EVEN_SIZES = (NSA_HEADS * HEAD_DIM,) + (NSA_KV_HEADS * HEAD_DIM,) * 6 + (NSA_HEADS * 3, SB_HEADS * HEAD_DIM, SB_KV_HEADS * HEAD_DIM, SB_KV_HEADS * HEAD_DIM)
EVEN_IN = sum(EVEN_SIZES)
ODD_SIZES = (MOBA_HEADS * HEAD_DIM, MOBA_KV_HEADS * HEAD_DIM, MOBA_KV_HEADS * HEAD_DIM)
ODD_IN = sum(ODD_SIZES)
EVEN_MIX = (NSA_HEADS + SB_HEADS) * HEAD_DIM
ODD_MIX = MOBA_HEADS * HEAD_DIM
NEG_INF = -1e30
SEL_BIG = 1e9

kernel_name = 'nsa_stickbreak_moba_hybrid_step'


def rms_norm(x, g, eps=1e-6):
    xf = x.astype(jnp.float32)
    y = xf * lax.rsqrt(jnp.mean(xf * xf, axis=-1, keepdims=True) + eps)
    return (y * g.astype(jnp.float32)).astype(x.dtype)


def split_cols(y, sizes):
    return jnp.split(y, np.cumsum(sizes)[:-1].tolist(), axis=-1)


def masked_softmax(s, mask):
    s = jnp.where(mask, s, NEG_INF)
    m = jnp.max(s, axis=-1, keepdims=True)
    e = jnp.where(mask, jnp.exp(s - m), 0.0)
    return e / jnp.maximum(jnp.sum(e, axis=-1, keepdims=True), 1e-30)


def rel_bucket(dist):
    n = jnp.maximum(dist, 0)
    exact = N_BUCKETS // 2
    nf = jnp.maximum(n, 1).astype(jnp.float32)
    big = exact + (jnp.log(nf / exact) / math.log(REL_MAX_DIST / exact) * (N_BUCKETS - exact)).astype(jnp.int32)
    return jnp.where(n < exact, n, jnp.minimum(big, N_BUCKETS - 1))


def to_blocks(x, bs):
    B, T, H, D = x.shape
    pad = (-T) % bs
    x = jnp.pad(x, ((0, 0), (0, pad), (0, 0), (0, 0)))
    return x.reshape(B, (T + pad) // bs, bs, H, D).transpose(0, 3, 1, 2, 4)


def gather_blocks(xb, idx):
    return jax.vmap(jax.vmap(lambda a, i: a[i]))(xb, idx)


def gather_pages(pool, page_table):
    g = pool[page_table]
    return g.reshape(g.shape[0], g.shape[1] * g.shape[2], *g.shape[3:])


def compress_rows(rows, pos_emb, w1, w2):
    B, T, H, D = rows.shape
    lhs = rows.transpose(0, 2, 3, 1).reshape(B * H, D, T)
    z = lax.conv_general_dilated(lhs, w1.transpose(2, 1, 0), (CMP_STRIDE,), 'VALID', dimension_numbers=('NCH', 'OIH', 'NCH'))
    z = z.reshape(B, H, D, -1).transpose(0, 3, 1, 2) + jnp.einsum('ld,lde->e', pos_emb, w1)
    return jax.nn.gelu(z) @ w2


def cmp_to_sel(nc, nsb):
    cs = CMP_STRIDE * np.arange(nc)[:, None]
    ss = SEL_BLOCK * np.arange(nsb)[None, :]
    ov = np.clip(np.minimum(cs + CMP_LEN, ss + SEL_BLOCK) - np.maximum(cs, ss), 0, None)
    return jnp.asarray(ov / CMP_LEN, dtype=jnp.float32)


def nsa_attend(q, gates, q_pos, kc, vc, ks_b, vs_b, kw, vw, kw_pos, tb):
    B, Tq, Hkv, G, D = q.shape
    tb_n = tb[:, :NSA_HEADS].reshape(N_BUCKETS, Hkv, G)
    hk = jnp.arange(Hkv)[None, :, None, None, None, None]
    gg = jnp.arange(G)[None, None, :, None, None, None]
    nc = kc.shape[1]
    dist_c = q_pos[:, None] - (CMP_STRIDE * jnp.arange(nc) + CMP_LEN - 1)[None, :]
    s_c = jnp.einsum('bqkgd,bckd->bkgqc', q, kc).astype(jnp.float32) * SCALE + tb_n[rel_bucket(dist_c)].transpose(2, 3, 0, 1)
    p_c = masked_softmax(s_c, dist_c >= 0)
    o_c = jnp.einsum('bkgqc,bckd->bqkgd', p_c.astype(vc.dtype), vc)
    nsb = ks_b.shape[2]
    imp = jnp.einsum('bkqc,cn->bkqn', jnp.sum(p_c, axis=2), cmp_to_sel(nc, nsb))
    blk = jnp.arange(nsb)
    valid = (SEL_BLOCK * blk)[None, :] <= q_pos[:, None]
    forced = (blk[None, :] == (q_pos // SEL_BLOCK)[:, None]) | (blk[None, :] == 0)
    score = jnp.where(forced, SEL_BIG, jnp.where(valid, imp, -SEL_BIG))
    _, idx = lax.top_k(score, min(N_SEL, nsb))
    kg = gather_blocks(ks_b, idx)
    vg = gather_blocks(vs_b, idx)
    dist_s = q_pos[None, None, :, None, None] - (idx[..., None] * SEL_BLOCK + jnp.arange(SEL_BLOCK))
    bias_s = tb_n[rel_bucket(dist_s)[:, :, None], hk, gg]
    s_s = jnp.einsum('bqkgd,bkqnld->bkgqnl', q, kg).astype(jnp.float32) * SCALE + bias_s
    p_s = masked_softmax(s_s.reshape(B, Hkv, G, Tq, -1), (dist_s >= 0)[:, :, None].reshape(B, Hkv, 1, Tq, -1))
    o_s = jnp.einsum('bkgqm,bkqmd->bqkgd', p_s.astype(vg.dtype), vg.reshape(B, Hkv, Tq, -1, D))
    dist_w = q_pos[:, None] - kw_pos[None, :]
    mask_w = (kw_pos[None, :] >= 0) & (dist_w >= 0) & (dist_w < WINDOW)
    s_w = jnp.einsum('bqkgd,bskd->bkgqs', q, kw).astype(jnp.float32) * SCALE + tb_n[rel_bucket(dist_w)].transpose(2, 3, 0, 1)
    p_w = masked_softmax(s_w, mask_w)
    o_w = jnp.einsum('bkgqs,bskd->bqkgd', p_w.astype(vw.dtype), vw)
    return gates[..., 0:1] * o_c + gates[..., 1:2] * o_s + gates[..., 2:3] * o_w


def stick_breaking(q, q_pos, k, v):
    z = jnp.einsum('bqkgd,bskd->bkgqs', q, k).astype(jnp.float32) * SCALE
    before = jnp.arange(k.shape[1])[None, :] < q_pos[:, None]
    log_keep = jnp.where(before, jax.nn.log_sigmoid(-z), 0.0)
    between = lax.cumsum(log_keep, axis=z.ndim - 1, reverse=True) - log_keep
    a = jnp.where(before, jnp.exp(jax.nn.log_sigmoid(z) + between), 0.0)
    return jnp.einsum('bkgqs,bskd->bqkgd', a.astype(v.dtype), v)


def moba_attend(q, q_pos, kb, vb, kmean, tb):
    B, Tq, Hkv, G, D = q.shape
    nb = kb.shape[2]
    own = q_pos // MOBA_BLOCK
    past = jnp.arange(nb)[None, :] < own[:, None]
    gate = jnp.where(past, jnp.einsum('bqkgd,bknd->bkgqn', q, kmean).astype(jnp.float32), -SEL_BIG)
    gval, idx = lax.top_k(gate, min(MOBA_TOPK, nb))
    idx = jnp.concatenate([idx, jnp.broadcast_to(own[:, None], (B, Hkv, G, Tq, 1))], axis=-1)
    ok = jnp.concatenate([gval > -SEL_BIG / 2, jnp.ones((B, Hkv, G, Tq, 1), bool)], axis=-1)
    n = idx.shape[-1]
    flat = idx.reshape(B, Hkv, G * Tq, n)
    kg = gather_blocks(kb, flat).reshape(B, Hkv, G, Tq, n, MOBA_BLOCK, D)
    vg = gather_blocks(vb, flat).reshape(B, Hkv, G, Tq, n * MOBA_BLOCK, D)
    dist = q_pos[:, None, None] - (idx[..., None] * MOBA_BLOCK + jnp.arange(MOBA_BLOCK))
    mask = ok[..., None] & (dist >= 0)
    tb_m = tb[:, :MOBA_HEADS].reshape(N_BUCKETS, Hkv, G)
    bias = tb_m[rel_bucket(dist), jnp.arange(Hkv)[None, :, None, None, None, None], jnp.arange(G)[None, None, :, None, None, None]]
    s = jnp.einsum('bqkgd,bkgqnld->bkgqnl', q, kg).astype(jnp.float32) * SCALE + bias
    p = masked_softmax(s.reshape(B, Hkv, G, Tq, -1), mask.reshape(B, Hkv, G, Tq, -1))
    return jnp.einsum('bkgqm,bkgqmd->bqkgd', p.astype(vg.dtype), vg)


def even_project(a, w_in, q_gain, k_gain):
    B, T, _ = a.shape
    q, kc, vc, ks, vs, kw, vw, g, qs, ksb, vsb = split_cols(a @ w_in, EVEN_SIZES)
    heads = lambda t: t.reshape(B, T, -1, HEAD_DIM)
    q = rms_norm(heads(q), q_gain).reshape(B, T, NSA_KV_HEADS, G_NSA, HEAD_DIM)
    cmp_rows = jnp.stack([heads(kc), heads(vc)], axis=2)
    slc_rows = jnp.stack([rms_norm(heads(ks), k_gain[1]), heads(vs)], axis=2)
    win_rows = jnp.stack([rms_norm(heads(kw), k_gain[2]), heads(vw)], axis=2)
    gates = jax.nn.sigmoid(g.reshape(B, T, NSA_KV_HEADS, G_NSA, 3))
    qs = heads(qs).reshape(B, T, SB_KV_HEADS, G_SB, HEAD_DIM)
    sb_rows = jnp.stack([heads(ksb), heads(vsb)], axis=2)
    return q, gates, qs, cmp_rows, slc_rows, win_rows, sb_rows


def nsa_cmp_kv(cmp_rows, W, j):
    kc = rms_norm(compress_rows(cmp_rows[:, :, 0], W['nsa_cmp_pos'][j, 0], W['nsa_cmp_w1'][j, 0], W['nsa_cmp_w2'][j, 0]), W['nsa_k_norm'][j, 0])
    vc = compress_rows(cmp_rows[:, :, 1], W['nsa_cmp_pos'][j, 1], W['nsa_cmp_w1'][j, 1], W['nsa_cmp_w2'][j, 1])
    return kc, vc


def even_attend(q, gates, qs, q_pos, kc, vc, ks_b, vs_b, kw, vw, kw_pos, ksb, vsb, tb):
    B, Tq = q.shape[:2]
    o_a = nsa_attend(q, gates, q_pos, kc, vc, ks_b, vs_b, kw, vw, kw_pos, tb)
    o_b = stick_breaking(qs, q_pos, ksb, vsb)
    return jnp.concatenate([o_a.reshape(B, Tq, -1), o_b.reshape(B, Tq, -1)], axis=-1)


def even_prompt(a, W, j):
    B, T, _ = a.shape
    q, gates, qs, cmp_rows, slc_rows, win_rows, sb_rows = even_project(a, W['w_in_even'][j], W['nsa_q_norm'][j], W['nsa_k_norm'][j])
    kc, vc = nsa_cmp_kv(cmp_rows, W, j)
    ks_b = to_blocks(slc_rows[:, :, 0], SEL_BLOCK)
    vs_b = to_blocks(slc_rows[:, :, 1], SEL_BLOCK)
    front = ((0, 0), (WINDOW, 0), (0, 0), (0, 0))
    kw_pad = jnp.pad(win_rows[:, :, 0], front)
    vw_pad = jnp.pad(win_rows[:, :, 1], front)
    ksb, vsb = sb_rows[:, :, 0], sb_rows[:, :, 1]

    def block(i):
        q0 = i * Q_BLOCK
        cut = lambda t: lax.dynamic_slice_in_dim(t, q0, Q_BLOCK, axis=1)
        band = lambda t: lax.dynamic_slice_in_dim(t, q0, WINDOW + Q_BLOCK, axis=1)
        q_pos = q0 + jnp.arange(Q_BLOCK)
        kw_pos = q0 - WINDOW + jnp.arange(WINDOW + Q_BLOCK)
        return even_attend(cut(q), cut(gates), cut(qs), q_pos, kc, vc, ks_b, vs_b, band(kw_pad), band(vw_pad), kw_pos, ksb, vsb, W['rel_bias_table'])

    o = lax.map(block, jnp.arange(T // Q_BLOCK))
    o = jnp.moveaxis(o, 0, 1).reshape(B, T, EVEN_MIX)
    return o @ W['w_out_even'][j], (cmp_rows, slc_rows, win_rows[:, T - min(WINDOW, T):], sb_rows)


def even_sample(a, W, j, c_cmp, c_slc, c_win, c_sb, page_table):
    B, T, _ = a.shape
    q, gates, qs, cmp_rows, slc_rows, win_rows, sb_rows = even_project(a, W['w_in_even'][j], W['nsa_q_norm'][j], W['nsa_k_norm'][j])
    cmp_full = jnp.concatenate([gather_pages(c_cmp, page_table), cmp_rows], axis=1)
    slc_full = jnp.concatenate([gather_pages(c_slc, page_table), slc_rows], axis=1)
    sb_full = jnp.concatenate([gather_pages(c_sb, page_table), sb_rows], axis=1)
    win_full = jnp.concatenate([c_win, win_rows], axis=1)
    past = cmp_full.shape[1] - T
    win_buf = c_win.shape[1]
    kc, vc = nsa_cmp_kv(cmp_full, W, j)
    q_pos = past + jnp.arange(T)
    kw_pos = past - win_buf + jnp.arange(win_full.shape[1])
    o = even_attend(q, gates, qs, q_pos, kc, vc, to_blocks(slc_full[:, :, 0], SEL_BLOCK), to_blocks(slc_full[:, :, 1], SEL_BLOCK),
                    win_full[:, :, 0], win_full[:, :, 1], kw_pos, sb_full[:, :, 0], sb_full[:, :, 1], W['rel_bias_table'])
    return o.reshape(B, T, EVEN_MIX) @ W['w_out_even'][j], (cmp_rows, slc_rows, win_full[:, win_full.shape[1] - win_buf:], sb_rows)


def odd_project(a, w_in, q_gain, k_gain):
    B, T, _ = a.shape
    q, k, v = split_cols(a @ w_in, ODD_SIZES)
    q = rms_norm(q.reshape(B, T, MOBA_HEADS, HEAD_DIM), q_gain).reshape(B, T, MOBA_KV_HEADS, G_MOBA, HEAD_DIM)
    rows = jnp.stack([rms_norm(k.reshape(B, T, MOBA_KV_HEADS, HEAD_DIM), k_gain), v.reshape(B, T, MOBA_KV_HEADS, HEAD_DIM)], axis=2)
    return q, rows


def moba_keys(rows):
    kb = to_blocks(rows[:, :, 0], MOBA_BLOCK)
    vb = to_blocks(rows[:, :, 1], MOBA_BLOCK)
    return kb, vb, jnp.mean(kb.astype(jnp.float32), axis=3)


def odd_prompt(a, W, j):
    B, T, _ = a.shape
    q, rows = odd_project(a, W['w_in_odd'][j], W['moba_q_norm'][j], W['moba_k_norm'][j])
    kb, vb, kmean = moba_keys(rows)

    def block(i):
        q0 = i * MOBA_Q_BLOCK
        qb = lax.dynamic_slice_in_dim(q, q0, MOBA_Q_BLOCK, axis=1)
        return moba_attend(qb, q0 + jnp.arange(MOBA_Q_BLOCK), kb, vb, kmean, W['rel_bias_table'])

    o = lax.map(block, jnp.arange(T // MOBA_Q_BLOCK))
    o = jnp.moveaxis(o, 0, 1).reshape(B, T, ODD_MIX)
    return o @ W['w_out_odd'][j], (rows,)


def odd_sample(a, W, j, c_moba, page_table):
    B, T, _ = a.shape
    q, rows = odd_project(a, W['w_in_odd'][j], W['moba_q_norm'][j], W['moba_k_norm'][j])
    full = jnp.concatenate([gather_pages(c_moba, page_table), rows], axis=1)
    past = full.shape[1] - T
    kb, vb, kmean = moba_keys(full)
    o = moba_attend(q, past + jnp.arange(T), kb, vb, kmean, W['rel_bias_table'])
    return o.reshape(B, T, ODD_MIX) @ W['w_out_odd'][j], (rows,)


def swiglu(m, w_gu, w_down):
    g, u = jnp.split(m @ w_gu, 2, axis=-1)
    return (jax.nn.silu(g) * u) @ w_down


def moe_ffn(m, w_r, b_r, w_gu, w_down):
    logits = (m @ w_r).astype(jnp.float32) + b_r.astype(jnp.float32)
    top_v, top_i = lax.top_k(logits, TOP_K)
    wts = jax.nn.softmax(top_v, axis=-1)
    gate = jnp.sum(wts[..., None] * jax.nn.one_hot(top_i, N_EXPERTS, dtype=jnp.float32), axis=-2)
    y = jnp.zeros_like(m)
    for e in range(N_EXPERTS):
        y = y + gate[..., e:e + 1].astype(m.dtype) * swiglu(m, w_gu[e], w_down[e])
    return y


def trunk(x, p, W, mixer):
    h = x
    even_states, odd_states = [], []
    for li in range(DEPTH):
        j = li // 2
        o, st = mixer(li, rms_norm(h, W['norm_mix'][li]))
        h = h + o
        m = rms_norm(h, W['norm_ffn'][li])
        if li % 2 == 0:
            h = h + swiglu(m, W['w_ffn_gu'][j], W['w_ffn_down'][j])
            even_states.append(st)
        else:
            h = h + moe_ffn(m, W['w_router'][j], W['b_router'][j], W['w_moe_gu'][j], W['w_moe_down'][j])
            odd_states.append(st)
        gate = jax.nn.sigmoid(rms_norm(h, W['norm_pe'][li]) @ W['w_pe_gate'][li])
        h = h + (p[li] @ W['w_pe_in'][li]) * gate
    return h, even_states, odd_states


def stack_state(states, k):
    return jnp.stack([s[k] for s in states])


def setup_inputs(seed: int = 0) -> dict:
    key = jax.random.key(seed)
    ks = iter(jax.random.split(key, 48))
    nrm = lambda shape, scale: jax.random.normal(next(ks), shape, jnp.float32) * scale
    gain = lambda shape: 1.0 + nrm(shape, 0.01)
    n_pages = PAST_LEN // PAGE_SIZE
    used = DEC_BATCH * n_pages
    n_phys = used + used // 4
    win_buf = min(WINDOW, PAST_LEN)
    inp = {}
    inp['x_prompt'] = nrm((BATCH, SEQ, D_MODEL), 1.0)
    inp['x_sample'] = nrm((DEC_BATCH, DEC_SEQ, D_MODEL), 1.0)
    inp['cache_nsa_cmp'] = nrm((N_EVEN, n_phys, PAGE_SIZE, 2, NSA_KV_HEADS, HEAD_DIM), 1.0)
    inp['cache_nsa_slc'] = nrm((N_EVEN, n_phys, PAGE_SIZE, 2, NSA_KV_HEADS, HEAD_DIM), 1.0)
    inp['cache_nsa_win'] = nrm((N_EVEN, DEC_BATCH, win_buf, 2, NSA_KV_HEADS, HEAD_DIM), 1.0)
    inp['cache_sb'] = nrm((N_EVEN, n_phys, PAGE_SIZE, 2, SB_KV_HEADS, HEAD_DIM), 1.0)
    inp['cache_moba'] = nrm((N_ODD, n_phys, PAGE_SIZE, 2, MOBA_KV_HEADS, HEAD_DIM), 1.0)
    inp['page_table'] = jax.random.permutation(next(ks), n_phys)[:used].reshape(DEC_BATCH, n_pages).astype(jnp.int32)
    inp['p_prompt'] = nrm((DEPTH, BATCH, SEQ, P_DIM), 1.0)
    inp['p_sample'] = nrm((DEPTH, DEC_BATCH, DEC_SEQ, P_DIM), 1.0)
    inp['rel_bias_table'] = nrm((N_BUCKETS, N_MIX_HEADS), 0.5)
    inp['norm_mix'] = gain((DEPTH, D_MODEL))
    inp['norm_ffn'] = gain((DEPTH, D_MODEL))
    inp['norm_pe'] = gain((DEPTH, D_MODEL))
    inp['w_pe_in'] = nrm((DEPTH, P_DIM, D_MODEL), P_DIM ** -0.5)
    inp['w_pe_gate'] = nrm((DEPTH, D_MODEL, D_MODEL), D_MODEL ** -0.5)
    inp['w_in_even'] = nrm((N_EVEN, D_MODEL, EVEN_IN), D_MODEL ** -0.5)
    inp['w_out_even'] = nrm((N_EVEN, EVEN_MIX, D_MODEL), EVEN_MIX ** -0.5)
    inp['nsa_q_norm'] = gain((N_EVEN, HEAD_DIM))
    inp['nsa_k_norm'] = gain((N_EVEN, 3, HEAD_DIM))
    inp['nsa_cmp_pos'] = nrm((N_EVEN, 2, CMP_LEN, HEAD_DIM), 0.1)
    inp['nsa_cmp_w1'] = nrm((N_EVEN, 2, CMP_LEN, HEAD_DIM, HEAD_DIM), (CMP_LEN * HEAD_DIM) ** -0.5)
    inp['nsa_cmp_w2'] = nrm((N_EVEN, 2, HEAD_DIM, HEAD_DIM), HEAD_DIM ** -0.5)
    inp['w_ffn_gu'] = nrm((N_EVEN, D_MODEL, 2 * D_FF), D_MODEL ** -0.5)
    inp['w_ffn_down'] = nrm((N_EVEN, D_FF, D_MODEL), D_FF ** -0.5)
    inp['w_in_odd'] = nrm((N_ODD, D_MODEL, ODD_IN), D_MODEL ** -0.5)
    inp['w_out_odd'] = nrm((N_ODD, ODD_MIX, D_MODEL), ODD_MIX ** -0.5)
    inp['moba_q_norm'] = gain((N_ODD, HEAD_DIM))
    inp['moba_k_norm'] = gain((N_ODD, HEAD_DIM))
    inp['w_router'] = nrm((N_ODD, D_MODEL, N_EXPERTS), D_MODEL ** -0.5)
    inp['b_router'] = nrm((N_ODD, N_EXPERTS), 0.01)
    inp['w_moe_gu'] = nrm((N_ODD, N_EXPERTS, D_MODEL, 2 * MOE_D_FF), D_MODEL ** -0.5)
    inp['w_moe_down'] = nrm((N_ODD, N_EXPERTS, MOE_D_FF, D_MODEL), MOE_D_FF ** -0.5)
    return inp


def reference(x_prompt, x_sample, cache_nsa_cmp, cache_nsa_slc, cache_nsa_win, cache_sb, cache_moba, page_table,
              p_prompt, p_sample, rel_bias_table, norm_mix, norm_ffn, norm_pe, w_pe_in, w_pe_gate,
              w_in_even, w_out_even, nsa_q_norm, nsa_k_norm, nsa_cmp_pos, nsa_cmp_w1, nsa_cmp_w2,
              w_ffn_gu, w_ffn_down, w_in_odd, w_out_odd, moba_q_norm, moba_k_norm,
              w_router, b_router, w_moe_gu, w_moe_down):
    W = {'rel_bias_table': rel_bias_table, 'norm_mix': norm_mix, 'norm_ffn': norm_ffn, 'norm_pe': norm_pe,
         'w_pe_in': w_pe_in, 'w_pe_gate': w_pe_gate, 'w_in_even': w_in_even, 'w_out_even': w_out_even,
         'nsa_q_norm': nsa_q_norm, 'nsa_k_norm': nsa_k_norm, 'nsa_cmp_pos': nsa_cmp_pos, 'nsa_cmp_w1': nsa_cmp_w1,
         'nsa_cmp_w2': nsa_cmp_w2, 'w_ffn_gu': w_ffn_gu, 'w_ffn_down': w_ffn_down, 'w_in_odd': w_in_odd,
         'w_out_odd': w_out_odd, 'moba_q_norm': moba_q_norm, 'moba_k_norm': moba_k_norm, 'w_router': w_router,
         'b_router': b_router, 'w_moe_gu': w_moe_gu, 'w_moe_down': w_moe_down}

    def mixer_prompt(li, a):
        if li % 2 == 0:
            return even_prompt(a, W, li // 2)
        return odd_prompt(a, W, li // 2)

    def mixer_sample(li, a):
        j = li // 2
        if li % 2 == 0:
            return even_sample(a, W, j, cache_nsa_cmp[j], cache_nsa_slc[j], cache_nsa_win[j], cache_sb[j], page_table)
        return odd_sample(a, W, j, cache_moba[j], page_table)

    y_prompt, pe, po = trunk(x_prompt, p_prompt, W, mixer_prompt)
    y_sample, se, so = trunk(x_sample, p_sample, W, mixer_sample)
    return (y_prompt, y_sample,
            stack_state(pe, 0), stack_state(se, 0),
            stack_state(pe, 1), stack_state(se, 1),
            stack_state(pe, 2), stack_state(se, 2),
            stack_state(pe, 3), stack_state(se, 3),
            stack_state(po, 0), stack_state(so, 0))
```

```python
import functools
import math

import numpy as np
import jax
import jax.numpy as jnp
from jax import lax
from jax.experimental import pallas as pl
from jax.experimental.pallas import tpu as pltpu

F32 = jnp.float32
BF16 = jnp.bfloat16

HEAD_DIM = 64
NSA_HEADS = 8
NSA_KV_HEADS = 2
G_NSA = 4
SB_HEADS = 8
SB_KV_HEADS = 4
G_SB = 2
MOBA_HEADS = 16
MOBA_KV_HEADS = 4
G_MOBA = 4
CMP_LEN = 32
CMP_STRIDE = 16
SEL_BLOCK = 64
N_SEL = 16
WINDOW = 512
MOBA_BLOCK = 256
MOBA_TOPK = 3
N_BUCKETS = 32
REL_MAX_DIST = 1024
N_EXPERTS = 8
PAGE_SIZE = 128
SCALE = HEAD_DIM ** -0.5
NEG_INF = -1e30
SEL_BIG = 1e9
EPS = 1e-6

LANES = 128
SUBLANES = 8
NBLK_PAD = LANES
TQ = 256
TK = 256
SB_TQ = 512
SB_CUT = 110.0
VMEM_LIMIT = 56 * 1024 * 1024


def _bucket_thresholds():
    n = np.arange(0, 2 * REL_MAX_DIST, dtype=np.float64)
    exact = N_BUCKETS // 2
    nf = np.maximum(n, 1.0)
    big = exact + (np.log(nf / exact) / math.log(REL_MAX_DIST / exact) * (N_BUCKETS - exact)).astype(np.int32)
    bucket = np.where(n < exact, n.astype(np.int32), np.minimum(big, N_BUCKETS - 1))
    return tuple(int(np.argmax(bucket >= k)) for k in range(1, N_BUCKETS))


BUCKET_THR = _bucket_thresholds()
SAT_DIST = BUCKET_THR[-1]


def _cparams(*sem):
    return pltpu.CompilerParams(dimension_semantics=sem, vmem_limit_bytes=VMEM_LIMIT)


def _row_tile(m, cap):
    t = cap
    while m % t:
        t //= 2
    return t


def _split2(x):
    hi = x.astype(BF16)
    lo = (x - hi.astype(F32)).astype(BF16)
    return hi, lo


def _dot(a, b):
    return jnp.dot(a, b, preferred_element_type=F32)


def _dot_nt(a, b):
    return lax.dot_general(a, b, (((1,), (1,)), ((), ())), preferred_element_type=F32)


def _dot_nt_f32(a, b):
    ah, al = _split2(a)
    bh, bl = _split2(b)
    return _dot_nt(ah, bh) + _dot_nt(ah, bl) + _dot_nt(al, bh)


def _rms(x, g):
    return x * lax.rsqrt(jnp.mean(x * x, axis=-1, keepdims=True) + EPS) * g


def _bias_from_dist(dist, tb_ref, h):
    val = jnp.full(dist.shape, tb_ref[0, h], F32)
    for k in range(1, N_BUCKETS):
        val = jnp.where(dist >= BUCKET_THR[k - 1], tb_ref[k, h], val)
    return val


def _norm_matmul_kernel(x_ref, g_ref, w_ref, o_ref):
    xn = _rms(x_ref[...], g_ref[...]).astype(BF16)
    o_ref[...] = _dot(xn, w_ref[...])


def _norm_matmul(x, g, w):
    m, d = x.shape
    n = w.shape[1]
    tm = _row_tile(m, 512)
    return pl.pallas_call(
        _norm_matmul_kernel,
        grid=(m // tm,),
        in_specs=[pl.BlockSpec((tm, d), lambda i: (i, 0)),
                  pl.BlockSpec((1, d), lambda i: (0, 0)),
                  pl.BlockSpec((d, n), lambda i: (0, 0))],
        out_specs=pl.BlockSpec((tm, n), lambda i: (i, 0)),
        out_shape=jax.ShapeDtypeStruct((m, n), F32),
        compiler_params=_cparams("parallel"),
    )(x, g.reshape(1, d), w)


def _matmul_res_kernel(x_ref, w_ref, r_ref, o_ref):
    o_ref[...] = r_ref[...] + _dot(x_ref[...], w_ref[...])


def _matmul_res(x, w, res):
    m, k = x.shape
    n = w.shape[1]
    tm = _row_tile(m, 512)
    return pl.pallas_call(
        _matmul_res_kernel,
        grid=(m // tm,),
        in_specs=[pl.BlockSpec((tm, k), lambda i: (i, 0)),
                  pl.BlockSpec((k, n), lambda i: (0, 0)),
                  pl.BlockSpec((tm, n), lambda i: (i, 0))],
        out_specs=pl.BlockSpec((tm, n), lambda i: (i, 0)),
        out_shape=jax.ShapeDtypeStruct((m, n), F32),
        compiler_params=_cparams("parallel"),
    )(x, w, res)


def _headnorm_kernel(x_ref, g_ref, f_ref, o_ref):
    ncol = x_ref.shape[1]
    lane = lax.broadcasted_iota(jnp.int32, (1, LANES), 1)
    left = lane < HEAD_DIM
    for c in range(ncol // LANES):
        sl = slice(c * LANES, (c + 1) * LANES)
        x = x_ref[:, sl]
        sq = x * x
        s_left = jnp.sum(jnp.where(left, sq, 0.0), axis=-1, keepdims=True)
        s_all = jnp.sum(sq, axis=-1, keepdims=True)
        ms = jnp.where(left, s_left, s_all - s_left) * (1.0 / HEAD_DIM)
        y = x * lax.rsqrt(ms + EPS) * g_ref[:, sl]
        o_ref[:, sl] = jnp.where(f_ref[:, sl] > 0.5, y, x)


def _headnorm(x, ncol, gain, flag):
    m = x.shape[0]
    tm = _row_tile(m, 512)
    return pl.pallas_call(
        _headnorm_kernel,
        grid=(m // tm,),
        in_specs=[pl.BlockSpec((tm, ncol), lambda i: (i, 0)),
                  pl.BlockSpec((1, ncol), lambda i: (0, 0)),
                  pl.BlockSpec((1, ncol), lambda i: (0, 0))],
        out_specs=pl.BlockSpec((tm, ncol), lambda i: (i, 0)),
        out_shape=jax.ShapeDtypeStruct((m, ncol), F32),
        compiler_params=_cparams("parallel"),
    )(x, gain.reshape(1, ncol), flag.reshape(1, ncol))


def _ffn_kernel(h_ref, g_ref, wg_ref, wu_ref, wd_ref, o_ref, xn_scr):
    f = pl.program_id(1)

    @pl.when(f == 0)
    def _():
        xn_scr[...] = _rms(h_ref[...], g_ref[...]).astype(BF16)
        o_ref[...] = h_ref[...]

    xn = xn_scr[...]
    gate = _dot(xn, wg_ref[...])
    up = _dot(xn, wu_ref[...])
    act = (gate * jax.nn.sigmoid(gate) * up).astype(BF16)
    o_ref[...] += _dot(act, wd_ref[...])


def _ffn(h, g, w_gu, w_down, tf):
    m, d = h.shape
    fdim = w_down.shape[0]
    nf = fdim // tf
    tm = _row_tile(m, 1024)
    return pl.pallas_call(
        _ffn_kernel,
        grid=(m // tm, nf),
        in_specs=[pl.BlockSpec((tm, d), lambda i, f: (i, 0)),
                  pl.BlockSpec((1, d), lambda i, f: (0, 0)),
                  pl.BlockSpec((d, tf), lambda i, f: (0, f)),
                  pl.BlockSpec((d, tf), lambda i, f: (0, f + nf)),
                  pl.BlockSpec((tf, d), lambda i, f: (f, 0))],
        out_specs=pl.BlockSpec((tm, d), lambda i, f: (i, 0)),
        out_shape=jax.ShapeDtypeStruct((m, d), F32),
        scratch_shapes=[pltpu.VMEM((tm, d), BF16)],
        compiler_params=_cparams("parallel", "arbitrary"),
    )(h, g.reshape(1, d), w_gu, w_gu, w_down)


def _router_kernel(h_ref, g_ref, w_ref, b_ref, o_ref):
    xn = _rms(h_ref[...], g_ref[...])
    xh, xl = _split2(xn)
    w = w_ref[...]
    wh, wl = _split2(w)
    logits = _dot(xh, wh) + _dot(xh, wl) + _dot(xl, wh) + b_ref[...]
    lane = lax.broadcasted_iota(jnp.int32, logits.shape, 1)
    logits = jnp.where(lane < N_EXPERTS, logits, -jnp.inf)
    v1 = jnp.max(logits, axis=-1, keepdims=True)
    i1 = jnp.min(jnp.where(logits == v1, lane, LANES), axis=-1, keepdims=True)
    rest = jnp.where(lane == i1, -jnp.inf, logits)
    v2 = jnp.max(rest, axis=-1, keepdims=True)
    i2 = jnp.min(jnp.where(rest == v2, lane, LANES), axis=-1, keepdims=True)
    e2 = jnp.exp(v2 - v1)
    w1 = 1.0 / (1.0 + e2)
    w2 = e2 / (1.0 + e2)
    o_ref[...] = jnp.where(lane == i1, w1, 0.0) + jnp.where(lane == i2, w2, 0.0)


def _router(h, g, w_r, b_r):
    m, d = h.shape
    tm = _row_tile(m, 512)
    w_pad = jnp.pad(w_r, ((0, 0), (0, LANES - N_EXPERTS)))
    b_pad = jnp.pad(b_r, (0, LANES - N_EXPERTS)).reshape(1, LANES)
    return pl.pallas_call(
        _router_kernel,
        grid=(m // tm,),
        in_specs=[pl.BlockSpec((tm, d), lambda i: (i, 0)),
                  pl.BlockSpec((1, d), lambda i: (0, 0)),
                  pl.BlockSpec((d, LANES), lambda i: (0, 0)),
                  pl.BlockSpec((1, LANES), lambda i: (0, 0))],
        out_specs=pl.BlockSpec((tm, LANES), lambda i: (i, 0)),
        out_shape=jax.ShapeDtypeStruct((m, LANES), F32),
        compiler_params=_cparams("parallel"),
    )(h, g.reshape(1, d), w_pad, b_pad)


def _moe_kernel(h_ref, g_ref, gate_ref, wg_ref, wu_ref, wd_ref, o_ref, xn_scr):
    e = pl.program_id(1)
    f = pl.program_id(2)

    @pl.when((e == 0) & (f == 0))
    def _():
        xn_scr[...] = _rms(h_ref[...], g_ref[...]).astype(BF16)
        o_ref[...] = h_ref[...]

    xn = xn_scr[...]
    gate = _dot(xn, wg_ref[0])
    up = _dot(xn, wu_ref[0])
    act = (gate * jax.nn.sigmoid(gate) * up * gate_ref[0]).astype(BF16)
    o_ref[...] += _dot(act, wd_ref[0])


def _moe(h, g, gates_t, w_gu, w_down, tf):
    m, d = h.shape
    ne, fdim, _ = w_down.shape
    nf = fdim // tf
    tm = _row_tile(m, 1024)
    return pl.pallas_call(
        _moe_kernel,
        grid=(m // tm, ne, nf),
        in_specs=[pl.BlockSpec((tm, d), lambda i, e, f: (i, 0)),
                  pl.BlockSpec((1, d), lambda i, e, f: (0, 0)),
                  pl.BlockSpec((1, tm, 1), lambda i, e, f: (e, i, 0)),
                  pl.BlockSpec((1, d, tf), lambda i, e, f: (e, 0, f)),
                  pl.BlockSpec((1, d, tf), lambda i, e, f: (e, 0, f + nf)),
                  pl.BlockSpec((1, tf, d), lambda i, e, f: (e, f, 0))],
        out_specs=pl.BlockSpec((tm, d), lambda i, e, f: (i, 0)),
        out_shape=jax.ShapeDtypeStruct((m, d), F32),
        scratch_shapes=[pltpu.VMEM((tm, d), BF16)],
        compiler_params=_cparams("parallel", "arbitrary", "arbitrary"),
    )(h, g.reshape(1, d), gates_t, w_gu, w_gu, w_down)


def _pe_kernel(h_ref, g_ref, p_ref, wg_ref, wi_ref, o_ref):
    h = h_ref[...]
    hn = _rms(h, g_ref[...]).astype(BF16)
    gate = jax.nn.sigmoid(_dot(hn, wg_ref[...]))
    o_ref[...] = h + _dot(p_ref[...], wi_ref[...]) * gate


def _pe_update(h, g, p, w_gate, w_in):
    m, d = h.shape
    pd = p.shape[1]
    tm = _row_tile(m, 512)
    return pl.pallas_call(
        _pe_kernel,
        grid=(m // tm,),
        in_specs=[pl.BlockSpec((tm, d), lambda i: (i, 0)),
                  pl.BlockSpec((1, d), lambda i: (0, 0)),
                  pl.BlockSpec((tm, pd), lambda i: (i, 0)),
                  pl.BlockSpec((d, d), lambda i: (0, 0)),
                  pl.BlockSpec((pd, d), lambda i: (0, 0))],
        out_specs=pl.BlockSpec((tm, d), lambda i: (i, 0)),
        out_shape=jax.ShapeDtypeStruct((m, d), F32),
        compiler_params=_cparams("parallel"),
    )(h, g.reshape(1, d), p, w_gate, w_in)


def _bias_kernel(tb_ref, qp_ref, kp_ref, o_ref, *, win):
    h = pl.program_id(0)
    kp = kp_ref[...]
    dist = qp_ref[...] - kp
    ok = (kp >= 0) & (dist >= 0)
    if win is not None:
        ok = ok & (dist < win)
    o_ref[0] = jnp.where(ok, _bias_from_dist(dist, tb_ref, h), NEG_INF)


def _bias_table(tb, n_heads, qpos, kpos, win=None):
    r, c = qpos.shape[0], kpos.shape[0]
    rb = _row_tile(r, 256)
    return pl.pallas_call(
        functools.partial(_bias_kernel, win=win),
        grid=(n_heads, r // rb),
        in_specs=[pl.BlockSpec(memory_space=pltpu.SMEM),
                  pl.BlockSpec((rb, 1), lambda h, i: (i, 0)),
                  pl.BlockSpec((1, c), lambda h, i: (0, 0))],
        out_specs=pl.BlockSpec((1, rb, c), lambda h, i: (h, i, 0)),
        out_shape=jax.ShapeDtypeStruct((n_heads, r, c), F32),
        compiler_params=_cparams("parallel", "parallel"),
    )(tb, jnp.asarray(qpos, jnp.int32).reshape(r, 1), jnp.asarray(kpos, jnp.int32).reshape(1, c))


def _tile_bias(tb, n_kv, g, nd, win=None):
    t = _bias_table(tb, n_kv * g, np.arange(nd * TQ), np.arange(TK), win)
    return t.reshape(n_kv, g, nd, TQ, TK).transpose(0, 2, 1, 3, 4)


def _rank_nsel(score, sc_scr, nloop, kth, thresh):
    nb, n = score.shape
    sc_scr[...] = score
    out = []
    for c in range(n // LANES):
        sl = slice(c * LANES, (c + 1) * LANES)
        sc = score[:, sl]
        blk = lax.broadcasted_iota(jnp.int32, sc.shape, 0)

        def body(jg, cnt, sc=sc, blk=blk, sl=sl):
            grp = sc_scr[pl.ds(pl.multiple_of(jg * SUBLANES, SUBLANES), SUBLANES), sl]
            for r in range(SUBLANES):
                row = grp[r:r + 1, :]
                beats = (row > sc) | ((row == sc) & (jg * SUBLANES + r < blk))
                cnt = cnt + jnp.where(beats, 1.0, 0.0)
            return cnt

        cnt = lax.fori_loop(0, (nloop + SUBLANES - 1) // SUBLANES, body, jnp.zeros(sc.shape, F32))
        sel = (cnt < kth) & (sc > thresh)
        out.append(jnp.where(sel, 0.0, 1.0))
    return out


def _cmp_mlp_kernel(x_ref, w1a_ref, w1b_ref, pos_ref, w1_ref, w2_ref, g_ref, o_ref, *, norm):
    nb, nc, _ = x_ref.shape
    x = x_ref[...].reshape(nb * nc, x_ref.shape[2])
    a = _dot(x, w1a_ref[0])
    b = _dot(x, w1b_ref[0])
    ph, plo = _split2(pos_ref[0])
    wh, wl = _split2(w1_ref[0])
    posb = (_dot(ph, wh) + _dot(ph, wl) + _dot(plo, wh))[0:1]
    z = a + pltpu.roll(b, nb * nc - 1, 0) + posb
    y = _dot(jax.nn.gelu(z).astype(BF16), w2_ref[0])
    if norm:
        y = _rms(y, g_ref[...])
    o_ref[...] = y.reshape(nb, nc, HEAD_DIM)


def _cmp_mlp(x, w1, pos, w2, gain, kv, norm):
    s, nc, cw = x.shape
    nb = _row_tile(s, max(1, 1024 // nc))
    half = CMP_STRIDE * HEAD_DIM
    w1f = w1.reshape(2, CMP_LEN * HEAD_DIM, HEAD_DIM)
    w1a = w1f[:, :half].astype(BF16)
    w1b = w1f[:, half:].astype(BF16)
    posf = jnp.broadcast_to(pos.reshape(2, 1, CMP_LEN * HEAD_DIM), (2, 8, CMP_LEN * HEAD_DIM))
    return pl.pallas_call(
        functools.partial(_cmp_mlp_kernel, norm=norm),
        grid=(s // nb,),
        in_specs=[pl.BlockSpec((nb, nc, cw), lambda i: (i, 0, 0)),
                  pl.BlockSpec((1, half, HEAD_DIM), lambda i: (kv, 0, 0)),
                  pl.BlockSpec((1, half, HEAD_DIM), lambda i: (kv, 0, 0)),
                  pl.BlockSpec((1, 8, 2 * half), lambda i: (kv, 0, 0)),
                  pl.BlockSpec((1, 2 * half, HEAD_DIM), lambda i: (kv, 0, 0)),
                  pl.BlockSpec((1, HEAD_DIM, HEAD_DIM), lambda i: (kv, 0, 0)),
                  pl.BlockSpec((1, HEAD_DIM), lambda i: (0, 0))],
        out_specs=pl.BlockSpec((nb, nc, HEAD_DIM), lambda i: (i, 0, 0)),
        out_shape=jax.ShapeDtypeStruct((s, nc, HEAD_DIM), F32),
        compiler_params=_cparams("parallel"),
    )(x, w1a, w1b, posf, w1f, w2.astype(BF16), gain.reshape(1, HEAD_DIM))


def _nsa_cmp_kernel(tb_ref, q_ref, kc_ref, vc_ref, c2s_ref, gate_ref, o_ref, nsel_ref, s_scr, sc_scr, *, n_kv):
    g, tq = q_ref.shape[1], q_ref.shape[2]
    ncp = kc_ref.shape[1]
    r = g * tq
    kvh = pl.program_id(0) % n_kv
    t0 = pl.program_id(1) * tq
    q = q_ref[0].reshape(r, HEAD_DIM)
    for cc in range(ncp // LANES):
        c0 = cc * LANES
        sl = slice(c0, c0 + LANES)
        min_kp = CMP_STRIDE * c0 + CMP_LEN - 1
        max_kp = CMP_STRIDE * (c0 + LANES - 1) + CMP_LEN - 1
        future = min_kp > t0 + (tq - 1)
        sat = t0 - max_kp >= SAT_DIST

        @pl.when(future)
        def _():
            s_scr[:, sl] = jnp.full((r, LANES), NEG_INF, F32)

        @pl.when(sat)
        def _():
            s = _dot_nt(q, kc_ref[0, sl, :]).reshape(g, tq, LANES)
            for gi in range(g):
                s_scr[gi * tq:(gi + 1) * tq, sl] = s[gi] + tb_ref[N_BUCKETS - 1, kvh * g + gi]

        @pl.when(jnp.logical_not(future | sat))
        def _():
            s = _dot_nt(q, kc_ref[0, sl, :]).reshape(g, tq, LANES)
            t = t0 + lax.broadcasted_iota(jnp.int32, (tq, LANES), 0)
            kp = CMP_STRIDE * (c0 + lax.broadcasted_iota(jnp.int32, (tq, LANES), 1)) + (CMP_LEN - 1)
            dist = t - kp
            for gi in range(g):
                b = _bias_from_dist(dist, tb_ref, kvh * g + gi)
                s_scr[gi * tq:(gi + 1) * tq, sl] = jnp.where(dist >= 0, s[gi] + b, NEG_INF)

    s = s_scr[...]
    m = jnp.max(s, axis=-1, keepdims=True)
    e = jnp.where(s > 0.5 * NEG_INF, jnp.exp(s - m), 0.0)
    p = e / jnp.maximum(jnp.sum(e, axis=-1, keepdims=True), 1e-30)
    o = _dot(p.astype(BF16), vc_ref[0])
    gate = jax.nn.sigmoid(gate_ref[0].reshape(r, 1))
    o_ref[0] = (gate * o).reshape(g, tq, HEAD_DIM)

    psum = jnp.sum(p.reshape(g, tq, ncp), axis=0)
    ph, plo = _split2(psum)
    c2s = c2s_ref[...]
    imp_t = _dot_nt(c2s, ph) + _dot_nt(c2s, plo)
    blk = lax.broadcasted_iota(jnp.int32, (NBLK_PAD, tq), 0)
    t = t0 + lax.broadcasted_iota(jnp.int32, (NBLK_PAD, tq), 1)
    valid = SEL_BLOCK * blk <= t
    forced = (blk == t // SEL_BLOCK) | (blk == 0)
    score = jnp.where(forced, SEL_BIG, jnp.where(valid, imp_t, -SEL_BIG))
    nloop = jnp.minimum((t0 + tq - 1) // SEL_BLOCK + 1, NBLK_PAD)
    cols = _rank_nsel(score, sc_scr, nloop, N_SEL, -jnp.inf)
    for c, nsel_t in enumerate(cols):
        nsel_ref[0, c * LANES:(c + 1) * LANES, :] = nsel_t.T.astype(BF16)


def _nsa_cmp(tb, q, kc, vc, c2s_t, gate, n_kv):
    bg, g, t, _ = q.shape
    ncp = kc.shape[1]
    tq = min(TQ, t)
    return pl.pallas_call(
        functools.partial(_nsa_cmp_kernel, n_kv=n_kv),
        grid=(bg, t // tq),
        in_specs=[pl.BlockSpec(memory_space=pltpu.SMEM),
                  pl.BlockSpec((1, g, tq, HEAD_DIM), lambda b, i: (b, 0, i, 0)),
                  pl.BlockSpec((1, ncp, HEAD_DIM), lambda b, i: (b, 0, 0)),
                  pl.BlockSpec((1, ncp, HEAD_DIM), lambda b, i: (b, 0, 0)),
                  pl.BlockSpec((NBLK_PAD, ncp), lambda b, i: (0, 0)),
                  pl.BlockSpec((1, g, tq, 1), lambda b, i: (b, 0, i, 0))],
        out_specs=[pl.BlockSpec((1, g, tq, HEAD_DIM), lambda b, i: (b, 0, i, 0)),
                   pl.BlockSpec((1, tq, NBLK_PAD), lambda b, i: (b, i, 0))],
        out_shape=[jax.ShapeDtypeStruct((bg, g, t, HEAD_DIM), F32),
                   jax.ShapeDtypeStruct((bg, t, NBLK_PAD), BF16)],
        scratch_shapes=[pltpu.VMEM((g * tq, ncp), F32), pltpu.VMEM((NBLK_PAD, tq), F32)],
        compiler_params=_cparams("parallel", "parallel"),
    )(tb, q, kc, vc, c2s_t, gate)


def _flash_kernel(*refs, look, gated):
    if gated:
        q_ref, k_ref, v_ref, b_ref, acc_ref, gate_ref, o_ref, m_scr, acc_scr = refs
    else:
        q_ref, k_ref, v_ref, b_ref, o_ref, m_scr, acc_scr = refs
    g, tq, dk = q_ref.shape[1], q_ref.shape[2], q_ref.shape[3]
    nd = b_ref.shape[1]
    r = g * tq
    i = pl.program_id(1)
    q = q_ref[0].reshape(r, dk)
    m_scr[...] = jnp.full((r, 1), NEG_INF, F32)
    acc_scr[...] = jnp.zeros((r, LANES), F32)
    lo = 0 if look is None else jnp.maximum(i - look, 0)

    def body(j, carry):
        off = pl.multiple_of(j * TK, TK)
        s = _dot_nt(q, k_ref[0, pl.ds(off, TK), :])
        d = jnp.minimum(i - j, nd - 1)
        s = (s.reshape(g, tq, TK) + b_ref[0, d]).reshape(r, TK)
        m_prev = m_scr[...]
        m_new = jnp.maximum(m_prev, jnp.max(s, axis=-1, keepdims=True))
        alpha = jnp.exp(m_prev - m_new)
        p = jnp.exp(s - m_new).astype(BF16)
        acc_scr[...] = alpha * acc_scr[...] + _dot(p, v_ref[0, pl.ds(off, TK), :])
        m_scr[...] = m_new
        return carry

    lax.fori_loop(lo, i + 1, body, 0)
    acc = acc_scr[...]
    o = acc[:, :HEAD_DIM] / jnp.maximum(acc[:, HEAD_DIM:HEAD_DIM + 1], 1e-30)
    if gated:
        o = acc_ref[0].reshape(r, HEAD_DIM) + jax.nn.sigmoid(gate_ref[0].reshape(r, 1)) * o
    o_ref[0] = o.reshape(g, tq, HEAD_DIM)


def _flash_attn(q, k, v1, bias, n_kv, look=None, acc=None, gate=None):
    bg, g, t, dk = q.shape
    nd = bias.shape[1]
    gated = acc is not None
    qspec = pl.BlockSpec((1, g, TQ, dk), lambda b, i: (b, 0, i, 0))
    ospec = pl.BlockSpec((1, g, TQ, HEAD_DIM), lambda b, i: (b, 0, i, 0))
    in_specs = [qspec,
                pl.BlockSpec((1, t, dk), lambda b, i: (b, 0, 0)),
                pl.BlockSpec((1, t, LANES), lambda b, i: (b, 0, 0)),
                pl.BlockSpec((1, nd, g, TQ, TK), lambda b, i: (b % n_kv, 0, 0, 0, 0))]
    args = [q, k, v1, bias]
    if gated:
        in_specs += [ospec, pl.BlockSpec((1, g, TQ, 1), lambda b, i: (b, 0, i, 0))]
        args += [acc, gate]
    return pl.pallas_call(
        functools.partial(_flash_kernel, look=look, gated=gated),
        grid=(bg, t // TQ),
        in_specs=in_specs,
        out_specs=ospec,
        out_shape=jax.ShapeDtypeStruct((bg, g, t, HEAD_DIM), F32),
        scratch_shapes=[pltpu.VMEM((g * TQ, 1), F32), pltpu.VMEM((g * TQ, LANES), F32)],
        compiler_params=_cparams("parallel", "parallel"),
    )(*args)


def _sb_kernel(q_ref, k_ref, v_ref, u_ref, o_ref, carry_scr, acc_scr):
    g, tq = q_ref.shape[1], q_ref.shape[2]
    r = g * tq
    i = pl.program_id(1)
    q = q_ref[0].reshape(r, HEAD_DIM)
    carry_scr[...] = jnp.zeros((r, 1), F32)
    acc_scr[...] = jnp.zeros((r, HEAD_DIM), F32)
    t = i * tq + lax.broadcasted_iota(jnp.int32, (g, tq, TK), 1).reshape(r, TK)
    lane = lax.broadcasted_iota(jnp.int32, (r, TK), 1)
    u = u_ref[...]

    def cond(st):
        return (st[0] >= 0) & (st[1] > 0)

    def body(st):
        j = st[0]
        off = pl.multiple_of(j * TK, TK)
        z = _dot_nt(q, k_ref[0, pl.ds(off, TK), :])
        before = (off + lane) < t
        sp = jnp.maximum(z, 0.0) + jnp.log1p(jnp.exp(-jnp.abs(z)))
        lk = jnp.where(before, -sp, 0.0)
        hi, lo = _split2(lk)
        carry = carry_scr[...]
        between = _dot(hi, u) + _dot(lo, u) + carry
        a = jnp.where(before, jnp.exp(z - sp + between), 0.0)
        acc_scr[...] += _dot(a.astype(BF16), v_ref[0, pl.ds(off, TK), :])
        carry = carry + jnp.sum(lk, axis=-1, keepdims=True)
        carry_scr[...] = carry
        return j - 1, (jnp.max(carry) > -SB_CUT).astype(jnp.int32)

    lax.while_loop(cond, body, (((i + 1) * tq - 1) // TK, jnp.int32(1)))
    o_ref[0] = acc_scr[...].reshape(g, tq, HEAD_DIM)


def _sb_attn(q, k, v):
    bg, g, t, _ = q.shape
    tq = min(SB_TQ, t)
    idx = np.arange(TK)
    u = jnp.asarray(idx[:, None] > idx[None, :], BF16)
    return pl.pallas_call(
        _sb_kernel,
        grid=(bg, t // tq),
        in_specs=[pl.BlockSpec((1, g, tq, HEAD_DIM), lambda b, i: (b, 0, i, 0)),
                  pl.BlockSpec((1, t, HEAD_DIM), lambda b, i: (b, 0, 0)),
                  pl.BlockSpec((1, t, HEAD_DIM), lambda b, i: (b, 0, 0)),
                  pl.BlockSpec((TK, TK), lambda b, i: (0, 0))],
        out_specs=pl.BlockSpec((1, g, tq, HEAD_DIM), lambda b, i: (b, 0, i, 0)),
        out_shape=jax.ShapeDtypeStruct((bg, g, t, HEAD_DIM), F32),
        scratch_shapes=[pltpu.VMEM((g * tq, 1), F32), pltpu.VMEM((g * tq, HEAD_DIM), F32)],
        compiler_params=_cparams("parallel", "parallel"),
    )(q, k, v, u)


def _moba_gate_kernel(q_ref, k_ref, nsel_ref, km_scr, sc_scr):
    g, tq = q_ref.shape[1], q_ref.shape[2]
    t_all = k_ref.shape[1]
    nb = t_all // MOBA_BLOCK
    i = pl.program_id(1)

    @pl.when(i == 0)
    def _():
        km_scr[...] = jnp.zeros(km_scr.shape, F32)
        km_scr[0:nb, :] = jnp.sum(k_ref[0].reshape(nb, MOBA_BLOCK, HEAD_DIM), axis=1) * (1.0 / MOBA_BLOCK)

    km = km_scr[...]
    blk = lax.broadcasted_iota(jnp.int32, (NBLK_PAD, tq), 0)
    own = (i * tq + lax.broadcasted_iota(jnp.int32, (NBLK_PAD, tq), 1)) // MOBA_BLOCK
    nloop = jnp.minimum((i * tq + tq - 1) // MOBA_BLOCK + 1, NBLK_PAD)
    for gi in range(g):
        gate_t = _dot_nt_f32(km, q_ref[0, gi])
        score = jnp.where(blk == own, SEL_BIG, jnp.where(blk < own, gate_t, -SEL_BIG))
        cols = _rank_nsel(score, sc_scr, nloop, MOBA_TOPK + 1, -SEL_BIG / 2)
        for c, nsel_t in enumerate(cols):
            nsel_ref[0, gi, c * LANES:(c + 1) * LANES, :] = nsel_t.T.astype(BF16)


def _moba_gate(q, k):
    bg, g, t, _ = q.shape
    tq = min(TQ, t)
    return pl.pallas_call(
        _moba_gate_kernel,
        grid=(bg, t // tq),
        in_specs=[pl.BlockSpec((1, g, tq, HEAD_DIM), lambda b, i: (b, 0, i, 0)),
                  pl.BlockSpec((1, t, HEAD_DIM), lambda b, i: (b, 0, 0))],
        out_specs=pl.BlockSpec((1, g, tq, NBLK_PAD), lambda b, i: (b, 0, i, 0)),
        out_shape=jax.ShapeDtypeStruct((bg, g, t, NBLK_PAD), BF16),
        scratch_shapes=[pltpu.VMEM((NBLK_PAD, HEAD_DIM), F32), pltpu.VMEM((NBLK_PAD, tq), F32)],
        compiler_params=_cparams("parallel", "arbitrary"),
    )(q, k)


def _block_onehot_neg(t, block):
    oh = (np.arange(t)[:, None] // block) == np.arange(NBLK_PAD)[None, :]
    return jnp.asarray(np.where(oh, NEG_INF, 0.0), BF16)


def _with_ones(v):
    ones = jnp.ones(v.shape[:-1] + (1,), BF16)
    zeros = jnp.zeros(v.shape[:-1] + (LANES - HEAD_DIM - 1,), BF16)
    return jnp.concatenate([v.astype(BF16), ones, zeros], axis=-1)


def _cmp_to_sel_t(ncp, nc, nsb):
    cs = CMP_STRIDE * np.arange(ncp)[None, :]
    ss = SEL_BLOCK * np.arange(NBLK_PAD)[:, None]
    ov = np.clip(np.minimum(cs + CMP_LEN, ss + SEL_BLOCK) - np.maximum(cs, ss), 0, None) / CMP_LEN
    ov = np.where((np.arange(ncp)[None, :] < nc) & (np.arange(NBLK_PAD)[:, None] < nsb), ov, 0.0)
    return jnp.asarray(ov, BF16)


def _heads_major(x, b, t, n_kv, g):
    return x.reshape(b, t, n_kv, g, HEAD_DIM).transpose(0, 2, 3, 1, 4).reshape(b * n_kv, g, t, HEAD_DIM)


def _kv_major(x, b, t, n_kv):
    return x.reshape(b, t, n_kv, HEAD_DIM).transpose(0, 2, 1, 3).reshape(b * n_kv, t, HEAD_DIM)


def _heads_minor(o, b, t, n_kv, g):
    return o.reshape(b, n_kv, g, t, HEAD_DIM).transpose(0, 3, 1, 2, 4).reshape(b * t, n_kv * g * HEAD_DIM)


_E_Q, _E_KV, _E_G, _E_QS, _E_SB = 512, 768, 24, 512, 512
_E_NORM_COLS = _E_Q + _E_KV
_E_QS_OFF = _E_NORM_COLS
_E_SB_OFF = _E_QS_OFF + _E_QS
_E_G_OFF = _E_SB_OFF + _E_SB


def _even_weights(w_in, q_gain, k_gain):
    o = 0
    q = w_in[:, o:o + _E_Q]; o += _E_Q
    kv = w_in[:, o:o + _E_KV]; o += _E_KV
    gt = w_in[:, o:o + _E_G]; o += _E_G
    qs = w_in[:, o:o + _E_QS]; o += _E_QS
    sb = w_in[:, o:o + _E_SB]
    w = jnp.concatenate([q, kv, qs, sb, jnp.pad(gt, ((0, 0), (0, LANES - _E_G)))], axis=1).astype(BF16)
    one = jnp.ones((HEAD_DIM,), F32)
    gain = jnp.concatenate([jnp.tile(q_gain, NSA_HEADS)] + [one] * 4 + [k_gain[1]] * 2 + [one] * 2 + [k_gain[2]] * 2 + [one] * 2)
    flag = jnp.asarray(np.concatenate([np.ones(_E_Q), np.zeros(256), np.ones(128), np.zeros(128), np.ones(128), np.zeros(128)]), F32)
    return w, gain, flag


def _odd_weights(q_gain, k_gain):
    ncol = (MOBA_HEADS + MOBA_KV_HEADS) * HEAD_DIM
    gain = jnp.concatenate([jnp.tile(q_gain, MOBA_HEADS), jnp.tile(k_gain, MOBA_KV_HEADS)])
    return gain, jnp.ones((ncol,), F32)


def _even_prompt(h, b, t, P, j, li):
    w_in, gain, flag = _even_weights(P['w_in_even'][j], P['nsa_q_norm'][j], P['nsa_k_norm'][j])
    proj = _norm_matmul(h, P['norm_mix'][li], w_in)
    nrm = _headnorm(proj, _E_NORM_COLS, gain, flag)
    tb = P['rel_bias_table']
    hkv, g = NSA_KV_HEADS, G_NSA
    kvw = hkv * HEAD_DIM
    cmp_rows = nrm[:, 512:768].reshape(b, t, 2, hkv, HEAD_DIM)
    slc_rows = nrm[:, 768:1024].reshape(b, t, 2, hkv, HEAD_DIM)
    win_rows = nrm[:, 1024:1280].reshape(b, t, 2, hkv, HEAD_DIM)
    sb_rows = proj[:, _E_SB_OFF:_E_SB_OFF + _E_SB].reshape(b, t, 2, SB_KV_HEADS, HEAD_DIM)

    q = (_heads_major(nrm[:, :512], b, t, hkv, g) * SCALE).astype(BF16)
    gates = proj[:, _E_G_OFF:_E_G_OFF + _E_G].reshape(b, t, hkv, g, 3).transpose(4, 0, 2, 3, 1).reshape(3, b * hkv, g, t, 1)

    ncp = t // CMP_STRIDE
    nc = ncp - 1
    chunks = lambda x: _kv_major(x, b, t, hkv).reshape(b * hkv, ncp, CMP_STRIDE * HEAD_DIM).astype(BF16)
    w1, pos, w2 = P['nsa_cmp_w1'][j], P['nsa_cmp_pos'][j], P['nsa_cmp_w2'][j]
    kc = _cmp_mlp(chunks(nrm[:, 512:640]), w1, pos, w2, P['nsa_k_norm'][j, 0], 0, True).astype(BF16)
    vc = _cmp_mlp(chunks(nrm[:, 640:768]), w1, pos, w2, P['nsa_k_norm'][j, 0], 1, False).astype(BF16)
    nsb = t // SEL_BLOCK
    o, nsel = _nsa_cmp(tb, q, kc, vc, _cmp_to_sel_t(ncp, nc, nsb), gates[0], hkv)

    nd_full = -(-(SAT_DIST + TK - 1) // TQ) + 1
    bias_c = _tile_bias(tb, hkv, g, nd_full)
    q_aug = jnp.concatenate([q, jnp.broadcast_to(nsel[:, None], (b * hkv, g, t, NBLK_PAD))], axis=-1)
    ks = _kv_major(nrm[:, 768:896], b, t, hkv).astype(BF16)
    k_aug = jnp.concatenate([ks, jnp.broadcast_to(_block_onehot_neg(t, SEL_BLOCK)[None], (b * hkv, t, NBLK_PAD))], axis=-1)
    vs = _with_ones(_kv_major(nrm[:, 896:1024], b, t, hkv))
    o = _flash_attn(q_aug, k_aug, vs, bias_c, hkv, acc=o, gate=gates[1])

    look = (WINDOW - 1 + TK - 1) // TK
    bias_w = _tile_bias(tb, hkv, g, look + 1, win=WINDOW)
    kw = _kv_major(nrm[:, 1024:1152], b, t, hkv).astype(BF16)
    vw = _with_ones(_kv_major(nrm[:, 1152:1280], b, t, hkv))
    o = _flash_attn(q, kw, vw, bias_w, hkv, look=look, acc=o, gate=gates[2])
    o_a = _heads_minor(o, b, t, hkv, g)

    qs = (_heads_major(proj[:, _E_QS_OFF:_E_QS_OFF + _E_QS], b, t, SB_KV_HEADS, G_SB) * SCALE).astype(BF16)
    ksb = _kv_major(proj[:, _E_SB_OFF:_E_SB_OFF + 256], b, t, SB_KV_HEADS).astype(BF16)
    vsb = _kv_major(proj[:, _E_SB_OFF + 256:_E_SB_OFF + 512], b, t, SB_KV_HEADS).astype(BF16)
    o_b = _heads_minor(_sb_attn(qs, ksb, vsb), b, t, SB_KV_HEADS, G_SB)

    mix = jnp.concatenate([o_a, o_b], axis=-1).astype(BF16)
    h = _matmul_res(mix, P['w_out_even'][j].astype(BF16), h)
    wb = min(WINDOW, t)
    return h, (cmp_rows, slc_rows, win_rows[:, t - wb:], sb_rows)


def _odd_prompt(h, b, t, P, j, li):
    hkv, g = MOBA_KV_HEADS, G_MOBA
    gain, flag = _odd_weights(P['moba_q_norm'][j], P['moba_k_norm'][j])
    proj = _norm_matmul(h, P['norm_mix'][li], P['w_in_odd'][j].astype(BF16))
    ncol = (MOBA_HEADS + hkv) * HEAD_DIM
    nrm = _headnorm(proj, ncol, gain, flag)
    tb = P['rel_bias_table']
    qn = _heads_major(nrm[:, :1024], b, t, hkv, g)
    kn = _kv_major(nrm[:, 1024:1280], b, t, hkv)
    rows = jnp.stack([nrm[:, 1024:1280].reshape(b, t, hkv, HEAD_DIM), proj[:, 1280:1536].reshape(b, t, hkv, HEAD_DIM)], axis=2)
    nsel = _moba_gate(qn, kn)
    nd_full = -(-(SAT_DIST + TK - 1) // TQ) + 1
    bias = _tile_bias(tb, hkv, g, nd_full)
    q_aug = jnp.concatenate([(qn * SCALE).astype(BF16), nsel], axis=-1)
    k_aug = jnp.concatenate([kn.astype(BF16), jnp.broadcast_to(_block_onehot_neg(t, MOBA_BLOCK)[None], (b * hkv, t, NBLK_PAD))], axis=-1)
    v1 = _with_ones(_kv_major(proj[:, 1280:1536], b, t, hkv))
    o = _heads_minor(_flash_attn(q_aug, k_aug, v1, bias, hkv), b, t, hkv, g)
    h = _matmul_res(o.astype(BF16), P['w_out_odd'][j].astype(BF16), h)
    return h, (rows,)


def _ffn_and_pe(h, pemb, P, li):
    j = li // 2
    if li % 2 == 0:
        h = _ffn(h, P['norm_ffn'][li], P['w_ffn_gu'][j].astype(BF16), P['w_ffn_down'][j].astype(BF16), 256)
    else:
        gates = _router(h, P['norm_ffn'][li], P['w_router'][j], P['b_router'][j])
        gates_t = gates[:, :N_EXPERTS].T[:, :, None]
        h = _moe(h, P['norm_ffn'][li], gates_t, P['w_moe_gu'][j].astype(BF16), P['w_moe_down'][j].astype(BF16), 512)
    return _pe_update(h, P['norm_pe'][li], pemb.astype(BF16), P['w_pe_gate'][li].astype(BF16), P['w_pe_in'][li].astype(BF16))


def _trunk_prompt(x, p, P):
    b, t, d = x.shape
    h = x.reshape(b * t, d)
    even_states, odd_states = [], []
    for li in range(p.shape[0]):
        j = li // 2
        if li % 2 == 0:
            h, st = _even_prompt(h, b, t, P, j, li)
            even_states.append(st)
        else:
            h, st = _odd_prompt(h, b, t, P, j, li)
            odd_states.append(st)
        h = _ffn_and_pe(h, p[li].reshape(b * t, -1), P, li)
    return h.reshape(b, t, d), even_states, odd_states


GATHER_PAGES = 4
DEC_ROWS = 16


def _gather_kernel(pt_ref, *refs):
    o_ref = refs[-1]
    for k, x_ref in enumerate(refs[:-1]):
        o_ref[k] = x_ref[...]


def _gather_pages(pool, layer, page_table):
    db, npg = page_table.shape
    _, _, ps, two, hh, dd = pool.shape
    kp = GATHER_PAGES

    def in_map(k):
        return lambda b, p, pt: (layer, pt[b * npg + p * kp + k], 0, 0, 0, 0)

    grid_spec = pltpu.PrefetchScalarGridSpec(
        num_scalar_prefetch=1,
        grid=(db, npg // kp),
        in_specs=[pl.BlockSpec((None, None, ps, two, hh, dd), in_map(k)) for k in range(kp)],
        out_specs=pl.BlockSpec((None, kp, ps, two, hh, dd), lambda b, p, pt: (b, p, 0, 0, 0, 0)),
    )
    out = pl.pallas_call(
        _gather_kernel,
        grid_spec=grid_spec,
        out_shape=jax.ShapeDtypeStruct((db, npg, ps, two, hh, dd), pool.dtype),
        compiler_params=_cparams("parallel", "parallel"),
    )(page_table.reshape(-1), *([pool] * kp))
    return out.reshape(db, npg * ps, two, hh, dd)


def _dec_attn_kernel(*refs, g, aug, gated, imp):
    it = iter(refs)
    q_ref, k_ref, v_ref, b_ref = next(it), next(it), next(it), next(it)
    qa_ref, oh_ref = (next(it), next(it)) if aug else (None, None)
    acc_ref, gate_ref = (next(it), next(it)) if gated else (None, None)
    c2s_ref = next(it) if imp else None
    o_ref = next(it)
    imp_ref = next(it) if imp else None
    n_kv = q_ref.shape[1]
    for h in range(n_kv):
        s = _dot_nt(q_ref[0, h], k_ref[0, h].astype(BF16)) + b_ref[h]
        if aug:
            s = s + _dot(qa_ref[0, h], oh_ref[...])
        m = jnp.max(s, axis=-1, keepdims=True)
        e = jnp.where(s > 0.5 * NEG_INF, jnp.exp(s - m), 0.0)
        p = e / jnp.maximum(jnp.sum(e, axis=-1, keepdims=True), 1e-30)
        o = _dot(p.astype(BF16), v_ref[0, h].astype(BF16))
        if gated:
            o = acc_ref[0, h] + jax.nn.sigmoid(gate_ref[0, h]) * o
        o_ref[0, h] = o
        if imp:
            row = lax.broadcasted_iota(jnp.int32, p.shape, 0)
            psum = jnp.sum(jnp.where(row < g, p, 0.0), axis=0, keepdims=True)
            ph, plo = _split2(jnp.broadcast_to(psum, p.shape))
            imp_ref[0, h] = _dot_nt(ph, c2s_ref[...]) + _dot_nt(plo, c2s_ref[...])


def _dec_attn(q, k, v, bias, g, qa=None, onehot_t=None, acc=None, gate=None, c2s_t=None):
    db, n_kv, _, _ = q.shape
    tp = k.shape[2]
    aug, gated, imp = qa is not None, acc is not None, c2s_t is not None
    row4 = lambda w: pl.BlockSpec((1, n_kv, DEC_ROWS, w), lambda b: (b, 0, 0, 0))
    in_specs = [row4(HEAD_DIM),
                pl.BlockSpec((1, n_kv, tp, HEAD_DIM), lambda b: (b, 0, 0, 0)),
                pl.BlockSpec((1, n_kv, tp, HEAD_DIM), lambda b: (b, 0, 0, 0)),
                pl.BlockSpec((n_kv, DEC_ROWS, tp), lambda b: (0, 0, 0))]
    args = [q, k, v, bias]
    if aug:
        in_specs += [row4(NBLK_PAD), pl.BlockSpec((NBLK_PAD, tp), lambda b: (0, 0))]
        args += [qa, onehot_t]
    if gated:
        in_specs += [row4(HEAD_DIM), row4(1)]
        args += [acc, gate]
    if imp:
        in_specs += [pl.BlockSpec((NBLK_PAD, tp), lambda b: (0, 0))]
        args += [c2s_t]
    out_specs = [row4(HEAD_DIM)]
    out_shape = [jax.ShapeDtypeStruct((db, n_kv, DEC_ROWS, HEAD_DIM), F32)]
    if imp:
        out_specs.append(row4(NBLK_PAD))
        out_shape.append(jax.ShapeDtypeStruct((db, n_kv, DEC_ROWS, NBLK_PAD), F32))
    res = pl.pallas_call(
        functools.partial(_dec_attn_kernel, g=g, aug=aug, gated=gated, imp=imp),
        grid=(db,),
        in_specs=in_specs,
        out_specs=out_specs,
        out_shape=out_shape,
        compiler_params=_cparams("parallel"),
    )(*args)
    return res if imp else res[0]


def _dec_sb_kernel(q_ref, k_ref, v_ref, u_ref, o_ref, *, past):
    n_kv = q_ref.shape[1]
    u = u_ref[...]
    for h in range(n_kv):
        q = q_ref[0, h]
        carry = jnp.zeros((DEC_ROWS, 1), F32)
        acc = jnp.zeros((DEC_ROWS, HEAD_DIM), F32)
        for c in reversed(range(past // TK)):
            sl = slice(c * TK, (c + 1) * TK)
            z = _dot_nt(q, k_ref[0, h, sl, :].astype(BF16))
            sp = jnp.maximum(z, 0.0) + jnp.log1p(jnp.exp(-jnp.abs(z)))
            lk = -sp
            hi, lo = _split2(lk)
            between = _dot(hi, u) + _dot(lo, u) + carry
            a = jnp.exp(z - sp + between)
            acc = acc + _dot(a.astype(BF16), v_ref[0, h, sl, :].astype(BF16))
            carry = carry + jnp.sum(lk, axis=-1, keepdims=True)
        o_ref[0, h] = acc


def _dec_sb(q, k, v, past):
    db, n_kv, _, _ = q.shape
    tp = k.shape[2]
    idx = np.arange(TK)
    u = jnp.asarray(idx[:, None] > idx[None, :], BF16)
    row4 = pl.BlockSpec((1, n_kv, DEC_ROWS, HEAD_DIM), lambda b: (b, 0, 0, 0))
    kv4 = pl.BlockSpec((1, n_kv, tp, HEAD_DIM), lambda b: (b, 0, 0, 0))
    return pl.pallas_call(
        functools.partial(_dec_sb_kernel, past=past),
        grid=(db,),
        in_specs=[row4, kv4, kv4, pl.BlockSpec((TK, TK), lambda b: (0, 0))],
        out_specs=row4,
        out_shape=jax.ShapeDtypeStruct((db, n_kv, DEC_ROWS, HEAD_DIM), F32),
        compiler_params=_cparams("parallel"),
    )(q, k, v, u)


def _dec_moba_gate_kernel(q_ref, k_ref, o_ref, km_scr):
    n_kv, tp = k_ref.shape[1], k_ref.shape[2]
    nb = tp // MOBA_BLOCK
    for h in range(n_kv):
        km_scr[...] = jnp.zeros(km_scr.shape, F32)
        for n in range(nb):
            blk_rows = k_ref[0, h, n * MOBA_BLOCK:(n + 1) * MOBA_BLOCK, :]
            km_scr[n:n + 1, :] = jnp.sum(blk_rows, axis=0, keepdims=True) * (1.0 / MOBA_BLOCK)
        o_ref[0, h] = _dot_nt_f32(q_ref[0, h], km_scr[...])


def _dec_moba_gate(q, k):
    db, n_kv, _, _ = q.shape
    tp = k.shape[2]
    return pl.pallas_call(
        _dec_moba_gate_kernel,
        grid=(db,),
        in_specs=[pl.BlockSpec((1, n_kv, DEC_ROWS, HEAD_DIM), lambda b: (b, 0, 0, 0)),
                  pl.BlockSpec((1, n_kv, tp, HEAD_DIM), lambda b: (b, 0, 0, 0))],
        out_specs=pl.BlockSpec((1, n_kv, DEC_ROWS, NBLK_PAD), lambda b: (b, 0, 0, 0)),
        out_shape=jax.ShapeDtypeStruct((db, n_kv, DEC_ROWS, NBLK_PAD), F32),
        scratch_shapes=[pltpu.VMEM((NBLK_PAD, HEAD_DIM), F32)],
        compiler_params=_cparams("parallel"),
    )(q, k)


def _rank_kernel(s_ref, o_ref, sc_scr, *, qpos, block, forced0, kth, thresh):
    n = s_ref.shape[2]
    blk = lax.broadcasted_iota(jnp.int32, (NBLK_PAD, n), 0)
    own = qpos // block
    past_score = jnp.where(blk * block <= qpos, s_ref[0], -SEL_BIG)
    forced = ((blk == own) | (blk == 0)) if forced0 else (blk == own)
    score = jnp.where(forced, SEL_BIG, past_score)
    cols = _rank_nsel(score, sc_scr, min(own + 1, NBLK_PAD), kth, thresh)
    for c, nsel_t in enumerate(cols):
        o_ref[0, c * LANES:(c + 1) * LANES, :] = nsel_t.T.astype(BF16)


def _rank_select(score_t, qpos, block, forced0, kth, thresh):
    x, _, n = score_t.shape
    return pl.pallas_call(
        functools.partial(_rank_kernel, qpos=qpos, block=block, forced0=forced0, kth=kth, thresh=thresh),
        grid=(x,),
        in_specs=[pl.BlockSpec((1, NBLK_PAD, n), lambda i: (i, 0, 0))],
        out_specs=pl.BlockSpec((1, n, NBLK_PAD), lambda i: (i, 0, 0)),
        out_shape=jax.ShapeDtypeStruct((x, n, NBLK_PAD), BF16),
        scratch_shapes=[pltpu.VMEM((NBLK_PAD, n), F32)],
        compiler_params=_cparams("parallel"),
    )(score_t)


def _dec_rows(x, db, n_kv, g):
    x = x.reshape(db, n_kv, g, HEAD_DIM)
    return jnp.pad(x, ((0, 0), (0, 0), (0, DEC_ROWS - g), (0, 0)))


def _dec_bias(tb, n_kv, g, qpos, kpos, win=None):
    t = _bias_table(tb, n_kv * g, np.full(DEC_ROWS, qpos), kpos, win)[:, 0]
    return jnp.pad(t.reshape(n_kv, g, -1), ((0, 0), (0, DEC_ROWS - g), (0, 0)))


def _dec_kv(rows_cache, rows_new, tp):
    full = jnp.concatenate([rows_cache, rows_new], axis=1)
    full = jnp.pad(full, ((0, 0), (0, tp - full.shape[1]), (0, 0), (0, 0), (0, 0)))
    full = full.transpose(2, 0, 3, 1, 4)
    return full[0], full[1]


def _dec_heads_minor(o, db, n_kv, g):
    return o[:, :, :g].reshape(db, n_kv * g * HEAD_DIM)


def _even_sample(h, P, j, li, c_cmp, c_slc, c_win, c_sb, page_table):
    db = h.shape[0]
    w_in, gain, flag = _even_weights(P['w_in_even'][j], P['nsa_q_norm'][j], P['nsa_k_norm'][j])
    proj = _norm_matmul(h, P['norm_mix'][li], w_in)
    nrm = _headnorm(proj, _E_NORM_COLS, gain, flag)
    tb = P['rel_bias_table']
    hkv, g = NSA_KV_HEADS, G_NSA
    cmp_rows = nrm[:, 512:768].reshape(db, 1, 2, hkv, HEAD_DIM)
    slc_rows = nrm[:, 768:1024].reshape(db, 1, 2, hkv, HEAD_DIM)
    win_rows = nrm[:, 1024:1280].reshape(db, 1, 2, hkv, HEAD_DIM)
    sb_rows = proj[:, _E_SB_OFF:_E_SB_OFF + _E_SB].reshape(db, 1, 2, SB_KV_HEADS, HEAD_DIM)
    q = (_dec_rows(nrm[:, :512], db, hkv, g) * SCALE).astype(BF16)
    gates = jnp.pad(proj[:, _E_G_OFF:_E_G_OFF + _E_G].reshape(db, hkv, g, 3), ((0, 0), (0, 0), (0, DEC_ROWS - g), (0, 0)))
    gates = gates.transpose(3, 0, 1, 2)[..., None]

    past = page_table.shape[1] * PAGE_SIZE
    tp = past + MOBA_BLOCK
    g_cmp = _gather_pages(c_cmp, j, page_table)
    ncp = past // CMP_STRIDE
    nc = (past + 1 - CMP_LEN) // CMP_STRIDE + 1
    chunks = lambda kv: g_cmp[:, :, kv].transpose(0, 2, 1, 3).reshape(db * hkv, ncp, CMP_STRIDE * HEAD_DIM).astype(BF16)
    w1, pos, w2 = P['nsa_cmp_w1'][j], P['nsa_cmp_pos'][j], P['nsa_cmp_w2'][j]
    kc = _cmp_mlp(chunks(0), w1, pos, w2, P['nsa_k_norm'][j, 0], 0, True).reshape(db, hkv, ncp, HEAD_DIM)
    vc = _cmp_mlp(chunks(1), w1, pos, w2, P['nsa_k_norm'][j, 0], 1, False).reshape(db, hkv, ncp, HEAD_DIM)
    kpos_c = np.where(np.arange(ncp) < nc, CMP_STRIDE * np.arange(ncp) + CMP_LEN - 1, -1)
    nsb = -(-(past + 1) // SEL_BLOCK)
    o, imp = _dec_attn(q, kc, vc, _dec_bias(tb, hkv, g, past, kpos_c), g,
                       acc=jnp.zeros((db, hkv, DEC_ROWS, HEAD_DIM), F32), gate=gates[0], c2s_t=_cmp_to_sel_t(ncp, nc, nsb))
    nsel = _rank_select(imp[:, :, 0].transpose(1, 2, 0), past, SEL_BLOCK, True, N_SEL, -jnp.inf)
    qa = jnp.broadcast_to(nsel.transpose(1, 0, 2)[:, :, None], (db, hkv, DEC_ROWS, NBLK_PAD))

    kpos = np.arange(tp)
    ks, vs = _dec_kv(_gather_pages(c_slc, j, page_table), slc_rows, tp)
    oh_sel = _block_onehot_neg(tp, SEL_BLOCK).T
    o = _dec_attn(q, ks, vs, _dec_bias(tb, hkv, g, past, kpos), g, qa=qa, onehot_t=oh_sel, acc=o, gate=gates[1])

    wb = c_win.shape[2]
    wp = wb + LANES
    kw, vw = _dec_kv(c_win[j], win_rows, wp)
    kpos_w = np.where(np.arange(wp) <= wb, past - wb + np.arange(wp), -1)
    o = _dec_attn(q, kw, vw, _dec_bias(tb, hkv, g, past, kpos_w, win=WINDOW), g, acc=o, gate=gates[2])
    o_a = _dec_heads_minor(o, db, hkv, g)

    qs = (_dec_rows(proj[:, _E_QS_OFF:_E_QS_OFF + _E_QS], db, SB_KV_HEADS, G_SB) * SCALE).astype(BF16)
    ksb, vsb = _dec_kv(_gather_pages(c_sb, j, page_table), sb_rows, tp)
    o_b = _dec_heads_minor(_dec_sb(qs, ksb, vsb, past), db, SB_KV_HEADS, G_SB)

    mix = jnp.concatenate([o_a, o_b], axis=-1).astype(BF16)
    h = _matmul_res(mix, P['w_out_even'][j].astype(BF16), h)
    win_new = jnp.concatenate([c_win[j], win_rows], axis=1)[:, 1:]
    return h, (cmp_rows, slc_rows, win_new, sb_rows)


def _odd_sample(h, P, j, li, c_moba, page_table):
    db = h.shape[0]
    hkv, g = MOBA_KV_HEADS, G_MOBA
    gain, flag = _odd_weights(P['moba_q_norm'][j], P['moba_k_norm'][j])
    proj = _norm_matmul(h, P['norm_mix'][li], P['w_in_odd'][j].astype(BF16))
    ncol = (MOBA_HEADS + hkv) * HEAD_DIM
    nrm = _headnorm(proj, ncol, gain, flag)
    tb = P['rel_bias_table']
    rows = jnp.stack([nrm[:, 1024:1280].reshape(db, 1, hkv, HEAD_DIM), proj[:, 1280:1536].reshape(db, 1, hkv, HEAD_DIM)], axis=2)
    past = page_table.shape[1] * PAGE_SIZE
    tp = past + MOBA_BLOCK
    qn = _dec_rows(nrm[:, :1024], db, hkv, g)
    k, v = _dec_kv(_gather_pages(c_moba, j, page_table), rows, tp)
    gate = _dec_moba_gate(qn, k)
    score_t = gate[:, :, :g].transpose(1, 2, 3, 0).reshape(hkv * g, NBLK_PAD, db)
    nsel = _rank_select(score_t, past, MOBA_BLOCK, False, MOBA_TOPK + 1, -SEL_BIG / 2)
    qa = jnp.pad(nsel.reshape(hkv, g, db, NBLK_PAD).transpose(2, 0, 1, 3), ((0, 0), (0, 0), (0, DEC_ROWS - g), (0, 0)))
    oh = _block_onehot_neg(tp, MOBA_BLOCK).T
    o = _dec_attn((qn * SCALE).astype(BF16), k, v, _dec_bias(tb, hkv, g, past, np.arange(tp)), g, qa=qa, onehot_t=oh)
    h = _matmul_res(_dec_heads_minor(o, db, hkv, g).astype(BF16), P['w_out_odd'][j].astype(BF16), h)
    return h, (rows,)


def _trunk_sample(x, p, P, caches, page_table):
    db, t, d = x.shape
    h = x.reshape(db * t, d)
    c_cmp, c_slc, c_win, c_sb, c_moba = caches
    even_states, odd_states = [], []
    for li in range(p.shape[0]):
        j = li // 2
        if li % 2 == 0:
            h, st = _even_sample(h, P, j, li, c_cmp, c_slc, c_win, c_sb, page_table)
            even_states.append(st)
        else:
            h, st = _odd_sample(h, P, j, li, c_moba, page_table)
            odd_states.append(st)
        h = _ffn_and_pe(h, p[li].reshape(db * t, -1), P, li)
    return h.reshape(db, t, d), even_states, odd_states


def kernel(x_prompt, x_sample, cache_nsa_cmp, cache_nsa_slc, cache_nsa_win, cache_sb, cache_moba, page_table,
           p_prompt, p_sample, rel_bias_table, norm_mix, norm_ffn, norm_pe, w_pe_in, w_pe_gate,
           w_in_even, w_out_even, nsa_q_norm, nsa_k_norm, nsa_cmp_pos, nsa_cmp_w1, nsa_cmp_w2,
           w_ffn_gu, w_ffn_down, w_in_odd, w_out_odd, moba_q_norm, moba_k_norm,
           w_router, b_router, w_moe_gu, w_moe_down):
    P = {'rel_bias_table': rel_bias_table, 'norm_mix': norm_mix, 'norm_ffn': norm_ffn, 'norm_pe': norm_pe,
         'w_pe_in': w_pe_in, 'w_pe_gate': w_pe_gate, 'w_in_even': w_in_even, 'w_out_even': w_out_even,
         'nsa_q_norm': nsa_q_norm, 'nsa_k_norm': nsa_k_norm, 'nsa_cmp_pos': nsa_cmp_pos, 'nsa_cmp_w1': nsa_cmp_w1,
         'nsa_cmp_w2': nsa_cmp_w2, 'w_ffn_gu': w_ffn_gu, 'w_ffn_down': w_ffn_down, 'w_in_odd': w_in_odd,
         'w_out_odd': w_out_odd, 'moba_q_norm': moba_q_norm, 'moba_k_norm': moba_k_norm, 'w_router': w_router,
         'b_router': b_router, 'w_moe_gu': w_moe_gu, 'w_moe_down': w_moe_down}
    y_prompt, pe, po = _trunk_prompt(x_prompt, p_prompt, P)
    caches = (cache_nsa_cmp, cache_nsa_slc, cache_nsa_win, cache_sb, cache_moba)
    y_sample, se, so = _trunk_sample(x_sample, p_sample, P, caches, page_table)
    stack = lambda states, k: jnp.stack([s[k] for s in states])
    return (y_prompt, y_sample,
            stack(pe, 0), stack(se, 0), stack(pe, 1), stack(se, 1),
            stack(pe, 2), stack(se, 2), stack(pe, 3), stack(se, 3),
            stack(po, 0), stack(so, 0))
```

```python
import functools
import math

import numpy as np
import jax
import jax.numpy as jnp
from jax import lax
from jax.experimental import pallas as pl
from jax.experimental.pallas import tpu as pltpu

F32 = jnp.float32
BF16 = jnp.bfloat16

HEAD_DIM = 64
NSA_HEADS = 8
NSA_KV_HEADS = 2
G_NSA = 4
SB_HEADS = 8
SB_KV_HEADS = 4
G_SB = 2
MOBA_HEADS = 16
MOBA_KV_HEADS = 4
G_MOBA = 4
CMP_LEN = 32
CMP_STRIDE = 16
SEL_BLOCK = 64
N_SEL = 16
WINDOW = 512
MOBA_BLOCK = 256
MOBA_TOPK = 3
N_BUCKETS = 32
REL_MAX_DIST = 1024
N_EXPERTS = 8
PAGE_SIZE = 128
SCALE = HEAD_DIM ** -0.5
NEG_INF = -1e30
SEL_BIG = 1e9
EPS = 1e-6

LANES = 128
SUBLANES = 8
NBLK_PAD = LANES
TQ = 256
TK = 256
SB_TQ = 512
SB_CUT = 110.0
DEC_ROWS = 16
VMEM_LIMIT = 56 * 1024 * 1024


def _bucket_thresholds():
    n = np.arange(0, 2 * REL_MAX_DIST, dtype=np.float64)
    exact = N_BUCKETS // 2
    nf = np.maximum(n, 1.0)
    big = exact + (np.log(nf / exact) / math.log(REL_MAX_DIST / exact) * (N_BUCKETS - exact)).astype(np.int32)
    bucket = np.where(n < exact, n.astype(np.int32), np.minimum(big, N_BUCKETS - 1))
    return tuple(int(np.argmax(bucket >= k)) for k in range(1, N_BUCKETS))


BUCKET_THR = _bucket_thresholds()
SAT_DIST = BUCKET_THR[-1]


def _cparams(*sem):
    return pltpu.CompilerParams(dimension_semantics=sem, vmem_limit_bytes=VMEM_LIMIT)


def _row_tile(m, cap):
    t = cap
    while m % t:
        t //= 2
    return t


def _split2(x):
    hi = x.astype(BF16)
    lo = (x - hi.astype(F32)).astype(BF16)
    return hi, lo


def _split3(x):
    hi = x.astype(BF16)
    r = x - hi.astype(F32)
    mid = r.astype(BF16)
    lo = (r - mid.astype(F32)).astype(BF16)
    return hi, mid, lo


def _dot(a, b):
    return jnp.dot(a, b, preferred_element_type=F32)


def _dot_nt(a, b):
    return lax.dot_general(a, b, (((1,), (1,)), ((), ())), preferred_element_type=F32)


def _rms(x, g):
    return x * lax.rsqrt(jnp.mean(x * x, axis=-1, keepdims=True) + EPS) * g


def _bias_from_dist(dist, tb_ref, h):
    val = jnp.full(dist.shape, tb_ref[0, h], F32)
    for k in range(1, N_BUCKETS):
        val = jnp.where(dist >= BUCKET_THR[k - 1], tb_ref[k, h], val)
    return val


def _left_half():
    return lax.broadcasted_iota(jnp.int32, (1, LANES), 1) < HEAD_DIM


def _half_rms_inv(x, both):
    left = _left_half()
    sq = x * x
    s_left = jnp.sum(jnp.where(left, sq, 0.0), axis=-1, keepdims=True)
    inv_l = lax.rsqrt(s_left * (1.0 / HEAD_DIM) + EPS)
    if both:
        s_all = jnp.sum(sq, axis=-1, keepdims=True)
        inv_r = lax.rsqrt((s_all - s_left) * (1.0 / HEAD_DIM) + EPS)
    else:
        inv_r = 1.0
    return jnp.where(left, inv_l, inv_r)


def _proj_kernel(h_ref, g_ref, wr_ref, cs_ref, wt_ref, gt_ref, *outs, row_modes, row_outs, t_modes, t_outs):
    xn = _rms(h_ref[...], g_ref[...]).astype(BF16)
    y = _dot(xn, wr_ref[...])
    for (c0, c1), o_ref in zip(row_outs, outs[:len(row_outs)]):
        for c in range(c0, c1):
            sl = slice(c * LANES, (c + 1) * LANES)
            x = y[:, sl]
            mode = row_modes[c]
            if mode in ('norm', 'normk'):
                x = x * _half_rms_inv(x, mode == 'norm')
            x = x * cs_ref[:, sl]
            if mode == 'sigmoid':
                x = jax.nn.sigmoid(x)
            o_ref[:, (c - c0) * LANES:(c - c0 + 1) * LANES] = x.astype(o_ref.dtype)
    yt = _dot_nt(wt_ref[...], xn)
    for (r0, r1), o_ref in zip(t_outs, outs[len(row_outs):]):
        for r in range(r0, r1):
            sl = slice(r * HEAD_DIM, (r + 1) * HEAD_DIM)
            x = yt[sl, :]
            if t_modes[r] == 'norm':
                x = x * lax.rsqrt(jnp.mean(x * x, axis=0, keepdims=True) + EPS) * gt_ref[sl, :]
            o_ref[0, (r - r0) * HEAD_DIM:(r - r0 + 1) * HEAD_DIM, :] = x


def _proj(h, b, t, g, w_row, cs, row_modes, row_outs, row_dtypes, w_t, gt, t_modes, t_outs):
    n, d = h.shape
    tm = min(512, t)
    nt = t // tm
    cr, ct = w_row.shape[1], w_t.shape[0]
    row = lambda bi, i: (bi * nt + i, 0)
    const = lambda bi, i: (0, 0)
    out_specs = [pl.BlockSpec((tm, (c1 - c0) * LANES), row) for c0, c1 in row_outs]
    out_shape = [jax.ShapeDtypeStruct((n, (c1 - c0) * LANES), dt) for (c0, c1), dt in zip(row_outs, row_dtypes)]
    out_specs += [pl.BlockSpec((1, (r1 - r0) * HEAD_DIM, tm), lambda bi, i: (bi, 0, i)) for r0, r1 in t_outs]
    out_shape += [jax.ShapeDtypeStruct((b, (r1 - r0) * HEAD_DIM, t), F32) for r0, r1 in t_outs]
    return pl.pallas_call(
        functools.partial(_proj_kernel, row_modes=row_modes, row_outs=row_outs, t_modes=t_modes, t_outs=t_outs),
        grid=(b, nt),
        in_specs=[pl.BlockSpec((tm, d), row),
                  pl.BlockSpec((1, d), const),
                  pl.BlockSpec((d, cr), const),
                  pl.BlockSpec((1, cr), const),
                  pl.BlockSpec((ct, d), const),
                  pl.BlockSpec((ct, 1), const)],
        out_specs=out_specs,
        out_shape=out_shape,
        compiler_params=_cparams("parallel", "parallel"),
        name="proj",
    )(h, g.reshape(1, d), w_row, cs.reshape(1, cr), w_t, gt.reshape(ct, 1))


def _out_proj_kernel(*refs):
    *x_refs, w_ref, r_ref, o_ref = refs
    acc = r_ref[...]
    k0 = 0
    for x_ref in x_refs:
        k = x_ref.shape[1]
        acc = acc + _dot(x_ref[...].astype(BF16), w_ref[k0:k0 + k, :])
        k0 += k
    o_ref[...] = acc


def _out_proj(xs, w, res):
    m, n = res.shape
    tm = _row_tile(m, 512)
    return pl.pallas_call(
        _out_proj_kernel,
        grid=(m // tm,),
        in_specs=[pl.BlockSpec((tm, x.shape[1]), lambda i: (i, 0)) for x in xs]
        + [pl.BlockSpec(w.shape, lambda i: (0, 0)), pl.BlockSpec((tm, n), lambda i: (i, 0))],
        out_specs=pl.BlockSpec((tm, n), lambda i: (i, 0)),
        out_shape=jax.ShapeDtypeStruct((m, n), F32),
        compiler_params=_cparams("parallel"),
        name="out_proj",
    )(*xs, w, res)


def _ffn_kernel(h_ref, g_ref, wg_ref, wu_ref, wd_ref, o_ref, xn_scr):
    f = pl.program_id(1)

    @pl.when(f == 0)
    def _():
        xn_scr[...] = _rms(h_ref[...], g_ref[...]).astype(BF16)
        o_ref[...] = h_ref[...]

    xn = xn_scr[...]
    gate = _dot(xn, wg_ref[...])
    up = _dot(xn, wu_ref[...])
    act = (gate * jax.nn.sigmoid(gate) * up).astype(BF16)
    o_ref[...] += _dot(act, wd_ref[...])


def _ffn(h, g, w_gu, w_down, tf):
    m, d = h.shape
    fdim = w_down.shape[0]
    nf = fdim // tf
    tm = _row_tile(m, 1024)
    return pl.pallas_call(
        _ffn_kernel,
        grid=(m // tm, nf),
        in_specs=[pl.BlockSpec((tm, d), lambda i, f: (i, 0)),
                  pl.BlockSpec((1, d), lambda i, f: (0, 0)),
                  pl.BlockSpec((d, tf), lambda i, f: (0, f)),
                  pl.BlockSpec((d, tf), lambda i, f: (0, f + nf)),
                  pl.BlockSpec((tf, d), lambda i, f: (f, 0))],
        out_specs=pl.BlockSpec((tm, d), lambda i, f: (i, 0)),
        out_shape=jax.ShapeDtypeStruct((m, d), F32),
        scratch_shapes=[pltpu.VMEM((tm, d), BF16)],
        compiler_params=_cparams("parallel", "arbitrary"),
        name="ffn",
    )(h, g.reshape(1, d), w_gu, w_gu, w_down)


def _router_kernel(h_ref, g_ref, w_ref, b_ref, o_ref):
    xn = _rms(h_ref[...], g_ref[...])
    xh, xl = _split2(xn)
    wh, wl = _split2(w_ref[...])
    logits = _dot(xh, wh) + _dot(xh, wl) + _dot(xl, wh) + b_ref[...]
    lane = lax.broadcasted_iota(jnp.int32, logits.shape, 1)
    logits = jnp.where(lane < N_EXPERTS, logits, -jnp.inf)
    v1 = jnp.max(logits, axis=-1, keepdims=True)
    i1 = jnp.min(jnp.where(logits == v1, lane, LANES), axis=-1, keepdims=True)
    rest = jnp.where(lane == i1, -jnp.inf, logits)
    v2 = jnp.max(rest, axis=-1, keepdims=True)
    i2 = jnp.min(jnp.where(rest == v2, lane, LANES), axis=-1, keepdims=True)
    e2 = jnp.exp(v2 - v1)
    w1 = 1.0 / (1.0 + e2)
    w2 = e2 / (1.0 + e2)
    o_ref[...] = jnp.where(lane == i1, w1, 0.0) + jnp.where(lane == i2, w2, 0.0)


def _router(h, g, w_r, b_r):
    m, d = h.shape
    tm = _row_tile(m, 512)
    w_pad = jnp.pad(w_r, ((0, 0), (0, LANES - N_EXPERTS)))
    b_pad = jnp.pad(b_r, (0, LANES - N_EXPERTS)).reshape(1, LANES)
    return pl.pallas_call(
        _router_kernel,
        grid=(m // tm,),
        in_specs=[pl.BlockSpec((tm, d), lambda i: (i, 0)),
                  pl.BlockSpec((1, d), lambda i: (0, 0)),
                  pl.BlockSpec((d, LANES), lambda i: (0, 0)),
                  pl.BlockSpec((1, LANES), lambda i: (0, 0))],
        out_specs=pl.BlockSpec((tm, LANES), lambda i: (i, 0)),
        out_shape=jax.ShapeDtypeStruct((m, LANES), F32),
        compiler_params=_cparams("parallel"),
        name="router",
    )(h, g.reshape(1, d), w_pad, b_pad)


def _moe_kernel(h_ref, g_ref, gate_ref, wg_ref, wu_ref, wd_ref, o_ref, xn_scr):
    e = pl.program_id(1)
    f = pl.program_id(2)

    @pl.when((e == 0) & (f == 0))
    def _():
        xn_scr[...] = _rms(h_ref[...], g_ref[...]).astype(BF16)
        o_ref[...] = h_ref[...]

    xn = xn_scr[...]
    gate = _dot(xn, wg_ref[0])
    up = _dot(xn, wu_ref[0])
    act = (gate * jax.nn.sigmoid(gate) * up * gate_ref[0]).astype(BF16)
    o_ref[...] += _dot(act, wd_ref[0])


def _moe(h, g, gates_t, w_gu, w_down, tf):
    m, d = h.shape
    ne, fdim, _ = w_down.shape
    nf = fdim // tf
    tm = _row_tile(m, 1024)
    return pl.pallas_call(
        _moe_kernel,
        grid=(m // tm, ne, nf),
        in_specs=[pl.BlockSpec((tm, d), lambda i, e, f: (i, 0)),
                  pl.BlockSpec((1, d), lambda i, e, f: (0, 0)),
                  pl.BlockSpec((1, tm, 1), lambda i, e, f: (e, i, 0)),
                  pl.BlockSpec((1, d, tf), lambda i, e, f: (e, 0, f)),
                  pl.BlockSpec((1, d, tf), lambda i, e, f: (e, 0, f + nf)),
                  pl.BlockSpec((1, tf, d), lambda i, e, f: (e, f, 0))],
        out_specs=pl.BlockSpec((tm, d), lambda i, e, f: (i, 0)),
        out_shape=jax.ShapeDtypeStruct((m, d), F32),
        scratch_shapes=[pltpu.VMEM((tm, d), BF16)],
        compiler_params=_cparams("parallel", "arbitrary", "arbitrary"),
        name="moe",
    )(h, g.reshape(1, d), gates_t, w_gu, w_gu, w_down)


def _pe_kernel(h_ref, g_ref, p_ref, wg_ref, wi_ref, o_ref):
    h = h_ref[...]
    hn = _rms(h, g_ref[...]).astype(BF16)
    gate = jax.nn.sigmoid(_dot(hn, wg_ref[...]))
    o_ref[...] = h + _dot(p_ref[...].astype(BF16), wi_ref[...]) * gate


def _pe_update(h, g, p, w_gate, w_in):
    m, d = h.shape
    pd = p.shape[1]
    tm = _row_tile(m, 512)
    return pl.pallas_call(
        _pe_kernel,
        grid=(m // tm,),
        in_specs=[pl.BlockSpec((tm, d), lambda i: (i, 0)),
                  pl.BlockSpec((1, d), lambda i: (0, 0)),
                  pl.BlockSpec((tm, pd), lambda i: (i, 0)),
                  pl.BlockSpec((d, d), lambda i: (0, 0)),
                  pl.BlockSpec((pd, d), lambda i: (0, 0))],
        out_specs=pl.BlockSpec((tm, d), lambda i: (i, 0)),
        out_shape=jax.ShapeDtypeStruct((m, d), F32),
        compiler_params=_cparams("parallel"),
        name="pe_update",
    )(h, g.reshape(1, d), p, w_gate, w_in)


def _bias_kernel(tb_ref, qp_ref, kp_ref, o_ref, *, win):
    h = pl.program_id(0)
    kp = kp_ref[...]
    dist = qp_ref[...] - kp
    ok = (kp >= 0) & (dist >= 0)
    if win is not None:
        ok = ok & (dist < win)
    o_ref[0] = jnp.where(ok, _bias_from_dist(dist, tb_ref, h), NEG_INF)


def _bias_table(tb, n_heads, qpos, kpos, win=None):
    r, c = qpos.shape[0], kpos.shape[0]
    rb = _row_tile(r, 256)
    return pl.pallas_call(
        functools.partial(_bias_kernel, win=win),
        grid=(n_heads, r // rb),
        in_specs=[pl.BlockSpec(memory_space=pltpu.SMEM),
                  pl.BlockSpec((rb, 1), lambda h, i: (i, 0)),
                  pl.BlockSpec((1, c), lambda h, i: (0, 0))],
        out_specs=pl.BlockSpec((1, rb, c), lambda h, i: (h, i, 0)),
        out_shape=jax.ShapeDtypeStruct((n_heads, r, c), F32),
        compiler_params=_cparams("parallel", "parallel"),
        name="bias_table",
    )(tb, jnp.asarray(qpos, jnp.int32).reshape(r, 1), jnp.asarray(kpos, jnp.int32).reshape(1, c))


def _tile_bias(tb, n_kv, g, nd, win=None):
    t = _bias_table(tb, n_kv * g, np.arange(nd * TQ), np.arange(TK), win)
    t = t.reshape(n_kv, g, nd, TQ, TK).transpose(0, 2, 1, 3, 4)
    return jnp.concatenate([t, jnp.full((n_kv, 1, g, TQ, TK), NEG_INF, F32)], axis=1)


def _rank_nsel(score, sc_scr, nloop, kth, thresh):
    nb, n = score.shape
    sc_scr[...] = score
    out = []
    for c in range(n // LANES):
        sl = slice(c * LANES, (c + 1) * LANES)
        sc = score[:, sl]
        blk = lax.broadcasted_iota(jnp.int32, sc.shape, 0)

        def body(jg, cnt, sc=sc, blk=blk, sl=sl):
            grp = sc_scr[pl.ds(pl.multiple_of(jg * SUBLANES, SUBLANES), SUBLANES), sl]
            for r in range(SUBLANES):
                row = grp[r:r + 1, :]
                beats = (row > sc) | ((row == sc) & (jg * SUBLANES + r < blk))
                cnt = cnt + jnp.where(beats, 1.0, 0.0)
            return cnt

        cnt = lax.fori_loop(0, (nloop + SUBLANES - 1) // SUBLANES, body, jnp.zeros(sc.shape, F32))
        sel = (cnt < kth) & (sc > thresh)
        out.append(jnp.where(sel, 0.0, 1.0))
    return out


def _pos_term(pos_ref, w1_ref):
    ph, plo = _split2(pos_ref[0])
    wh, wl = _split2(w1_ref[0])
    return (_dot(ph, wh) + _dot(ph, wl) + _dot(plo, wh))[0:1]


def _cmp_mlp_kernel(x_ref, w1a_ref, w1b_ref, pos_ref, w1_ref, w2_ref, g_ref, o_ref, *, norm):
    nb, nc, _ = x_ref.shape
    x = x_ref[...].reshape(nb * nc, x_ref.shape[2])
    a = _dot(x, w1a_ref[0])
    b = _dot(x, w1b_ref[0])
    z = a + pltpu.roll(b, nb * nc - 1, 0) + _pos_term(pos_ref, w1_ref)
    y = _dot(jax.nn.gelu(z).astype(BF16), w2_ref[0])
    if norm:
        y = _rms(y, g_ref[...])
    o_ref[...] = y.reshape(nb, nc, HEAD_DIM)


def _cmp_mlp(x, w1, pos, w2, gain, kv, norm):
    s, nc, cw = x.shape
    nb = _row_tile(s, max(1, 1024 // nc))
    half = CMP_STRIDE * HEAD_DIM
    w1f = w1.reshape(2, CMP_LEN * HEAD_DIM, HEAD_DIM)
    w1a = w1f[:, :half].astype(BF16)
    w1b = w1f[:, half:].astype(BF16)
    posf = jnp.broadcast_to(pos.reshape(2, 1, CMP_LEN * HEAD_DIM), (2, SUBLANES, CMP_LEN * HEAD_DIM))
    return pl.pallas_call(
        functools.partial(_cmp_mlp_kernel, norm=norm),
        grid=(s // nb,),
        in_specs=[pl.BlockSpec((nb, nc, cw), lambda i: (i, 0, 0)),
                  pl.BlockSpec((1, half, HEAD_DIM), lambda i: (kv, 0, 0)),
                  pl.BlockSpec((1, half, HEAD_DIM), lambda i: (kv, 0, 0)),
                  pl.BlockSpec((1, SUBLANES, 2 * half), lambda i: (kv, 0, 0)),
                  pl.BlockSpec((1, 2 * half, HEAD_DIM), lambda i: (kv, 0, 0)),
                  pl.BlockSpec((1, HEAD_DIM, HEAD_DIM), lambda i: (kv, 0, 0)),
                  pl.BlockSpec((1, HEAD_DIM), lambda i: (0, 0))],
        out_specs=pl.BlockSpec((nb, nc, HEAD_DIM), lambda i: (i, 0, 0)),
        out_shape=jax.ShapeDtypeStruct((s, nc, HEAD_DIM), F32),
        compiler_params=_cparams("parallel"),
        name="cmp_mlp",
    )(x, w1a, w1b, posf, w1f, w2.astype(BF16), gain.reshape(1, HEAD_DIM))


def _stack_heads(q_ref, g):
    return jnp.concatenate([q_ref[:, gi * HEAD_DIM:(gi + 1) * HEAD_DIM] for gi in range(g)], axis=0)


def _unstack_heads(o, g, tq):
    return [o[gi * tq:(gi + 1) * tq] for gi in range(g)]


def _nsa_cmp_kernel(tb_ref, q_ref, kc_ref, vc_ref, c2s_ref, gate_ref, o_ref, nsel_ref, s_scr, sc_scr, *, n_kv, g):
    tq = q_ref.shape[0]
    ncp = kc_ref.shape[1]
    r = g * tq
    kvh = pl.program_id(0) % n_kv
    t0 = pl.program_id(1) * tq
    q = _stack_heads(q_ref, g)
    for cc in range(ncp // LANES):
        c0 = cc * LANES
        sl = slice(c0, c0 + LANES)
        min_kp = CMP_STRIDE * c0 + CMP_LEN - 1
        max_kp = CMP_STRIDE * (c0 + LANES - 1) + CMP_LEN - 1
        future = min_kp > t0 + (tq - 1)
        sat = t0 - max_kp >= SAT_DIST

        @pl.when(future)
        def _():
            s_scr[:, sl] = jnp.full((r, LANES), NEG_INF, F32)

        @pl.when(sat)
        def _():
            s = _dot_nt(q, kc_ref[0, sl, :]).reshape(g, tq, LANES)
            for gi in range(g):
                s_scr[gi * tq:(gi + 1) * tq, sl] = s[gi] + tb_ref[N_BUCKETS - 1, kvh * g + gi]

        @pl.when(jnp.logical_not(future | sat))
        def _():
            s = _dot_nt(q, kc_ref[0, sl, :]).reshape(g, tq, LANES)
            t = t0 + lax.broadcasted_iota(jnp.int32, (tq, LANES), 0)
            kp = CMP_STRIDE * (c0 + lax.broadcasted_iota(jnp.int32, (tq, LANES), 1)) + (CMP_LEN - 1)
            dist = t - kp
            for gi in range(g):
                b = _bias_from_dist(dist, tb_ref, kvh * g + gi)
                s_scr[gi * tq:(gi + 1) * tq, sl] = jnp.where(dist >= 0, s[gi] + b, NEG_INF)

    s = s_scr[...]
    m = jnp.max(s, axis=-1, keepdims=True)
    e = jnp.where(s > 0.5 * NEG_INF, jnp.exp(s - m), 0.0)
    p = e / jnp.maximum(jnp.sum(e, axis=-1, keepdims=True), 1e-30)
    o = _unstack_heads(_dot(p.astype(BF16), vc_ref[0]), g, tq)
    o_ref[...] = jnp.concatenate([o[gi] * gate_ref[:, gi:gi + 1] for gi in range(g)], axis=-1)

    psum = jnp.sum(p.reshape(g, tq, ncp), axis=0)
    ph, plo = _split2(psum)
    c2s = c2s_ref[...]
    imp_t = _dot_nt(c2s, ph) + _dot_nt(c2s, plo)
    blk = lax.broadcasted_iota(jnp.int32, (NBLK_PAD, tq), 0)
    t = t0 + lax.broadcasted_iota(jnp.int32, (NBLK_PAD, tq), 1)
    valid = SEL_BLOCK * blk <= t
    forced = (blk == t // SEL_BLOCK) | (blk == 0)
    score = jnp.where(forced, SEL_BIG, jnp.where(valid, imp_t, -SEL_BIG))
    nloop = jnp.minimum((t0 + tq - 1) // SEL_BLOCK + 1, NBLK_PAD)
    cols = _rank_nsel(score, sc_scr, nloop, N_SEL, -jnp.inf)
    for c, nsel_t in enumerate(cols):
        nsel_ref[0, c * LANES:(c + 1) * LANES, :] = nsel_t.T.astype(BF16)


def _nsa_cmp(tb, q, kc, vc, c2s_t, gates, b, t, n_kv, g):
    ncp = kc.shape[1]
    tq = min(TQ, t)
    nq = t // tq
    gw = g * HEAD_DIM
    row = lambda bg, i: ((bg // n_kv) * nq + i, bg % n_kv)
    return pl.pallas_call(
        functools.partial(_nsa_cmp_kernel, n_kv=n_kv, g=g),
        grid=(b * n_kv, nq),
        in_specs=[pl.BlockSpec(memory_space=pltpu.SMEM),
                  pl.BlockSpec((tq, gw), row),
                  pl.BlockSpec((1, ncp, HEAD_DIM), lambda bg, i: (bg, 0, 0)),
                  pl.BlockSpec((1, ncp, HEAD_DIM), lambda bg, i: (bg, 0, 0)),
                  pl.BlockSpec((NBLK_PAD, ncp), lambda bg, i: (0, 0)),
                  pl.BlockSpec((tq, LANES), row)],
        out_specs=[pl.BlockSpec((tq, gw), row),
                   pl.BlockSpec((1, tq, NBLK_PAD), lambda bg, i: (bg, i, 0))],
        out_shape=[jax.ShapeDtypeStruct((b * t, n_kv * gw), F32),
                   jax.ShapeDtypeStruct((b * n_kv, t, NBLK_PAD), BF16)],
        scratch_shapes=[pltpu.VMEM((g * tq, ncp), F32), pltpu.VMEM((NBLK_PAD, tq), F32)],
        compiler_params=_cparams("parallel", "parallel"),
        name="nsa_cmp",
    )(tb, q, kc, vc, c2s_t, gates)


def _flash_kernel(*refs, g, look, aug, shared_nsel, branch):
    it = iter(refs)
    q_ref = next(it)
    nsel_ref = next(it) if aug else None
    kt_ref, vt_ref = next(it), next(it)
    oh_ref = next(it) if aug else None
    b_ref = next(it)
    acc_ref, gate_ref = (next(it), next(it)) if branch is not None else (None, None)
    o_ref, k_scr, v_scr, q_scr, m_scr, acc_scr = it
    tq = q_ref.shape[0]
    nt = k_scr.shape[0]
    nd = b_ref.shape[1]
    r = g * tq
    i = pl.program_id(1)

    @pl.when(i == 0)
    def _():
        for j in range(nt):
            cols = slice(j * TK, (j + 1) * TK)
            if aug:
                k_scr[j, 0:NBLK_PAD, :] = oh_ref[:, cols]
                k_scr[j, NBLK_PAD:NBLK_PAD + HEAD_DIM, :] = kt_ref[0, :, cols].astype(BF16)
            else:
                k_scr[j] = kt_ref[0, :, cols].astype(BF16)
            v_scr[j, 0:HEAD_DIM, :] = vt_ref[0, :, cols].astype(BF16)
            v_scr[j, HEAD_DIM:LANES, :] = jnp.ones((LANES - HEAD_DIM, TK), BF16)

    for gi in range(g):
        rows = slice(gi * tq, (gi + 1) * tq)
        qg = q_ref[:, gi * HEAD_DIM:(gi + 1) * HEAD_DIM]
        if aug:
            q_scr[rows, 0:NBLK_PAD] = nsel_ref[0] if shared_nsel else nsel_ref[:, gi * NBLK_PAD:(gi + 1) * NBLK_PAD]
            q_scr[rows, NBLK_PAD:NBLK_PAD + HEAD_DIM] = qg
        else:
            q_scr[rows, :] = qg
    m_scr[...] = jnp.full((r, LANES), NEG_INF, F32)
    acc_scr[...] = jnp.zeros((r, LANES), F32)
    lo = 0 if look is None else jnp.maximum(i - look, 0)

    def body(jj, carry):
        ja = lo + 2 * jj
        jb = ja + 1
        jb_c = jnp.minimum(jb, nt - 1)
        da = jnp.minimum(i - ja, nd - 2)
        db = jnp.where(jb > i, nd - 1, jnp.minimum(i - jb, nd - 2))
        q = q_scr[...]
        sa = (_dot(q, k_scr[ja]).reshape(g, tq, TK) + b_ref[0, da]).reshape(r, TK)
        sb = (_dot(q, k_scr[jb_c]).reshape(g, tq, TK) + b_ref[0, db]).reshape(r, TK)
        m_prev = m_scr[...]
        m_cur = jnp.maximum(jnp.max(sa, axis=-1, keepdims=True), jnp.max(sb, axis=-1, keepdims=True))
        m_new = jnp.maximum(m_prev, m_cur)
        alpha = jnp.exp(m_prev - m_new)
        m_rep = jnp.concatenate([m_new] * (TK // LANES), axis=-1)
        pa = jnp.exp(sa - m_rep).astype(BF16)
        pb = jnp.exp(sb - m_rep).astype(BF16)
        acc_scr[...] = alpha * acc_scr[...] + _dot_nt(pa, v_scr[ja]) + _dot_nt(pb, v_scr[jb_c])
        m_scr[...] = m_new
        return carry

    lax.fori_loop(0, (i + 2 - lo) // 2, body, 0)
    acc = acc_scr[...]
    o = _unstack_heads(acc[:, :HEAD_DIM] / jnp.maximum(acc[:, HEAD_DIM:], 1e-30), g, tq)
    if branch is not None:
        o = [acc_ref[:, gi * HEAD_DIM:(gi + 1) * HEAD_DIM] + gate_ref[:, branch * g + gi:branch * g + gi + 1] * o[gi]
             for gi in range(g)]
    o_ref[...] = jnp.concatenate(o, axis=-1)


def _flash_attn(q, kv_t, k_blk, v_blk, bias, b, t, n_kv, g, nsel=None, onehot_t=None, look=None,
                acc=None, gates=None, branch=None):
    aug = nsel is not None
    shared = aug and nsel.ndim == 3
    nq = t // TQ
    gw = g * HEAD_DIM
    dk = HEAD_DIM + (NBLK_PAD if aug else 0)
    row = lambda bg, i: ((bg // n_kv) * nq + i, bg % n_kv)
    in_specs = [pl.BlockSpec((TQ, gw), row)]
    args = [q]
    if aug:
        in_specs.append(pl.BlockSpec((1, TQ, NBLK_PAD), lambda bg, i: (bg, i, 0)) if shared
                        else pl.BlockSpec((TQ, g * NBLK_PAD), row))
        args.append(nsel)
    in_specs += [pl.BlockSpec((1, HEAD_DIM, t), lambda bg, i: (bg // n_kv, k_blk + bg % n_kv, 0)),
                 pl.BlockSpec((1, HEAD_DIM, t), lambda bg, i: (bg // n_kv, v_blk + bg % n_kv, 0))]
    args += [kv_t, kv_t]
    if aug:
        in_specs.append(pl.BlockSpec((NBLK_PAD, t), lambda bg, i: (0, 0)))
        args.append(onehot_t)
    nd = bias.shape[1]
    in_specs.append(pl.BlockSpec((1, nd, g, TQ, TK), lambda bg, i: (bg % n_kv, 0, 0, 0, 0)))
    args.append(bias)
    if branch is not None:
        in_specs += [pl.BlockSpec((TQ, gw), row), pl.BlockSpec((TQ, LANES), row)]
        args += [acc, gates]
    return pl.pallas_call(
        functools.partial(_flash_kernel, g=g, look=look, aug=aug, shared_nsel=shared, branch=branch),
        grid=(b * n_kv, nq),
        in_specs=in_specs,
        out_specs=pl.BlockSpec((TQ, gw), row),
        out_shape=jax.ShapeDtypeStruct((b * t, n_kv * gw), F32),
        scratch_shapes=[pltpu.VMEM((t // TK, dk, TK), BF16), pltpu.VMEM((t // TK, LANES, TK), BF16),
                        pltpu.VMEM((g * TQ, dk), BF16), pltpu.VMEM((g * TQ, LANES), F32),
                        pltpu.VMEM((g * TQ, LANES), F32)],
        compiler_params=_cparams("parallel", "arbitrary"),
        name="flash_attn",
    )(*args)


def _sb_kernel(q_ref, kt_ref, vt_ref, u_ref, o_ref, k_scr, v_scr, carry_scr, acc_scr, *, g):
    tq = q_ref.shape[0]
    nt = k_scr.shape[0]
    r = g * tq
    i = pl.program_id(1)

    @pl.when(i == 0)
    def _():
        for j in range(nt):
            cols = slice(j * TK, (j + 1) * TK)
            k_scr[j] = kt_ref[0, :, cols].astype(BF16)
            v_scr[j] = vt_ref[0, :, cols].astype(BF16)

    q = _stack_heads(q_ref, g)
    carry_scr[...] = jnp.zeros((r, LANES), F32)
    acc_scr[...] = jnp.zeros((r, HEAD_DIM), F32)
    t = i * tq + lax.broadcasted_iota(jnp.int32, (g, tq, TK), 1).reshape(r, TK)
    lane = lax.broadcasted_iota(jnp.int32, (r, TK), 1)
    u = u_ref[...]

    def cond(st):
        return (st[0] >= 0) & (st[1] > 0)

    def body(st):
        j = st[0]
        z = _dot(q, k_scr[j])
        before = (j * TK + lane) < t
        sp = jnp.maximum(z, 0.0) + jnp.log1p(jnp.exp(-jnp.abs(z)))
        lk = jnp.where(before, -sp, 0.0)
        hi, lo = _split2(lk)
        carry = carry_scr[...]
        between = _dot(hi, u) + _dot(lo, u) + jnp.concatenate([carry] * (TK // LANES), axis=-1)
        a = jnp.where(before, jnp.exp(z - sp + between), 0.0)
        acc_scr[...] += _dot_nt(a.astype(BF16), v_scr[j])
        carry = carry + jnp.sum(lk, axis=-1, keepdims=True)
        carry_scr[...] = carry
        return j - 1, (jnp.max(carry) > -SB_CUT).astype(jnp.int32)

    lax.while_loop(cond, body, (((i + 1) * tq - 1) // TK, jnp.int32(1)))
    o_ref[...] = jnp.concatenate(_unstack_heads(acc_scr[...], g, tq), axis=-1)


def _sb_attn(q, kv_t, b, t, n_kv, g):
    tq = min(SB_TQ, t)
    nq = t // tq
    gw = g * HEAD_DIM
    idx = np.arange(TK)
    u = jnp.asarray(idx[:, None] > idx[None, :], BF16)
    row = lambda bg, i: ((bg // n_kv) * nq + i, bg % n_kv)
    return pl.pallas_call(
        functools.partial(_sb_kernel, g=g),
        grid=(b * n_kv, nq),
        in_specs=[pl.BlockSpec((tq, gw), row),
                  pl.BlockSpec((1, HEAD_DIM, t), lambda bg, i: (bg // n_kv, bg % n_kv, 0)),
                  pl.BlockSpec((1, HEAD_DIM, t), lambda bg, i: (bg // n_kv, n_kv + bg % n_kv, 0)),
                  pl.BlockSpec((TK, TK), lambda bg, i: (0, 0))],
        out_specs=pl.BlockSpec((tq, gw), row),
        out_shape=jax.ShapeDtypeStruct((b * t, n_kv * gw), F32),
        scratch_shapes=[pltpu.VMEM((t // TK, HEAD_DIM, TK), BF16), pltpu.VMEM((t // TK, HEAD_DIM, TK), BF16),
                        pltpu.VMEM((g * tq, LANES), F32), pltpu.VMEM((g * tq, HEAD_DIM), F32)],
        compiler_params=_cparams("parallel", "arbitrary"),
        name="sb_attn",
    )(q, kv_t, kv_t, u)


def _moba_gate_kernel(q_ref, kt_ref, a_ref, nsel_ref, km_scr, sc_scr, *, g):
    tq = q_ref.shape[0]
    i = pl.program_id(1)

    @pl.when(i == 0)
    def _():
        a = a_ref[...]
        h1, h2, h3 = _split3(kt_ref[0])
        km_t = _dot(h1, a) + _dot(h2, a) + _dot(h3, a)
        km_sq = jnp.concatenate([km_t, jnp.zeros_like(km_t)], axis=0).T
        km_scr[...] = km_sq[:, :HEAD_DIM]

    kh, kl = _split2(km_scr[...])
    blk = lax.broadcasted_iota(jnp.int32, (NBLK_PAD, tq), 0)
    own = (i * tq + lax.broadcasted_iota(jnp.int32, (NBLK_PAD, tq), 1)) // MOBA_BLOCK
    nloop = jnp.minimum((i * tq + tq - 1) // MOBA_BLOCK + 1, NBLK_PAD)
    for gi in range(g):
        qg = q_ref[:, gi * HEAD_DIM:(gi + 1) * HEAD_DIM]
        gate_t = (_dot_nt(kh, qg) + _dot_nt(kl, qg)) * (1.0 / SCALE)
        score = jnp.where(blk == own, SEL_BIG, jnp.where(blk < own, gate_t, -SEL_BIG))
        cols = _rank_nsel(score, sc_scr, nloop, MOBA_TOPK + 1, -SEL_BIG / 2)
        for c, nsel_t in enumerate(cols):
            nsel_ref[c * LANES:(c + 1) * LANES, gi * NBLK_PAD:(gi + 1) * NBLK_PAD] = nsel_t.T.astype(BF16)


def _moba_gate(q, kv_t, b, t, n_kv, g):
    tq = min(TQ, t)
    nq = t // tq
    avg = (np.arange(t)[:, None] // MOBA_BLOCK) == np.arange(NBLK_PAD)[None, :]
    avg = jnp.asarray(np.where(avg, 1.0 / MOBA_BLOCK, 0.0), BF16)
    row = lambda bg, i: ((bg // n_kv) * nq + i, bg % n_kv)
    return pl.pallas_call(
        functools.partial(_moba_gate_kernel, g=g),
        grid=(b * n_kv, nq),
        in_specs=[pl.BlockSpec((tq, g * HEAD_DIM), row),
                  pl.BlockSpec((1, HEAD_DIM, t), lambda bg, i: (bg // n_kv, bg % n_kv, 0)),
                  pl.BlockSpec((t, NBLK_PAD), lambda bg, i: (0, 0))],
        out_specs=pl.BlockSpec((tq, g * NBLK_PAD), row),
        out_shape=jax.ShapeDtypeStruct((b * t, n_kv * g * NBLK_PAD), BF16),
        scratch_shapes=[pltpu.VMEM((NBLK_PAD, HEAD_DIM), F32), pltpu.VMEM((NBLK_PAD, tq), F32)],
        compiler_params=_cparams("parallel", "arbitrary"),
        name="moba_gate",
    )(q, kv_t, avg)


def _block_onehot_neg_t(t, block):
    oh = np.arange(NBLK_PAD)[:, None] == (np.arange(t)[None, :] // block)
    return jnp.asarray(np.where(oh, NEG_INF, 0.0), BF16)


def _cmp_to_sel_t(ncp, nc, nsb):
    cs = CMP_STRIDE * np.arange(ncp)[None, :]
    ss = SEL_BLOCK * np.arange(NBLK_PAD)[:, None]
    ov = np.clip(np.minimum(cs + CMP_LEN, ss + SEL_BLOCK) - np.maximum(cs, ss), 0, None) / CMP_LEN
    ov = np.where((np.arange(ncp)[None, :] < nc) & (np.arange(NBLK_PAD)[:, None] < nsb), ov, 0.0)
    return jnp.asarray(ov, BF16)


def _rows_from_t(x_t, b, t, n_heads):
    return x_t.reshape(b, 2, n_heads, HEAD_DIM, t).transpose(0, 4, 1, 2, 3)


def _gate_cols(w_g):
    d = w_g.shape[0]
    w = w_g.reshape(d, NSA_KV_HEADS, G_NSA, 3).transpose(0, 1, 3, 2).reshape(d, NSA_KV_HEADS, 3 * G_NSA)
    return jnp.pad(w, ((0, 0), (0, 0), (0, LANES - 3 * G_NSA))).reshape(d, NSA_KV_HEADS * LANES)


_E_COLS = dict(q=(0, 512), kv=(512, 1280), g=(1280, 1304), qs=(1304, 1816), sb=(1816, 2328))
_E_T_MODES = ('raw',) * 4 + ('norm',) * 2 + ('raw',) * 2 + ('norm',) * 2 + ('raw',) * 2 + ('raw',) * 8
_E_T_OUTS = ((0, 4), (4, 8), (8, 12), (12, 20))


def _even_t_weights(w_in, k_gain):
    cut = lambda k: w_in[:, _E_COLS[k][0]:_E_COLS[k][1]]
    w_t = jnp.concatenate([cut('kv'), cut('sb')], axis=1).T.astype(BF16)
    one = jnp.ones((HEAD_DIM,), F32)
    gt = jnp.concatenate([one] * 4 + [k_gain[1]] * 2 + [one] * 2 + [k_gain[2]] * 2 + [one] * 10)
    return w_t, gt


def _pair_cols(w_k, w_v, n_heads):
    d = w_k.shape[0]
    return jnp.concatenate([w_k.reshape(d, n_heads, HEAD_DIM), w_v.reshape(d, n_heads, HEAD_DIM)], axis=2).reshape(d, n_heads * LANES)


def _even_prompt(h, b, t, P, j, li):
    w_in = P['w_in_even'][j]
    q_gain, k_gain = P['nsa_q_norm'][j], P['nsa_k_norm'][j]
    cut = lambda k: w_in[:, _E_COLS[k][0]:_E_COLS[k][1]]
    w_row = jnp.concatenate([cut('q'), cut('qs'), _gate_cols(cut('g')), cut('kv')[:, :256]], axis=1).astype(BF16)
    cs = jnp.concatenate([jnp.tile(q_gain, NSA_HEADS) * SCALE, jnp.full((512,), SCALE, F32), jnp.ones((512,), F32)])
    row_modes = ('norm',) * 4 + ('scale',) * 4 + ('sigmoid',) * 2 + ('scale',) * 2
    w_t, gt = _even_t_weights(w_in, k_gain)
    q, qs, gates, kcvc, cmp_t, slc_t, win_t, sb_t = _proj(
        h, b, t, P['norm_mix'][li], w_row, cs, row_modes, ((0, 4), (4, 8), (8, 10), (10, 12)),
        (BF16, BF16, F32, BF16), w_t, gt, _E_T_MODES, _E_T_OUTS)
    tb = P['rel_bias_table']
    hkv, g = NSA_KV_HEADS, G_NSA

    ncp = t // CMP_STRIDE
    x4 = kcvc.reshape(b, ncp, CMP_STRIDE, 2 * hkv, HEAD_DIM).transpose(0, 3, 1, 2, 4).reshape(b, 2, hkv, ncp, CMP_STRIDE * HEAD_DIM)
    w1, pos, w2 = P['nsa_cmp_w1'][j], P['nsa_cmp_pos'][j], P['nsa_cmp_w2'][j]
    kc = _cmp_mlp(x4[:, 0].reshape(b * hkv, ncp, -1), w1, pos, w2, k_gain[0], 0, True).astype(BF16)
    vc = _cmp_mlp(x4[:, 1].reshape(b * hkv, ncp, -1), w1, pos, w2, k_gain[0], 1, False).astype(BF16)
    o, nsel = _nsa_cmp(tb, q, kc, vc, _cmp_to_sel_t(ncp, ncp - 1, t // SEL_BLOCK), gates, b, t, hkv, g)

    nd_full = -(-(SAT_DIST + TK - 1) // TQ) + 1
    o = _flash_attn(q, slc_t, 0, hkv, _tile_bias(tb, hkv, g, nd_full), b, t, hkv, g, nsel=nsel,
                    onehot_t=_block_onehot_neg_t(t, SEL_BLOCK), acc=o, gates=gates, branch=1)
    look = (WINDOW - 1 + TK - 1) // TK
    o = _flash_attn(q, win_t, 0, hkv, _tile_bias(tb, hkv, g, look + 1, win=WINDOW), b, t, hkv, g, look=look,
                    acc=o, gates=gates, branch=2)

    o_b = _sb_attn(qs, sb_t, b, t, SB_KV_HEADS, G_SB)
    h = _out_proj([o, o_b], P['w_out_even'][j].astype(BF16), h)
    wb = min(WINDOW, t)
    states = (_rows_from_t(cmp_t, b, t, hkv), _rows_from_t(slc_t, b, t, hkv),
              _rows_from_t(win_t[:, :, t - wb:], b, wb, hkv), _rows_from_t(sb_t, b, t, SB_KV_HEADS))
    return h, states


def _odd_prompt(h, b, t, P, j, li):
    hkv, g = MOBA_KV_HEADS, G_MOBA
    w_in = P['w_in_odd'][j]
    w_row = w_in[:, :1024].astype(BF16)
    cs = jnp.tile(P['moba_q_norm'][j], MOBA_HEADS) * SCALE
    w_t = w_in[:, 1024:].T.astype(BF16)
    gt = jnp.concatenate([jnp.tile(P['moba_k_norm'][j], hkv), jnp.ones((hkv * HEAD_DIM,), F32)])
    q, kv_t = _proj(h, b, t, P['norm_mix'][li], w_row, cs, ('norm',) * 8, ((0, 8),), (BF16,),
                    w_t, gt, ('norm',) * 4 + ('raw',) * 4, ((0, 8),))
    tb = P['rel_bias_table']
    nsel = _moba_gate(q, kv_t, b, t, hkv, g)
    nd_full = -(-(SAT_DIST + TK - 1) // TQ) + 1
    o = _flash_attn(q, kv_t, 0, hkv, _tile_bias(tb, hkv, g, nd_full), b, t, hkv, g, nsel=nsel,
                    onehot_t=_block_onehot_neg_t(t, MOBA_BLOCK))
    h = _out_proj([o], P['w_out_odd'][j].astype(BF16), h)
    return h, (_rows_from_t(kv_t, b, t, hkv),)


def _ffn_and_pe(h, pemb, P, li):
    j = li // 2
    if li % 2 == 0:
        h = _ffn(h, P['norm_ffn'][li], P['w_ffn_gu'][j].astype(BF16), P['w_ffn_down'][j].astype(BF16), 256)
    else:
        gates = _router(h, P['norm_ffn'][li], P['w_router'][j], P['b_router'][j])
        gates_t = gates[:, :N_EXPERTS].T[:, :, None]
        h = _moe(h, P['norm_ffn'][li], gates_t, P['w_moe_gu'][j].astype(BF16), P['w_moe_down'][j].astype(BF16), 512)
    return _pe_update(h, P['norm_pe'][li], pemb, P['w_pe_gate'][li].astype(BF16), P['w_pe_in'][li].astype(BF16))


def _trunk_prompt(x, p, P):
    b, t, d = x.shape
    h = x.reshape(b * t, d)
    even_states, odd_states = [], []
    for li in range(p.shape[0]):
        j = li // 2
        if li % 2 == 0:
            h, st = _even_prompt(h, b, t, P, j, li)
            even_states.append(st)
        else:
            h, st = _odd_prompt(h, b, t, P, j, li)
            odd_states.append(st)
        h = _ffn_and_pe(h, p[li].reshape(b * t, -1), P, li)
    return h.reshape(b, t, d), even_states, odd_states


def _kv_rows(x_ref, h):
    kv_t = jnp.concatenate([x_ref[0, h], x_ref[1, h]], axis=0)
    n = kv_t.shape[1]
    return jnp.concatenate([kv_t[:, c * LANES:(c + 1) * LANES].T for c in range(n // LANES)], axis=0)


def _gather_kernel(pt_ref, *refs):
    o_ref = refs[-1]
    n_heads = o_ref.shape[1]
    for p, x_ref in enumerate(refs[:-1]):
        for h in range(n_heads):
            o_ref[0, h, p * PAGE_SIZE:(p + 1) * PAGE_SIZE, :] = _kv_rows(x_ref, h)


def _gather_kv(pool_t, layer, page_table):
    db, npg = page_table.shape
    _, _, two, hh, dd, ps = pool_t.shape

    def in_map(p):
        return lambda b, pt: (layer, pt[b * npg + p], 0, 0, 0, 0)

    grid_spec = pltpu.PrefetchScalarGridSpec(
        num_scalar_prefetch=1,
        grid=(db,),
        in_specs=[pl.BlockSpec((None, None, two, hh, dd, ps), in_map(p)) for p in range(npg)],
        out_specs=pl.BlockSpec((1, hh, npg * ps, LANES), lambda b, pt: (b, 0, 0, 0)),
    )
    return pl.pallas_call(
        _gather_kernel,
        grid_spec=grid_spec,
        out_shape=jax.ShapeDtypeStruct((db, hh, npg * ps, LANES), F32),
        compiler_params=_cparams("parallel"),
        name="gather_kv",
    )(page_table.reshape(-1), *([pool_t] * npg))


def _win_rows_kernel(x_ref, o_ref):
    for h in range(o_ref.shape[1]):
        o_ref[0, h] = _kv_rows(x_ref, h)


def _win_rows(win_t, layer):
    _, db, two, hh, dd, w = win_t.shape
    return pl.pallas_call(
        _win_rows_kernel,
        grid=(db,),
        in_specs=[pl.BlockSpec((None, None, two, hh, dd, w), lambda b: (layer, b, 0, 0, 0, 0))],
        out_specs=pl.BlockSpec((1, hh, w, LANES), lambda b: (b, 0, 0, 0)),
        out_shape=jax.ShapeDtypeStruct((db, hh, w, LANES), F32),
        compiler_params=_cparams("parallel"),
        name="win_rows",
    )(win_t)


def _win_shift_kernel(x_ref, new_ref, o_ref):
    x = x_ref[0]
    lane = lax.broadcasted_iota(jnp.int32, x.shape, 1)
    o_ref[0] = jnp.where(lane == x.shape[1] - 1, new_ref[0], pltpu.roll(x, x.shape[1] - 1, 1))


def _win_shift(win_t, layer, new_col):
    _, db, two, hh, dd, w = win_t.shape
    rows = two * hh * dd
    x = win_t.reshape(win_t.shape[0], db, rows, w)
    out = pl.pallas_call(
        _win_shift_kernel,
        grid=(db,),
        in_specs=[pl.BlockSpec((None, 1, rows, w), lambda b: (layer, b, 0, 0)),
                  pl.BlockSpec((1, rows, 1), lambda b: (b, 0, 0))],
        out_specs=pl.BlockSpec((1, rows, w), lambda b: (b, 0, 0)),
        out_shape=jax.ShapeDtypeStruct((db, rows, w), F32),
        compiler_params=_cparams("parallel"),
        name="win_shift",
    )(x, new_col)
    return out.reshape(db, two, hh, dd, w)


def _dec_cmp_mlp_kernel(kv_ref, w1_ref, pos_ref, w1f_ref, w2_ref, g_ref, o_ref):
    n_heads, rows = kv_ref.shape[1], kv_ref.shape[2]
    nc = rows // CMP_STRIDE
    posb = jnp.concatenate([_pos_term(pos_ref.at[0:1], w1f_ref.at[0:1]), _pos_term(pos_ref.at[1:2], w1f_ref.at[1:2])], axis=-1)
    left = _left_half()
    for h in range(n_heads):
        acc = jnp.zeros((nc, 2 * LANES), F32)
        for l in range(CMP_STRIDE):
            xl = kv_ref[0, h, pl.ds(l, nc, stride=CMP_STRIDE), :].astype(BF16)
            acc = acc + _dot(xl, w1_ref[l])
        z = acc[:, :LANES] + pltpu.roll(acc[:, LANES:], nc - 1, 0) + posb
        y = _dot(jax.nn.gelu(z).astype(BF16), w2_ref[...])
        y = y * jnp.where(left, _half_rms_inv(y, False) * g_ref[...], 1.0)
        o_ref[0, h] = y


def _dec_cmp_mlp(kv, w1, pos, w2, gain):
    db, hh, rows, _ = kv.shape
    nc = rows // CMP_STRIDE
    z64 = jnp.zeros((CMP_LEN, HEAD_DIM, HEAD_DIM), F32)
    wd = jnp.concatenate([jnp.concatenate([w1[0], z64], axis=2), jnp.concatenate([z64, w1[1]], axis=2)], axis=1)
    w1d = jnp.concatenate([wd[:CMP_STRIDE], wd[CMP_STRIDE:]], axis=2).astype(BF16)
    z2 = jnp.zeros((HEAD_DIM, HEAD_DIM), F32)
    w2d = jnp.concatenate([jnp.concatenate([w2[0], z2], axis=1), jnp.concatenate([z2, w2[1]], axis=1)], axis=0).astype(BF16)
    w1f = w1.reshape(2, CMP_LEN * HEAD_DIM, HEAD_DIM)
    posf = jnp.broadcast_to(pos.reshape(2, 1, CMP_LEN * HEAD_DIM), (2, SUBLANES, CMP_LEN * HEAD_DIM))
    gpad = jnp.concatenate([gain, jnp.ones((HEAD_DIM,), F32)]).reshape(1, LANES)
    return pl.pallas_call(
        _dec_cmp_mlp_kernel,
        grid=(db,),
        in_specs=[pl.BlockSpec((1, hh, rows, LANES), lambda b: (b, 0, 0, 0)),
                  pl.BlockSpec(w1d.shape, lambda b: (0, 0, 0)),
                  pl.BlockSpec(posf.shape, lambda b: (0, 0, 0)),
                  pl.BlockSpec(w1f.shape, lambda b: (0, 0, 0)),
                  pl.BlockSpec(w2d.shape, lambda b: (0, 0)),
                  pl.BlockSpec((1, LANES), lambda b: (0, 0))],
        out_specs=pl.BlockSpec((1, hh, nc, LANES), lambda b: (b, 0, 0, 0)),
        out_shape=jax.ShapeDtypeStruct((db, hh, nc, LANES), F32),
        compiler_params=_cparams("parallel"),
        name="dec_cmp_mlp",
    )(kv, w1d, posf, w1f, w2d, gpad)


def _dec_scores(q, kv_ref, new_ref, h):
    kvb = kv_ref[0, h].astype(BF16)
    s = _dot_nt(q, kvb)
    if new_ref is None:
        return s, kvb, None
    nb = new_ref[0, h].astype(BF16)
    return jnp.concatenate([s, _dot_nt(q, nb)], axis=-1), kvb, nb


def _dec_softmax_out(s, kvb, nb):
    m = jnp.max(s, axis=-1, keepdims=True)
    e = jnp.where(s > 0.5 * NEG_INF, jnp.exp(s - m), 0.0)
    p = e / jnp.maximum(jnp.sum(e, axis=-1, keepdims=True), 1e-30)
    pb = p.astype(BF16)
    nm = kvb.shape[0]
    o = _dot(pb[:, :nm], kvb)
    if nb is not None:
        o = o + _dot(pb[:, nm:], nb)
    return p, o


def _dec_attn_kernel(*refs, g, has_new, aug, gated, imp):
    it = iter(refs)
    q_ref, kv_ref = next(it), next(it)
    new_ref = next(it) if has_new else None
    b_ref = next(it)
    qa_ref, oh_ref = (next(it), next(it)) if aug else (None, None)
    acc_ref, gate_ref = (next(it), next(it)) if gated else (None, None)
    c2s_ref = next(it) if imp else None
    o_ref = next(it)
    imp_ref = next(it) if imp else None
    for h in range(q_ref.shape[1]):
        s, kvb, nb = _dec_scores(q_ref[0, h], kv_ref, new_ref, h)
        s = s + b_ref[h]
        if aug:
            s = s + _dot(qa_ref[0, h], oh_ref[...])
        p, o = _dec_softmax_out(s, kvb, nb)
        if gated:
            o = acc_ref[0, h] + gate_ref[0, h] * o
        o_ref[0, h] = o
        if imp:
            row = lax.broadcasted_iota(jnp.int32, p.shape, 0)
            psum = jnp.sum(jnp.where(row < g, p, 0.0), axis=0, keepdims=True)
            ph, plo = _split2(jnp.broadcast_to(psum, p.shape))
            imp_ref[0, h] = _dot_nt(ph, c2s_ref[...]) + _dot_nt(plo, c2s_ref[...])


def _dec_attn(q, kv, new, bias, g, qa=None, onehot_t=None, acc=None, gate=None, c2s_t=None):
    db, n_kv, _, _ = q.shape
    tm = kv.shape[2]
    tp = bias.shape[2]
    has_new, aug, gated, imp = new is not None, qa is not None, acc is not None, c2s_t is not None
    row4 = lambda w: pl.BlockSpec((1, n_kv, DEC_ROWS, w), lambda b: (b, 0, 0, 0))
    in_specs = [row4(LANES), pl.BlockSpec((1, n_kv, tm, LANES), lambda b: (b, 0, 0, 0))]
    args = [q, kv]
    if has_new:
        in_specs.append(pl.BlockSpec((1, n_kv, LANES, LANES), lambda b: (b, 0, 0, 0)))
        args.append(new)
    in_specs.append(pl.BlockSpec((n_kv, DEC_ROWS, tp), lambda b: (0, 0, 0)))
    args.append(bias)
    if aug:
        in_specs += [row4(NBLK_PAD), pl.BlockSpec((NBLK_PAD, tp), lambda b: (0, 0))]
        args += [qa, onehot_t]
    if gated:
        in_specs += [row4(LANES), row4(1)]
        args += [acc, gate]
    if imp:
        in_specs += [pl.BlockSpec((NBLK_PAD, tp), lambda b: (0, 0))]
        args += [c2s_t]
    out_specs = [row4(LANES)]
    out_shape = [jax.ShapeDtypeStruct((db, n_kv, DEC_ROWS, LANES), F32)]
    if imp:
        out_specs.append(row4(NBLK_PAD))
        out_shape.append(jax.ShapeDtypeStruct((db, n_kv, DEC_ROWS, NBLK_PAD), F32))
    res = pl.pallas_call(
        functools.partial(_dec_attn_kernel, g=g, has_new=has_new, aug=aug, gated=gated, imp=imp),
        grid=(db,),
        in_specs=in_specs,
        out_specs=out_specs,
        out_shape=out_shape,
        compiler_params=_cparams("parallel"),
        name="dec_attn",
    )(*args)
    return res if imp else res[0]


def _dec_moba_kernel(q_ref, kv_ref, new_ref, b_ref, oh_ref, o_ref, *, own):
    lane = lax.broadcasted_iota(jnp.int32, (DEC_ROWS, NBLK_PAD), 1)
    for h in range(q_ref.shape[1]):
        s, kvb, nb = _dec_scores(q_ref[0, h], kv_ref, new_ref, h)
        gate = jnp.full((DEC_ROWS, NBLK_PAD), -SEL_BIG, F32)
        for n in range(own):
            gn = jnp.sum(s[:, n * MOBA_BLOCK:(n + 1) * MOBA_BLOCK], axis=-1, keepdims=True) * (1.0 / (SCALE * MOBA_BLOCK))
            gate = jnp.where(lane == n, gn, gate)
        score = jnp.where(lane == own, SEL_BIG, gate)
        cnt = jnp.zeros(score.shape, F32)
        for n in range(own + 1):
            col = score[:, n:n + 1]
            cnt = cnt + jnp.where((col > score) | ((col == score) & (n < lane)), 1.0, 0.0)
        sel = (cnt < MOBA_TOPK + 1) & (score > -SEL_BIG / 2)
        nsel = jnp.where(sel, 0.0, 1.0).astype(BF16)
        s = s + b_ref[h] + _dot(nsel, oh_ref[...])
        _, o = _dec_softmax_out(s, kvb, nb)
        o_ref[0, h] = o


def _dec_moba(q, kv, new, bias, onehot_t, past):
    db, n_kv, _, _ = q.shape
    tm, tp = kv.shape[2], bias.shape[2]
    row4 = pl.BlockSpec((1, n_kv, DEC_ROWS, LANES), lambda b: (b, 0, 0, 0))
    return pl.pallas_call(
        functools.partial(_dec_moba_kernel, own=past // MOBA_BLOCK),
        grid=(db,),
        in_specs=[row4,
                  pl.BlockSpec((1, n_kv, tm, LANES), lambda b: (b, 0, 0, 0)),
                  pl.BlockSpec((1, n_kv, LANES, LANES), lambda b: (b, 0, 0, 0)),
                  pl.BlockSpec((n_kv, DEC_ROWS, tp), lambda b: (0, 0, 0)),
                  pl.BlockSpec((NBLK_PAD, tp), lambda b: (0, 0))],
        out_specs=row4,
        out_shape=jax.ShapeDtypeStruct((db, n_kv, DEC_ROWS, LANES), F32),
        compiler_params=_cparams("parallel"),
        name="dec_moba",
    )(q, kv, new, bias, onehot_t)


def _dec_sb_kernel(q_ref, kv_ref, u_ref, o_ref):
    u = u_ref[...]
    for h in range(q_ref.shape[1]):
        q = q_ref[0, h]
        carry = jnp.zeros((DEC_ROWS, 1), F32)
        acc = jnp.zeros((DEC_ROWS, LANES), F32)
        for c in reversed(range(kv_ref.shape[2] // TK)):
            kvb = kv_ref[0, h, c * TK:(c + 1) * TK, :].astype(BF16)
            z = _dot_nt(q, kvb)
            sp = jnp.maximum(z, 0.0) + jnp.log1p(jnp.exp(-jnp.abs(z)))
            lk = -sp
            hi, lo = _split2(lk)
            between = _dot(hi, u) + _dot(lo, u) + carry
            a = jnp.exp(z - sp + between)
            acc = acc + _dot(a.astype(BF16), kvb)
            carry = carry + jnp.sum(lk, axis=-1, keepdims=True)
        o_ref[0, h] = acc


def _dec_sb(q, kv):
    db, n_kv, _, _ = q.shape
    tm = kv.shape[2]
    idx = np.arange(TK)
    u = jnp.asarray(idx[:, None] > idx[None, :], BF16)
    row4 = pl.BlockSpec((1, n_kv, DEC_ROWS, LANES), lambda b: (b, 0, 0, 0))
    return pl.pallas_call(
        _dec_sb_kernel,
        grid=(db,),
        in_specs=[row4, pl.BlockSpec((1, n_kv, tm, LANES), lambda b: (b, 0, 0, 0)), pl.BlockSpec((TK, TK), lambda b: (0, 0))],
        out_specs=row4,
        out_shape=jax.ShapeDtypeStruct((db, n_kv, DEC_ROWS, LANES), F32),
        compiler_params=_cparams("parallel"),
        name="dec_sb",
    )(q, kv, u)


def _rank_kernel(s_ref, o_ref, sc_scr, *, qpos, block, kth):
    n = s_ref.shape[2]
    blk = lax.broadcasted_iota(jnp.int32, (NBLK_PAD, n), 0)
    own = qpos // block
    past_score = jnp.where(blk * block <= qpos, s_ref[0], -SEL_BIG)
    score = jnp.where((blk == own) | (blk == 0), SEL_BIG, past_score)
    cols = _rank_nsel(score, sc_scr, min(own + 1, NBLK_PAD), kth, -jnp.inf)
    for c, nsel_t in enumerate(cols):
        o_ref[0, c * LANES:(c + 1) * LANES, :] = nsel_t.T.astype(BF16)


def _rank_select(score_t, qpos, block, kth):
    x, _, n = score_t.shape
    return pl.pallas_call(
        functools.partial(_rank_kernel, qpos=qpos, block=block, kth=kth),
        grid=(x,),
        in_specs=[pl.BlockSpec((1, NBLK_PAD, n), lambda i: (i, 0, 0))],
        out_specs=pl.BlockSpec((1, n, NBLK_PAD), lambda i: (i, 0, 0)),
        out_shape=jax.ShapeDtypeStruct((x, n, NBLK_PAD), BF16),
        scratch_shapes=[pltpu.VMEM((NBLK_PAD, n), F32)],
        compiler_params=_cparams("parallel"),
        name="rank_select",
    )(score_t)


def _dec_q(x, db, n_kv, g):
    x = x.reshape(db, n_kv, g, HEAD_DIM)
    return jnp.pad(x, ((0, 0), (0, 0), (0, DEC_ROWS - g), (0, LANES - HEAD_DIM)))


def _dec_new(x, db, n_heads):
    return jnp.pad(x.reshape(db, n_heads, 1, LANES), ((0, 0), (0, 0), (0, LANES - 1), (0, 0)))


def _dec_bias(tb, n_kv, g, qpos, kpos, win=None):
    t = _bias_table(tb, n_kv * g, np.full(DEC_ROWS, qpos), kpos, win)[:, 0]
    return jnp.pad(t.reshape(n_kv, g, -1), ((0, 0), (0, DEC_ROWS - g), (0, 0)))


def _dec_out(o, db, n_kv, g):
    return o[:, :, :g, HEAD_DIM:].reshape(db, n_kv * g * HEAD_DIM)


def _dec_state(x_t, db, n_heads):
    return x_t.reshape(2, n_heads, HEAD_DIM, db).transpose(3, 0, 1, 2).reshape(db, 1, 2, n_heads, HEAD_DIM)


def _pool_t(pool):
    return pool.transpose(0, 1, 3, 4, 5, 2)


def _even_sample(h, P, j, li, c_cmp, c_slc, c_win, c_sb, page_table):
    db = h.shape[0]
    w_in = P['w_in_even'][j]
    q_gain, k_gain = P['nsa_q_norm'][j], P['nsa_k_norm'][j]
    cut = lambda k: w_in[:, _E_COLS[k][0]:_E_COLS[k][1]]
    hkv, g = NSA_KV_HEADS, G_NSA
    kvw = cut('kv')
    w_row = jnp.concatenate([cut('q'), cut('qs'), _gate_cols(cut('g')),
                             _pair_cols(kvw[:, 256:384], kvw[:, 384:512], hkv),
                             _pair_cols(kvw[:, 512:640], kvw[:, 640:768], hkv),
                             _pair_cols(cut('sb')[:, :256], cut('sb')[:, 256:], SB_KV_HEADS)], axis=1).astype(BF16)
    one = jnp.ones((HEAD_DIM,), F32)
    cs = jnp.concatenate([jnp.tile(q_gain, NSA_HEADS) * SCALE, jnp.full((512,), SCALE, F32), jnp.ones((256,), F32),
                          jnp.tile(jnp.concatenate([k_gain[1], one]), hkv), jnp.tile(jnp.concatenate([k_gain[2], one]), hkv),
                          jnp.ones((512,), F32)])
    row_modes = ('norm',) * 4 + ('scale',) * 4 + ('sigmoid',) * 2 + ('normk',) * 4 + ('scale',) * 4
    w_t, gt = _even_t_weights(w_in, k_gain)
    q, qs, gates, slc_new, win_new, sb_new, cmp_t, slc_t, win_t, sb_t = _proj(
        h, 1, db, P['norm_mix'][li], w_row, cs, row_modes, ((0, 4), (4, 8), (8, 10), (10, 12), (12, 14), (14, 18)),
        (BF16, BF16, F32, F32, F32, F32), w_t, gt, _E_T_MODES, _E_T_OUTS)
    tb = P['rel_bias_table']
    qp = _dec_q(q, db, hkv, g)
    gcol = lambda br: jnp.pad(gates.reshape(db, hkv, LANES)[:, :, br * g:(br + 1) * g], ((0, 0), (0, 0), (0, DEC_ROWS - g)))[..., None]
    past = page_table.shape[1] * PAGE_SIZE
    tp = past + LANES
    kpos = np.where(np.arange(tp) <= past, np.arange(tp), -1)

    ncp = past // CMP_STRIDE
    nc = (past + 1 - CMP_LEN) // CMP_STRIDE + 1
    kcvc = _dec_cmp_mlp(_gather_kv(_pool_t(c_cmp), j, page_table), P['nsa_cmp_w1'][j], P['nsa_cmp_pos'][j],
                        P['nsa_cmp_w2'][j], k_gain[0])
    kpos_c = np.where(np.arange(ncp) < nc, CMP_STRIDE * np.arange(ncp) + CMP_LEN - 1, -1)
    nsb = -(-(past + 1) // SEL_BLOCK)
    o, imp = _dec_attn(qp, kcvc, None, _dec_bias(tb, hkv, g, past, kpos_c), g,
                       acc=jnp.zeros((db, hkv, DEC_ROWS, LANES), F32), gate=gcol(0), c2s_t=_cmp_to_sel_t(ncp, nc, nsb))
    nsel = _rank_select(imp[:, :, 0].transpose(1, 2, 0), past, SEL_BLOCK, N_SEL)
    qa = jnp.broadcast_to(nsel.transpose(1, 0, 2)[:, :, None], (db, hkv, DEC_ROWS, NBLK_PAD))

    o = _dec_attn(qp, _gather_kv(_pool_t(c_slc), j, page_table), _dec_new(slc_new, db, hkv),
                  _dec_bias(tb, hkv, g, past, kpos), g, qa=qa, onehot_t=_block_onehot_neg_t(tp, SEL_BLOCK),
                  acc=o, gate=gcol(1))

    cw_t = _pool_t(c_win)
    wb = c_win.shape[2]
    kpos_w = np.where(np.arange(wb + LANES) <= wb, past - wb + np.arange(wb + LANES), -1)
    o = _dec_attn(qp, _win_rows(cw_t, j), _dec_new(win_new, db, hkv),
                  _dec_bias(tb, hkv, g, past, kpos_w, win=WINDOW), g, acc=o, gate=gcol(2))

    o_b = _dec_sb(_dec_q(qs, db, SB_KV_HEADS, G_SB), _gather_kv(_pool_t(c_sb), j, page_table))
    h = _out_proj([_dec_out(o, db, hkv, g), _dec_out(o_b, db, SB_KV_HEADS, G_SB)], P['w_out_even'][j].astype(BF16), h)

    win_state = _win_shift(cw_t, j, win_t[0].T[:, :, None]).transpose(0, 4, 1, 2, 3)
    return h, (_dec_state(cmp_t, db, hkv), _dec_state(slc_t, db, hkv), win_state, _dec_state(sb_t, db, SB_KV_HEADS))


def _odd_sample(h, P, j, li, c_moba, page_table):
    db = h.shape[0]
    hkv, g = MOBA_KV_HEADS, G_MOBA
    w_in = P['w_in_odd'][j]
    w_row = jnp.concatenate([w_in[:, :1024], _pair_cols(w_in[:, 1024:1280], w_in[:, 1280:], hkv)], axis=1).astype(BF16)
    one = jnp.ones((HEAD_DIM,), F32)
    cs = jnp.concatenate([jnp.tile(P['moba_q_norm'][j], MOBA_HEADS) * SCALE, jnp.tile(jnp.concatenate([P['moba_k_norm'][j], one]), hkv)])
    w_t = w_in[:, 1024:].T.astype(BF16)
    gt = jnp.concatenate([jnp.tile(P['moba_k_norm'][j], hkv), jnp.ones((hkv * HEAD_DIM,), F32)])
    q, kv_new, kv_t = _proj(h, 1, db, P['norm_mix'][li], w_row, cs, ('norm',) * 8 + ('normk',) * 4, ((0, 8), (8, 12)),
                            (BF16, F32), w_t, gt, ('norm',) * 4 + ('raw',) * 4, ((0, 8),))
    past = page_table.shape[1] * PAGE_SIZE
    tp = past + LANES
    kpos = np.where(np.arange(tp) <= past, np.arange(tp), -1)
    o = _dec_moba(_dec_q(q, db, hkv, g), _gather_kv(_pool_t(c_moba), j, page_table), _dec_new(kv_new, db, hkv),
                  _dec_bias(P['rel_bias_table'], hkv, g, past, kpos), _block_onehot_neg_t(tp, MOBA_BLOCK), past)
    h = _out_proj([_dec_out(o, db, hkv, g)], P['w_out_odd'][j].astype(BF16), h)
    return h, (_dec_state(kv_t, db, hkv),)


def _trunk_sample(x, p, P, caches, page_table):
    db, t, d = x.shape
    h = x.reshape(db * t, d)
    c_cmp, c_slc, c_win, c_sb, c_moba = caches
    even_states, odd_states = [], []
    for li in range(p.shape[0]):
        j = li // 2
        if li % 2 == 0:
            h, st = _even_sample(h, P, j, li, c_cmp, c_slc, c_win, c_sb, page_table)
            even_states.append(st)
        else:
            h, st = _odd_sample(h, P, j, li, c_moba, page_table)
            odd_states.append(st)
        h = _ffn_and_pe(h, p[li].reshape(db * t, -1), P, li)
    return h.reshape(db, t, d), even_states, odd_states


def kernel(x_prompt, x_sample, cache_nsa_cmp, cache_nsa_slc, cache_nsa_win, cache_sb, cache_moba, page_table,
           p_prompt, p_sample, rel_bias_table, norm_mix, norm_ffn, norm_pe, w_pe_in, w_pe_gate,
           w_in_even, w_out_even, nsa_q_norm, nsa_k_norm, nsa_cmp_pos, nsa_cmp_w1, nsa_cmp_w2,
           w_ffn_gu, w_ffn_down, w_in_odd, w_out_odd, moba_q_norm, moba_k_norm,
           w_router, b_router, w_moe_gu, w_moe_down):
    P = {'rel_bias_table': rel_bias_table, 'norm_mix': norm_mix, 'norm_ffn': norm_ffn, 'norm_pe': norm_pe,
         'w_pe_in': w_pe_in, 'w_pe_gate': w_pe_gate, 'w_in_even': w_in_even, 'w_out_even': w_out_even,
         'nsa_q_norm': nsa_q_norm, 'nsa_k_norm': nsa_k_norm, 'nsa_cmp_pos': nsa_cmp_pos, 'nsa_cmp_w1': nsa_cmp_w1,
         'nsa_cmp_w2': nsa_cmp_w2, 'w_ffn_gu': w_ffn_gu, 'w_ffn_down': w_ffn_down, 'w_in_odd': w_in_odd,
         'w_out_odd': w_out_odd, 'moba_q_norm': moba_q_norm, 'moba_k_norm': moba_k_norm, 'w_router': w_router,
         'b_router': b_router, 'w_moe_gu': w_moe_gu, 'w_moe_down': w_moe_down}
    y_prompt, pe, po = _trunk_prompt(x_prompt, p_prompt, P)
    caches = (cache_nsa_cmp, cache_nsa_slc, cache_nsa_win, cache_sb, cache_moba)
    y_sample, se, so = _trunk_sample(x_sample, p_sample, P, caches, page_table)
    stack = lambda states, k: jnp.stack([s[k] for s in states])
    return (y_prompt, y_sample,
            stack(pe, 0), stack(se, 0), stack(pe, 1), stack(se, 1),
            stack(pe, 2), stack(se, 2), stack(pe, 3), stack(se, 3),
            stack(po, 0), stack(so, 0))
```

```python
import functools
import math

import numpy as np
import jax
import jax.numpy as jnp
from jax import lax
from jax.experimental import pallas as pl
from jax.experimental.pallas import tpu as pltpu

F32 = jnp.float32
BF16 = jnp.bfloat16

HEAD_DIM = 64
NSA_HEADS = 8
NSA_KV_HEADS = 2
G_NSA = 4
SB_HEADS = 8
SB_KV_HEADS = 4
G_SB = 2
MOBA_HEADS = 16
MOBA_KV_HEADS = 4
G_MOBA = 4
CMP_LEN = 32
CMP_STRIDE = 16
SEL_BLOCK = 64
N_SEL = 16
WINDOW = 512
MOBA_BLOCK = 256
MOBA_TOPK = 3
N_BUCKETS = 32
REL_MAX_DIST = 1024
N_EXPERTS = 8
PAGE_SIZE = 128
SCALE = HEAD_DIM ** -0.5
NEG_INF = -1e30
SEL_BIG = 1e9
EPS = 1e-6

LANES = 128
SUBLANES = 8
NBLK_PAD = LANES
TQ = 256
TK = 256
FLASH_HEADS = 4
SB_TQ = 512
SB_CUT = 110.0
DEC_ROWS = 16
VMEM_LIMIT = 56 * 1024 * 1024


def _bucket_thresholds():
    n = np.arange(0, 2 * REL_MAX_DIST, dtype=np.float64)
    exact = N_BUCKETS // 2
    nf = np.maximum(n, 1.0)
    big = exact + (np.log(nf / exact) / math.log(REL_MAX_DIST / exact) * (N_BUCKETS - exact)).astype(np.int32)
    bucket = np.where(n < exact, n.astype(np.int32), np.minimum(big, N_BUCKETS - 1))
    return tuple(int(np.argmax(bucket >= k)) for k in range(1, N_BUCKETS))


BUCKET_THR = _bucket_thresholds()
SAT_DIST = BUCKET_THR[-1]


def _cparams(*sem):
    return pltpu.CompilerParams(dimension_semantics=sem, vmem_limit_bytes=VMEM_LIMIT)


def _row_tile(m, cap):
    t = cap
    while m % t:
        t //= 2
    return t


def _split2(x):
    hi = x.astype(BF16)
    lo = (x - hi.astype(F32)).astype(BF16)
    return hi, lo


def _split3(x):
    hi = x.astype(BF16)
    r = x - hi.astype(F32)
    mid = r.astype(BF16)
    lo = (r - mid.astype(F32)).astype(BF16)
    return hi, mid, lo


def _dot(a, b):
    return jnp.dot(a, b, preferred_element_type=F32)


def _dot_nt(a, b):
    return lax.dot_general(a, b, (((1,), (1,)), ((), ())), preferred_element_type=F32)


def _rms(x, g):
    return x * lax.rsqrt(jnp.mean(x * x, axis=-1, keepdims=True) + EPS) * g


def _bias_from_dist(dist, tb_ref, h):
    val = jnp.full(dist.shape, tb_ref[0, h], F32)
    for k in range(1, N_BUCKETS):
        val = jnp.where(dist >= BUCKET_THR[k - 1], tb_ref[k, h], val)
    return val


def _left_half():
    return lax.broadcasted_iota(jnp.int32, (1, LANES), 1) < HEAD_DIM


def _half_rms_inv(x, both):
    left = _left_half()
    sq = x * x
    s_left = jnp.sum(jnp.where(left, sq, 0.0), axis=-1, keepdims=True)
    inv_l = lax.rsqrt(s_left * (1.0 / HEAD_DIM) + EPS)
    if both:
        s_all = jnp.sum(sq, axis=-1, keepdims=True)
        inv_r = lax.rsqrt((s_all - s_left) * (1.0 / HEAD_DIM) + EPS)
    else:
        inv_r = 1.0
    return jnp.where(left, inv_l, inv_r)


def _proj_kernel(h_ref, g_ref, wr_ref, cs_ref, wt_ref, gt_ref, *outs, row_modes, row_outs, t_modes, t_outs):
    xn = _rms(h_ref[...], g_ref[...]).astype(BF16)
    y = _dot(xn, wr_ref[...])
    for (c0, c1), o_ref in zip(row_outs, outs[:len(row_outs)]):
        for c in range(c0, c1):
            sl = slice(c * LANES, (c + 1) * LANES)
            x = y[:, sl]
            mode = row_modes[c]
            if mode in ('norm', 'normk'):
                x = x * _half_rms_inv(x, mode == 'norm')
            x = x * cs_ref[:, sl]
            if mode == 'sigmoid':
                x = jax.nn.sigmoid(x)
            o_ref[:, (c - c0) * LANES:(c - c0 + 1) * LANES] = x.astype(o_ref.dtype)
    yt = _dot_nt(wt_ref[...], xn)
    for (r0, r1), o_ref in zip(t_outs, outs[len(row_outs):]):
        for r in range(r0, r1):
            sl = slice(r * HEAD_DIM, (r + 1) * HEAD_DIM)
            x = yt[sl, :]
            if t_modes[r] == 'norm':
                x = x * lax.rsqrt(jnp.mean(x * x, axis=0, keepdims=True) + EPS) * gt_ref[sl, :]
            o_ref[0, (r - r0) * HEAD_DIM:(r - r0 + 1) * HEAD_DIM, :] = x


def _proj(h, b, t, g, w_row, cs, row_modes, row_outs, row_dtypes, w_t, gt, t_modes, t_outs):
    n, d = h.shape
    tm = min(512, t)
    nt = t // tm
    cr, ct = w_row.shape[1], w_t.shape[0]
    row = lambda bi, i: (bi * nt + i, 0)
    const = lambda bi, i: (0, 0)
    out_specs = [pl.BlockSpec((tm, (c1 - c0) * LANES), row) for c0, c1 in row_outs]
    out_shape = [jax.ShapeDtypeStruct((n, (c1 - c0) * LANES), dt) for (c0, c1), dt in zip(row_outs, row_dtypes)]
    out_specs += [pl.BlockSpec((1, (r1 - r0) * HEAD_DIM, tm), lambda bi, i: (bi, 0, i)) for r0, r1 in t_outs]
    out_shape += [jax.ShapeDtypeStruct((b, (r1 - r0) * HEAD_DIM, t), F32) for r0, r1 in t_outs]
    return pl.pallas_call(
        functools.partial(_proj_kernel, row_modes=row_modes, row_outs=row_outs, t_modes=t_modes, t_outs=t_outs),
        grid=(b, nt),
        in_specs=[pl.BlockSpec((tm, d), row),
                  pl.BlockSpec((1, d), const),
                  pl.BlockSpec((d, cr), const),
                  pl.BlockSpec((1, cr), const),
                  pl.BlockSpec((ct, d), const),
                  pl.BlockSpec((ct, 1), const)],
        out_specs=out_specs,
        out_shape=out_shape,
        compiler_params=_cparams("parallel", "parallel"),
        name="proj",
    )(h, g.reshape(1, d), w_row, cs.reshape(1, cr), w_t, gt.reshape(ct, 1))


def _out_proj_kernel(*refs):
    *x_refs, w_ref, r_ref, o_ref = refs
    acc = r_ref[...]
    k0 = 0
    for x_ref in x_refs:
        k = x_ref.shape[1]
        acc = acc + _dot(x_ref[...].astype(BF16), w_ref[k0:k0 + k, :])
        k0 += k
    o_ref[...] = acc


def _out_proj(xs, w, res):
    m, n = res.shape
    tm = _row_tile(m, 512)
    return pl.pallas_call(
        _out_proj_kernel,
        grid=(m // tm,),
        in_specs=[pl.BlockSpec((tm, x.shape[1]), lambda i: (i, 0)) for x in xs]
        + [pl.BlockSpec(w.shape, lambda i: (0, 0)), pl.BlockSpec((tm, n), lambda i: (i, 0))],
        out_specs=pl.BlockSpec((tm, n), lambda i: (i, 0)),
        out_shape=jax.ShapeDtypeStruct((m, n), F32),
        compiler_params=_cparams("parallel"),
        name="out_proj",
    )(*xs, w, res)


def _ffn_kernel(h_ref, g_ref, wg_ref, wu_ref, wd_ref, o_ref, xn_scr):
    f = pl.program_id(1)

    @pl.when(f == 0)
    def _():
        xn_scr[...] = _rms(h_ref[...], g_ref[...]).astype(BF16)
        o_ref[...] = h_ref[...]

    xn = xn_scr[...]
    gate = _dot(xn, wg_ref[...])
    up = _dot(xn, wu_ref[...])
    act = (gate * jax.nn.sigmoid(gate) * up).astype(BF16)
    o_ref[...] += _dot(act, wd_ref[...])


def _ffn(h, g, w_gu, w_down, tf):
    m, d = h.shape
    fdim = w_down.shape[0]
    nf = fdim // tf
    tm = _row_tile(m, 1024)
    return pl.pallas_call(
        _ffn_kernel,
        grid=(m // tm, nf),
        in_specs=[pl.BlockSpec((tm, d), lambda i, f: (i, 0)),
                  pl.BlockSpec((1, d), lambda i, f: (0, 0)),
                  pl.BlockSpec((d, tf), lambda i, f: (0, f)),
                  pl.BlockSpec((d, tf), lambda i, f: (0, f + nf)),
                  pl.BlockSpec((tf, d), lambda i, f: (f, 0))],
        out_specs=pl.BlockSpec((tm, d), lambda i, f: (i, 0)),
        out_shape=jax.ShapeDtypeStruct((m, d), F32),
        scratch_shapes=[pltpu.VMEM((tm, d), BF16)],
        compiler_params=_cparams("parallel", "arbitrary"),
        name="ffn",
    )(h, g.reshape(1, d), w_gu, w_gu, w_down)


def _router_kernel(h_ref, g_ref, w_ref, b_ref, o_ref, ot_ref):
    xn = _rms(h_ref[...], g_ref[...])
    xh, xl = _split2(xn)
    wh, wl = _split2(w_ref[...])
    logits = _dot(xh, wh) + _dot(xh, wl) + _dot(xl, wh) + b_ref[...]
    lane = lax.broadcasted_iota(jnp.int32, logits.shape, 1)
    logits = jnp.where(lane < N_EXPERTS, logits, -jnp.inf)
    v1 = jnp.max(logits, axis=-1, keepdims=True)
    i1 = jnp.min(jnp.where(logits == v1, lane, LANES), axis=-1, keepdims=True)
    rest = jnp.where(lane == i1, -jnp.inf, logits)
    v2 = jnp.max(rest, axis=-1, keepdims=True)
    i2 = jnp.min(jnp.where(rest == v2, lane, LANES), axis=-1, keepdims=True)
    e2 = jnp.exp(v2 - v1)
    w1 = 1.0 / (1.0 + e2)
    w2 = e2 / (1.0 + e2)
    gates = jnp.where(lane == i1, w1, 0.0) + jnp.where(lane == i2, w2, 0.0)
    o_ref[...] = gates
    for c in range(gates.shape[0] // LANES):
        ot_ref[:, c * LANES:(c + 1) * LANES] = gates[c * LANES:(c + 1) * LANES, :].T


def _router(h, g, w_r, b_r):
    m, d = h.shape
    tm = _row_tile(m, 512)
    w_pad = jnp.pad(w_r, ((0, 0), (0, LANES - N_EXPERTS)))
    b_pad = jnp.pad(b_r, (0, LANES - N_EXPERTS)).reshape(1, LANES)
    return pl.pallas_call(
        _router_kernel,
        grid=(m // tm,),
        in_specs=[pl.BlockSpec((tm, d), lambda i: (i, 0)),
                  pl.BlockSpec((1, d), lambda i: (0, 0)),
                  pl.BlockSpec((d, LANES), lambda i: (0, 0)),
                  pl.BlockSpec((1, LANES), lambda i: (0, 0))],
        out_specs=[pl.BlockSpec((tm, LANES), lambda i: (i, 0)), pl.BlockSpec((LANES, tm), lambda i: (0, i))],
        out_shape=[jax.ShapeDtypeStruct((m, LANES), F32), jax.ShapeDtypeStruct((LANES, m), F32)],
        compiler_params=_cparams("parallel"),
        name="router",
    )(h, g.reshape(1, d), w_pad, b_pad)


MOE_CAP = 320


def _moe_kernel(h_ref, g_ref, gate_ref, gate_t_ref, wg_ref, wu_ref, wd_ref, o_ref,
                xn_scr, pos_scr, pos_t_scr, gate_t_scr, posc_scr, gatec_scr, xg_scr, acc_scr, cnt_scr):
    e = pl.program_id(1)
    f = pl.program_id(2)
    tm = h_ref.shape[0]
    cap = xg_scr.shape[1]

    @pl.when((e == 0) & (f == 0))
    def _():
        xn_scr[...] = _rms(h_ref[...], g_ref[...]).astype(BF16)
        o_ref[...] = h_ref[...]
        r = lax.broadcasted_iota(jnp.int32, (tm, tm), 0)
        c = lax.broadcasted_iota(jnp.int32, (tm, tm), 1)
        mask = jnp.where(gate_ref[...] > 0, 1.0, 0.0).astype(BF16)
        pos_scr[...] = _dot(jnp.where(c < r, 1.0, 0.0).astype(BF16), mask)
        mask_t = jnp.where(gate_t_ref[...] > 0, 1.0, 0.0).astype(BF16)
        pos_t = _dot(mask_t, jnp.where(r < c, 1.0, 0.0).astype(BF16))
        for ee in range(N_EXPERTS):
            pos_t_scr[ee] = pos_t[ee:ee + 1, :]
            gate_t_scr[ee] = gate_t_ref[ee:ee + 1, :]

    @pl.when(f == 0)
    def _():
        sel = lax.broadcasted_iota(jnp.int32, (tm, LANES), 1) == e
        posc_scr[...] = jnp.sum(jnp.where(sel, pos_scr[...], 0.0), axis=-1, keepdims=True)
        gatec_scr[...] = jnp.sum(jnp.where(sel, gate_ref[...], 0.0), axis=-1, keepdims=True)
        pos_r = pos_t_scr[e]
        routed = gate_t_scr[e] > 0
        cnt = jnp.sum(jnp.where(routed, 1.0, 0.0)).astype(jnp.int32)
        cnt_scr[0] = cnt

        def gather(ci, carry):
            slot = (ci * cap + lax.broadcasted_iota(jnp.int32, (cap, tm), 0)).astype(F32)
            gmat = jnp.where((pos_r == slot) & routed, 1.0, 0.0).astype(BF16)
            xg_scr[ci] = _dot(gmat, xn_scr[...]).astype(BF16)
            acc_scr[ci] = jnp.zeros(acc_scr.shape[1:], F32)
            return carry

        lax.fori_loop(0, (cnt + cap - 1) // cap, gather, 0)

    nchunk = (cnt_scr[0] + cap - 1) // cap

    def expert(ci, carry):
        x = xg_scr[ci]
        gate = _dot(x, wg_ref[0])
        up = _dot(x, wu_ref[0])
        act = (gate * jax.nn.sigmoid(gate) * up).astype(BF16)
        acc_scr[ci] += _dot(act, wd_ref[0])
        return carry

    lax.fori_loop(0, nchunk, expert, 0)

    @pl.when(f == pl.num_programs(2) - 1)
    def _():
        posc = posc_scr[...]
        gatec = gatec_scr[...]

        def scatter(ci, carry):
            slot = (ci * cap + lax.broadcasted_iota(jnp.int32, (tm, cap), 1)).astype(F32)
            smat = jnp.where((posc == slot) & (gatec > 0), 1.0, 0.0).astype(BF16)
            yh, yl = _split2(acc_scr[ci])
            o_ref[...] += gatec * (_dot(smat, yh) + _dot(smat, yl))
            return carry

        lax.fori_loop(0, nchunk, scatter, 0)


def _moe(h, g, gates, gates_t, w_gu, w_down, tf):
    m, d = h.shape
    ne, fdim, _ = w_down.shape
    nf = fdim // tf
    tm = _row_tile(m, 1024)
    cap = min(MOE_CAP, tm)
    nch = -(-tm // cap)
    return pl.pallas_call(
        _moe_kernel,
        grid=(m // tm, ne, nf),
        in_specs=[pl.BlockSpec((tm, d), lambda i, e, f: (i, 0)),
                  pl.BlockSpec((1, d), lambda i, e, f: (0, 0)),
                  pl.BlockSpec((tm, LANES), lambda i, e, f: (i, 0)),
                  pl.BlockSpec((LANES, tm), lambda i, e, f: (0, i)),
                  pl.BlockSpec((1, d, tf), lambda i, e, f: (e, 0, f)),
                  pl.BlockSpec((1, d, tf), lambda i, e, f: (e, 0, f + nf)),
                  pl.BlockSpec((1, tf, d), lambda i, e, f: (e, f, 0))],
        out_specs=pl.BlockSpec((tm, d), lambda i, e, f: (i, 0)),
        out_shape=jax.ShapeDtypeStruct((m, d), F32),
        scratch_shapes=[pltpu.VMEM((tm, d), BF16), pltpu.VMEM((tm, LANES), F32),
                        pltpu.VMEM((N_EXPERTS, 1, tm), F32), pltpu.VMEM((N_EXPERTS, 1, tm), F32),
                        pltpu.VMEM((tm, 1), F32), pltpu.VMEM((tm, 1), F32),
                        pltpu.VMEM((nch, cap, d), BF16), pltpu.VMEM((nch, cap, d), F32),
                        pltpu.SMEM((1,), jnp.int32)],
        compiler_params=_cparams("parallel", "arbitrary", "arbitrary"),
        name="moe",
    )(h, g.reshape(1, d), gates, gates_t, w_gu, w_gu, w_down)


def _pe_kernel(h_ref, g_ref, p_ref, wg_ref, wi_ref, o_ref):
    h = h_ref[...]
    hn = _rms(h, g_ref[...]).astype(BF16)
    gate = jax.nn.sigmoid(_dot(hn, wg_ref[...]))
    o_ref[...] = h + _dot(p_ref[...].astype(BF16), wi_ref[...]) * gate


def _pe_update(h, g, p, w_gate, w_in):
    m, d = h.shape
    pd = p.shape[1]
    tm = _row_tile(m, 512)
    return pl.pallas_call(
        _pe_kernel,
        grid=(m // tm,),
        in_specs=[pl.BlockSpec((tm, d), lambda i: (i, 0)),
                  pl.BlockSpec((1, d), lambda i: (0, 0)),
                  pl.BlockSpec((tm, pd), lambda i: (i, 0)),
                  pl.BlockSpec((d, d), lambda i: (0, 0)),
                  pl.BlockSpec((pd, d), lambda i: (0, 0))],
        out_specs=pl.BlockSpec((tm, d), lambda i: (i, 0)),
        out_shape=jax.ShapeDtypeStruct((m, d), F32),
        compiler_params=_cparams("parallel"),
        name="pe_update",
    )(h, g.reshape(1, d), p, w_gate, w_in)


def _bias_kernel(tb_ref, qp_ref, kp_ref, o_ref, *, win):
    h = pl.program_id(0)
    kp = kp_ref[...]
    dist = qp_ref[...] - kp
    ok = (kp >= 0) & (dist >= 0)
    if win is not None:
        ok = ok & (dist < win)
    o_ref[0] = jnp.where(ok, _bias_from_dist(dist, tb_ref, h), NEG_INF)


def _bias_table(tb, n_heads, qpos, kpos, win=None):
    r, c = qpos.shape[0], kpos.shape[0]
    rb = _row_tile(r, 256)
    return pl.pallas_call(
        functools.partial(_bias_kernel, win=win),
        grid=(n_heads, r // rb),
        in_specs=[pl.BlockSpec(memory_space=pltpu.SMEM),
                  pl.BlockSpec((rb, 1), lambda h, i: (i, 0)),
                  pl.BlockSpec((1, c), lambda h, i: (0, 0))],
        out_specs=pl.BlockSpec((1, rb, c), lambda h, i: (h, i, 0)),
        out_shape=jax.ShapeDtypeStruct((n_heads, r, c), F32),
        compiler_params=_cparams("parallel", "parallel"),
        name="bias_table",
    )(tb, jnp.asarray(qpos, jnp.int32).reshape(r, 1), jnp.asarray(kpos, jnp.int32).reshape(1, c))


def _tile_bias(tb, n_kv, g, nd, win=None):
    t = _bias_table(tb, n_kv * g, np.arange(nd * TQ), np.arange(TK), win)
    t = t.reshape(n_kv, g, nd, TQ, TK).transpose(0, 2, 1, 3, 4)
    return jnp.concatenate([t, jnp.full((n_kv, 1, g, TQ, TK), NEG_INF, F32)], axis=1)


def _rank_nsel(score, sc_scr, nloop, kth, thresh):
    nb, n = score.shape
    sc_scr[...] = score
    out = []
    for c in range(n // LANES):
        sl = slice(c * LANES, (c + 1) * LANES)
        sc = score[:, sl]
        blk = lax.broadcasted_iota(jnp.int32, sc.shape, 0)

        def body(jg, cnt, sc=sc, blk=blk, sl=sl):
            grp = sc_scr[pl.ds(pl.multiple_of(jg * SUBLANES, SUBLANES), SUBLANES), sl]
            for r in range(SUBLANES):
                row = grp[r:r + 1, :]
                beats = (row > sc) | ((row == sc) & (jg * SUBLANES + r < blk))
                cnt = cnt + jnp.where(beats, 1.0, 0.0)
            return cnt

        cnt = lax.fori_loop(0, (nloop + SUBLANES - 1) // SUBLANES, body, jnp.zeros(sc.shape, F32))
        sel = (cnt < kth) & (sc > thresh)
        out.append(jnp.where(sel, 0.0, 1.0))
    return out


def _pos_term(pos_ref, w1_ref):
    ph, plo = _split2(pos_ref[0])
    wh, wl = _split2(w1_ref[0])
    return (_dot(ph, wh) + _dot(ph, wl) + _dot(plo, wh))[0:1]


def _cmp_mlp_kernel(x_ref, w1a_ref, w1b_ref, pos_ref, w1_ref, w2_ref, g_ref, o_ref, *, norm):
    nb, nc, _ = x_ref.shape
    x = x_ref[...].reshape(nb * nc, x_ref.shape[2])
    a = _dot(x, w1a_ref[0])
    b = _dot(x, w1b_ref[0])
    z = a + pltpu.roll(b, nb * nc - 1, 0) + _pos_term(pos_ref, w1_ref)
    y = _dot(jax.nn.gelu(z).astype(BF16), w2_ref[0])
    if norm:
        y = _rms(y, g_ref[...])
    o_ref[...] = y.reshape(nb, nc, HEAD_DIM)


def _cmp_mlp(x, w1, pos, w2, gain, kv, norm):
    s, nc, cw = x.shape
    nb = _row_tile(s, max(1, 1024 // nc))
    half = CMP_STRIDE * HEAD_DIM
    w1f = w1.reshape(2, CMP_LEN * HEAD_DIM, HEAD_DIM)
    w1a = w1f[:, :half].astype(BF16)
    w1b = w1f[:, half:].astype(BF16)
    posf = jnp.broadcast_to(pos.reshape(2, 1, CMP_LEN * HEAD_DIM), (2, SUBLANES, CMP_LEN * HEAD_DIM))
    return pl.pallas_call(
        functools.partial(_cmp_mlp_kernel, norm=norm),
        grid=(s // nb,),
        in_specs=[pl.BlockSpec((nb, nc, cw), lambda i: (i, 0, 0)),
                  pl.BlockSpec((1, half, HEAD_DIM), lambda i: (kv, 0, 0)),
                  pl.BlockSpec((1, half, HEAD_DIM), lambda i: (kv, 0, 0)),
                  pl.BlockSpec((1, SUBLANES, 2 * half), lambda i: (kv, 0, 0)),
                  pl.BlockSpec((1, 2 * half, HEAD_DIM), lambda i: (kv, 0, 0)),
                  pl.BlockSpec((1, HEAD_DIM, HEAD_DIM), lambda i: (kv, 0, 0)),
                  pl.BlockSpec((1, HEAD_DIM), lambda i: (0, 0))],
        out_specs=pl.BlockSpec((nb, nc, HEAD_DIM), lambda i: (i, 0, 0)),
        out_shape=jax.ShapeDtypeStruct((s, nc, HEAD_DIM), F32),
        compiler_params=_cparams("parallel"),
        name="cmp_mlp",
    )(x, w1a, w1b, posf, w1f, w2.astype(BF16), gain.reshape(1, HEAD_DIM))


def _stack_heads(q_ref, g):
    return jnp.concatenate([q_ref[:, gi * HEAD_DIM:(gi + 1) * HEAD_DIM] for gi in range(g)], axis=0)


def _unstack_heads(o, g, tq):
    return [o[gi * tq:(gi + 1) * tq] for gi in range(g)]


def _nsa_cmp_kernel(tb_ref, q_ref, kc_ref, vc_ref, c2s_ref, gate_ref, o_ref, nsel_ref, s_scr, sc_scr, *, n_kv, g):
    tq = q_ref.shape[0]
    ncp = kc_ref.shape[1]
    r = g * tq
    kvh = pl.program_id(0) % n_kv
    t0 = pl.program_id(1) * tq
    q = _stack_heads(q_ref, g)
    for cc in range(ncp // LANES):
        c0 = cc * LANES
        sl = slice(c0, c0 + LANES)
        min_kp = CMP_STRIDE * c0 + CMP_LEN - 1
        max_kp = CMP_STRIDE * (c0 + LANES - 1) + CMP_LEN - 1
        future = min_kp > t0 + (tq - 1)
        sat = t0 - max_kp >= SAT_DIST

        @pl.when(future)
        def _():
            s_scr[:, sl] = jnp.full((r, LANES), NEG_INF, F32)

        @pl.when(sat)
        def _():
            s = _dot_nt(q, kc_ref[0, sl, :]).reshape(g, tq, LANES)
            for gi in range(g):
                s_scr[gi * tq:(gi + 1) * tq, sl] = s[gi] + tb_ref[N_BUCKETS - 1, kvh * g + gi]

        @pl.when(jnp.logical_not(future | sat))
        def _():
            s = _dot_nt(q, kc_ref[0, sl, :]).reshape(g, tq, LANES)
            t = t0 + lax.broadcasted_iota(jnp.int32, (tq, LANES), 0)
            kp = CMP_STRIDE * (c0 + lax.broadcasted_iota(jnp.int32, (tq, LANES), 1)) + (CMP_LEN - 1)
            dist = t - kp
            for gi in range(g):
                b = _bias_from_dist(dist, tb_ref, kvh * g + gi)
                s_scr[gi * tq:(gi + 1) * tq, sl] = jnp.where(dist >= 0, s[gi] + b, NEG_INF)

    s = s_scr[...]
    m = jnp.max(s, axis=-1, keepdims=True)
    e = jnp.where(s > 0.5 * NEG_INF, jnp.exp(s - m), 0.0)
    p = e / jnp.maximum(jnp.sum(e, axis=-1, keepdims=True), 1e-30)
    o = _unstack_heads(_dot(p.astype(BF16), vc_ref[0]), g, tq)
    o_ref[...] = jnp.concatenate([o[gi] * gate_ref[:, gi:gi + 1] for gi in range(g)], axis=-1)

    psum = jnp.sum(p.reshape(g, tq, ncp), axis=0)
    ph, plo = _split2(psum)
    c2s = c2s_ref[...]
    imp_t = _dot_nt(c2s, ph) + _dot_nt(c2s, plo)
    blk = lax.broadcasted_iota(jnp.int32, (NBLK_PAD, tq), 0)
    t = t0 + lax.broadcasted_iota(jnp.int32, (NBLK_PAD, tq), 1)
    valid = SEL_BLOCK * blk <= t
    forced = (blk == t // SEL_BLOCK) | (blk == 0)
    score = jnp.where(forced, SEL_BIG, jnp.where(valid, imp_t, -SEL_BIG))
    nloop = jnp.minimum((t0 + tq - 1) // SEL_BLOCK + 1, NBLK_PAD)
    cols = _rank_nsel(score, sc_scr, nloop, N_SEL, -jnp.inf)
    for c, nsel_t in enumerate(cols):
        nsel_ref[0, c * LANES:(c + 1) * LANES, :] = nsel_t.T.astype(BF16)


def _nsa_cmp(tb, q, kc, vc, c2s_t, gates, b, t, n_kv, g):
    ncp = kc.shape[1]
    tq = min(TQ, t)
    nq = t // tq
    gw = g * HEAD_DIM
    row = lambda bg, i: ((bg // n_kv) * nq + i, bg % n_kv)
    return pl.pallas_call(
        functools.partial(_nsa_cmp_kernel, n_kv=n_kv, g=g),
        grid=(b * n_kv, nq),
        in_specs=[pl.BlockSpec(memory_space=pltpu.SMEM),
                  pl.BlockSpec((tq, gw), row),
                  pl.BlockSpec((1, ncp, HEAD_DIM), lambda bg, i: (bg, 0, 0)),
                  pl.BlockSpec((1, ncp, HEAD_DIM), lambda bg, i: (bg, 0, 0)),
                  pl.BlockSpec((NBLK_PAD, ncp), lambda bg, i: (0, 0)),
                  pl.BlockSpec((tq, LANES), row)],
        out_specs=[pl.BlockSpec((tq, gw), row),
                   pl.BlockSpec((1, tq, NBLK_PAD), lambda bg, i: (bg, i, 0))],
        out_shape=[jax.ShapeDtypeStruct((b * t, n_kv * gw), F32),
                   jax.ShapeDtypeStruct((b * n_kv, t, NBLK_PAD), BF16)],
        scratch_shapes=[pltpu.VMEM((g * tq, ncp), F32), pltpu.VMEM((NBLK_PAD, tq), F32)],
        compiler_params=_cparams("parallel", "parallel"),
        name="nsa_cmp",
    )(tb, q, kc, vc, c2s_t, gates)


def _flash_kernel(*refs, g, look, aug, shared_nsel, branch):
    it = iter(refs)
    q_ref = next(it)
    nsel_ref = next(it) if aug else None
    kt_ref, vt_ref = next(it), next(it)
    oh_ref = next(it) if aug else None
    b_ref = next(it)
    acc_ref, gate_ref = (next(it), next(it)) if branch is not None else (None, None)
    o_ref, k_scr, v_scr, q_scr, m_scr, acc_scr = it
    tq = q_ref.shape[0]
    nt = k_scr.shape[0]
    nd = b_ref.shape[1]
    r = g * tq
    i = pl.program_id(1)

    @pl.when(i == 0)
    def _():
        for j in range(nt):
            cols = slice(j * TK, (j + 1) * TK)
            if aug:
                k_scr[j, 0:NBLK_PAD, :] = oh_ref[:, cols]
                k_scr[j, NBLK_PAD:NBLK_PAD + HEAD_DIM, :] = kt_ref[0, :, cols].astype(BF16)
            else:
                k_scr[j] = kt_ref[0, :, cols].astype(BF16)
            v_scr[j, 0:HEAD_DIM, :] = vt_ref[0, :, cols].astype(BF16)
            v_scr[j, HEAD_DIM:LANES, :] = jnp.ones((LANES - HEAD_DIM, TK), BF16)

    for gi in range(g):
        rows = slice(gi * tq, (gi + 1) * tq)
        qg = q_ref[:, gi * HEAD_DIM:(gi + 1) * HEAD_DIM]
        if aug:
            q_scr[rows, 0:NBLK_PAD] = nsel_ref[0] if shared_nsel else nsel_ref[:, gi * NBLK_PAD:(gi + 1) * NBLK_PAD]
            q_scr[rows, NBLK_PAD:NBLK_PAD + HEAD_DIM] = qg
        else:
            q_scr[rows, :] = qg
    m_scr[...] = jnp.full((r, LANES), NEG_INF, F32)
    acc_scr[...] = jnp.zeros((r, LANES), F32)
    lo = 0 if look is None else jnp.maximum(i - look, 0)

    def body(jj, carry):
        ja = lo + 2 * jj
        jb = ja + 1
        jb_c = jnp.minimum(jb, nt - 1)
        da = jnp.minimum(i - ja, nd - 2)
        db = jnp.where(jb > i, nd - 1, jnp.minimum(i - jb, nd - 2))
        ka, kb, va, vb = k_scr[ja], k_scr[jb_c], v_scr[ja], v_scr[jb_c]
        for gs in range(0, g, FLASH_HEADS):
            rows = slice(gs * tq, (gs + FLASH_HEADS) * tq)
            rr = FLASH_HEADS * tq
            q = q_scr[rows, :]
            sa = (_dot(q, ka).reshape(FLASH_HEADS, tq, TK) + b_ref[0, da, gs:gs + FLASH_HEADS]).reshape(rr, TK)
            sb = (_dot(q, kb).reshape(FLASH_HEADS, tq, TK) + b_ref[0, db, gs:gs + FLASH_HEADS]).reshape(rr, TK)
            m_prev = m_scr[rows, :]
            m_cur = jnp.maximum(jnp.max(sa, axis=-1, keepdims=True), jnp.max(sb, axis=-1, keepdims=True))
            m_new = jnp.maximum(m_prev, m_cur)
            alpha = jnp.exp(m_prev - m_new)
            m_rep = jnp.concatenate([m_new] * (TK // LANES), axis=-1)
            pa = jnp.exp(sa - m_rep).astype(BF16)
            pb = jnp.exp(sb - m_rep).astype(BF16)
            acc_scr[rows, :] = alpha * acc_scr[rows, :] + _dot_nt(pa, va) + _dot_nt(pb, vb)
            m_scr[rows, :] = m_new
        return carry

    lax.fori_loop(0, (i + 2 - lo) // 2, body, 0)
    acc = acc_scr[...]
    o = _unstack_heads(acc[:, :HEAD_DIM] / jnp.maximum(acc[:, HEAD_DIM:], 1e-30), g, tq)
    if branch is not None:
        o = [acc_ref[:, gi * HEAD_DIM:(gi + 1) * HEAD_DIM] + gate_ref[:, branch * g + gi:branch * g + gi + 1] * o[gi]
             for gi in range(g)]
    o_ref[...] = jnp.concatenate(o, axis=-1)


def _flash_attn(q, kv_t, k_blk, v_blk, bias, b, t, n_kv, g, nsel=None, onehot_t=None, look=None,
                acc=None, gates=None, branch=None):
    aug = nsel is not None
    shared = aug and nsel.ndim == 3
    nq = t // TQ
    gw = g * HEAD_DIM
    dk = HEAD_DIM + (NBLK_PAD if aug else 0)
    row = lambda bg, i: ((bg // n_kv) * nq + i, bg % n_kv)
    in_specs = [pl.BlockSpec((TQ, gw), row)]
    args = [q]
    if aug:
        in_specs.append(pl.BlockSpec((1, TQ, NBLK_PAD), lambda bg, i: (bg, i, 0)) if shared
                        else pl.BlockSpec((TQ, g * NBLK_PAD), row))
        args.append(nsel)
    in_specs += [pl.BlockSpec((1, HEAD_DIM, t), lambda bg, i: (bg // n_kv, k_blk + bg % n_kv, 0)),
                 pl.BlockSpec((1, HEAD_DIM, t), lambda bg, i: (bg // n_kv, v_blk + bg % n_kv, 0))]
    args += [kv_t, kv_t]
    if aug:
        in_specs.append(pl.BlockSpec((NBLK_PAD, t), lambda bg, i: (0, 0)))
        args.append(onehot_t)
    nd = bias.shape[1]
    in_specs.append(pl.BlockSpec((1, nd, g, TQ, TK), lambda bg, i: (bg % n_kv, 0, 0, 0, 0)))
    args.append(bias)
    if branch is not None:
        in_specs += [pl.BlockSpec((TQ, gw), row), pl.BlockSpec((TQ, LANES), row)]
        args += [acc, gates]
    return pl.pallas_call(
        functools.partial(_flash_kernel, g=g, look=look, aug=aug, shared_nsel=shared, branch=branch),
        grid=(b * n_kv, nq),
        in_specs=in_specs,
        out_specs=pl.BlockSpec((TQ, gw), row),
        out_shape=jax.ShapeDtypeStruct((b * t, n_kv * gw), F32),
        scratch_shapes=[pltpu.VMEM((t // TK, dk, TK), BF16), pltpu.VMEM((t // TK, LANES, TK), BF16),
                        pltpu.VMEM((g * TQ, dk), BF16), pltpu.VMEM((g * TQ, LANES), F32),
                        pltpu.VMEM((g * TQ, LANES), F32)],
        compiler_params=_cparams("parallel", "arbitrary"),
        name="flash_attn",
    )(*args)


def _sb_kernel(q_ref, kt_ref, vt_ref, u_ref, o_ref, k_scr, v_scr, carry_scr, acc_scr, *, g):
    tq = q_ref.shape[0]
    nt = k_scr.shape[0]
    r = g * tq
    i = pl.program_id(1)

    @pl.when(i == 0)
    def _():
        for j in range(nt):
            cols = slice(j * TK, (j + 1) * TK)
            k_scr[j] = kt_ref[0, :, cols].astype(BF16)
            v_scr[j] = vt_ref[0, :, cols].astype(BF16)

    q = _stack_heads(q_ref, g)
    carry_scr[...] = jnp.zeros((r, LANES), F32)
    acc_scr[...] = jnp.zeros((r, HEAD_DIM), F32)
    t = i * tq + lax.broadcasted_iota(jnp.int32, (g, tq, TK), 1).reshape(r, TK)
    lane = lax.broadcasted_iota(jnp.int32, (r, TK), 1)
    u = u_ref[...]

    def cond(st):
        return (st[0] >= 0) & (st[1] > 0)

    def body(st):
        j = st[0]
        z = _dot(q, k_scr[j])
        before = (j * TK + lane) < t
        sp = jnp.maximum(z, 0.0) + jnp.log1p(jnp.exp(-jnp.abs(z)))
        lk = jnp.where(before, -sp, 0.0)
        hi, lo = _split2(lk)
        carry = carry_scr[...]
        between = _dot(hi, u) + _dot(lo, u) + jnp.concatenate([carry] * (TK // LANES), axis=-1)
        a = jnp.where(before, jnp.exp(z - sp + between), 0.0)
        acc_scr[...] += _dot_nt(a.astype(BF16), v_scr[j])
        carry = carry + jnp.sum(lk, axis=-1, keepdims=True)
        carry_scr[...] = carry
        return j - 1, (jnp.max(carry) > -SB_CUT).astype(jnp.int32)

    lax.while_loop(cond, body, (((i + 1) * tq - 1) // TK, jnp.int32(1)))
    o_ref[...] = jnp.concatenate(_unstack_heads(acc_scr[...], g, tq), axis=-1)


def _sb_attn(q, kv_t, b, t, n_kv, g):
    tq = min(SB_TQ, t)
    nq = t // tq
    gw = g * HEAD_DIM
    idx = np.arange(TK)
    u = jnp.asarray(idx[:, None] > idx[None, :], BF16)
    row = lambda bg, i: ((bg // n_kv) * nq + i, bg % n_kv)
    return pl.pallas_call(
        functools.partial(_sb_kernel, g=g),
        grid=(b * n_kv, nq),
        in_specs=[pl.BlockSpec((tq, gw), row),
                  pl.BlockSpec((1, HEAD_DIM, t), lambda bg, i: (bg // n_kv, bg % n_kv, 0)),
                  pl.BlockSpec((1, HEAD_DIM, t), lambda bg, i: (bg // n_kv, n_kv + bg % n_kv, 0)),
                  pl.BlockSpec((TK, TK), lambda bg, i: (0, 0))],
        out_specs=pl.BlockSpec((tq, gw), row),
        out_shape=jax.ShapeDtypeStruct((b * t, n_kv * gw), F32),
        scratch_shapes=[pltpu.VMEM((t // TK, HEAD_DIM, TK), BF16), pltpu.VMEM((t // TK, HEAD_DIM, TK), BF16),
                        pltpu.VMEM((g * tq, LANES), F32), pltpu.VMEM((g * tq, HEAD_DIM), F32)],
        compiler_params=_cparams("parallel", "arbitrary"),
        name="sb_attn",
    )(q, kv_t, kv_t, u)


def _moba_gate_kernel(q_ref, kt_ref, a_ref, nsel_ref, km_scr, sc_scr, *, g):
    tq = q_ref.shape[0]
    i = pl.program_id(1)

    @pl.when(i == 0)
    def _():
        a = a_ref[...]
        h1, h2, h3 = _split3(kt_ref[0])
        km_t = _dot(h1, a) + _dot(h2, a) + _dot(h3, a)
        km_sq = jnp.concatenate([km_t, jnp.zeros_like(km_t)], axis=0).T
        km_scr[...] = km_sq[:, :HEAD_DIM]

    nb = sc_scr.shape[0]
    kh, kl = _split2(km_scr[0:nb, :])
    blk = lax.broadcasted_iota(jnp.int32, (nb, tq), 0)
    own = (i * tq + lax.broadcasted_iota(jnp.int32, (nb, tq), 1)) // MOBA_BLOCK
    nloop = jnp.minimum((i * tq + tq - 1) // MOBA_BLOCK + 1, nb)
    pad_rows = jnp.ones((NBLK_PAD - nb, LANES), F32)
    for gi in range(g):
        qg = q_ref[:, gi * HEAD_DIM:(gi + 1) * HEAD_DIM]
        gate_t = (_dot_nt(kh, qg) + _dot_nt(kl, qg)) * (1.0 / SCALE)
        score = jnp.where(blk == own, SEL_BIG, jnp.where(blk < own, gate_t, -SEL_BIG))
        cols = _rank_nsel(score, sc_scr, nloop, MOBA_TOPK + 1, -SEL_BIG / 2)
        for c, nsel_t in enumerate(cols):
            full = jnp.concatenate([nsel_t, pad_rows], axis=0) if nb < NBLK_PAD else nsel_t
            nsel_ref[c * LANES:(c + 1) * LANES, gi * NBLK_PAD:(gi + 1) * NBLK_PAD] = full.T.astype(BF16)


def _moba_gate(q, kv_t, b, t, n_kv, g):
    tq = min(TQ, t)
    nq = t // tq
    avg = (np.arange(t)[:, None] // MOBA_BLOCK) == np.arange(NBLK_PAD)[None, :]
    avg = jnp.asarray(np.where(avg, 1.0 / MOBA_BLOCK, 0.0), BF16)
    row = lambda bg, i: ((bg // n_kv) * nq + i, bg % n_kv)
    return pl.pallas_call(
        functools.partial(_moba_gate_kernel, g=g),
        grid=(b * n_kv, nq),
        in_specs=[pl.BlockSpec((tq, g * HEAD_DIM), row),
                  pl.BlockSpec((1, HEAD_DIM, t), lambda bg, i: (bg // n_kv, bg % n_kv, 0)),
                  pl.BlockSpec((t, NBLK_PAD), lambda bg, i: (0, 0))],
        out_specs=pl.BlockSpec((tq, g * NBLK_PAD), row),
        out_shape=jax.ShapeDtypeStruct((b * t, n_kv * g * NBLK_PAD), BF16),
        scratch_shapes=[pltpu.VMEM((NBLK_PAD, HEAD_DIM), F32),
                        pltpu.VMEM((-(-(t // MOBA_BLOCK) // SUBLANES) * SUBLANES, tq), F32)],
        compiler_params=_cparams("parallel", "arbitrary"),
        name="moba_gate",
    )(q, kv_t, avg)


def _block_onehot_neg_t(t, block):
    oh = np.arange(NBLK_PAD)[:, None] == (np.arange(t)[None, :] // block)
    return jnp.asarray(np.where(oh, NEG_INF, 0.0), BF16)


def _cmp_to_sel_t(ncp, nc, nsb):
    cs = CMP_STRIDE * np.arange(ncp)[None, :]
    ss = SEL_BLOCK * np.arange(NBLK_PAD)[:, None]
    ov = np.clip(np.minimum(cs + CMP_LEN, ss + SEL_BLOCK) - np.maximum(cs, ss), 0, None) / CMP_LEN
    ov = np.where((np.arange(ncp)[None, :] < nc) & (np.arange(NBLK_PAD)[:, None] < nsb), ov, 0.0)
    return jnp.asarray(ov, BF16)


def _rows_from_t(x_t, b, t, n_heads):
    return x_t.reshape(b, 2, n_heads, HEAD_DIM, t).transpose(0, 4, 1, 2, 3)


def _gate_cols(w_g):
    d = w_g.shape[0]
    w = w_g.reshape(d, NSA_KV_HEADS, G_NSA, 3).transpose(0, 1, 3, 2).reshape(d, NSA_KV_HEADS, 3 * G_NSA)
    return jnp.pad(w, ((0, 0), (0, 0), (0, LANES - 3 * G_NSA))).reshape(d, NSA_KV_HEADS * LANES)


_E_COLS = dict(q=(0, 512), kv=(512, 1280), g=(1280, 1304), qs=(1304, 1816), sb=(1816, 2328))
_E_T_MODES = ('raw',) * 4 + ('norm',) * 2 + ('raw',) * 2 + ('norm',) * 2 + ('raw',) * 2 + ('raw',) * 8
_E_T_OUTS = ((0, 4), (4, 8), (8, 12), (12, 20))


def _even_t_weights(w_in, k_gain):
    cut = lambda k: w_in[:, _E_COLS[k][0]:_E_COLS[k][1]]
    w_t = jnp.concatenate([cut('kv'), cut('sb')], axis=1).T.astype(BF16)
    one = jnp.ones((HEAD_DIM,), F32)
    gt = jnp.concatenate([one] * 4 + [k_gain[1]] * 2 + [one] * 2 + [k_gain[2]] * 2 + [one] * 10)
    return w_t, gt


def _pair_cols(w_k, w_v, n_heads):
    d = w_k.shape[0]
    return jnp.concatenate([w_k.reshape(d, n_heads, HEAD_DIM), w_v.reshape(d, n_heads, HEAD_DIM)], axis=2).reshape(d, n_heads * LANES)


def _even_prompt(h, b, t, P, j, li):
    w_in = P['w_in_even'][j]
    q_gain, k_gain = P['nsa_q_norm'][j], P['nsa_k_norm'][j]
    cut = lambda k: w_in[:, _E_COLS[k][0]:_E_COLS[k][1]]
    w_row = jnp.concatenate([cut('q'), cut('qs'), _gate_cols(cut('g')), cut('kv')[:, :256]], axis=1).astype(BF16)
    cs = jnp.concatenate([jnp.tile(q_gain, NSA_HEADS) * SCALE, jnp.full((512,), SCALE, F32), jnp.ones((512,), F32)])
    row_modes = ('norm',) * 4 + ('scale',) * 4 + ('sigmoid',) * 2 + ('scale',) * 2
    w_t, gt = _even_t_weights(w_in, k_gain)
    q, qs, gates, kcvc, cmp_t, slc_t, win_t, sb_t = _proj(
        h, b, t, P['norm_mix'][li], w_row, cs, row_modes, ((0, 4), (4, 8), (8, 10), (10, 12)),
        (BF16, BF16, F32, BF16), w_t, gt, _E_T_MODES, _E_T_OUTS)
    tb = P['rel_bias_table']
    hkv, g = NSA_KV_HEADS, G_NSA

    ncp = t // CMP_STRIDE
    x4 = kcvc.reshape(b, ncp, CMP_STRIDE, 2 * hkv, HEAD_DIM).transpose(0, 3, 1, 2, 4).reshape(b, 2, hkv, ncp, CMP_STRIDE * HEAD_DIM)
    w1, pos, w2 = P['nsa_cmp_w1'][j], P['nsa_cmp_pos'][j], P['nsa_cmp_w2'][j]
    kc = _cmp_mlp(x4[:, 0].reshape(b * hkv, ncp, -1), w1, pos, w2, k_gain[0], 0, True).astype(BF16)
    vc = _cmp_mlp(x4[:, 1].reshape(b * hkv, ncp, -1), w1, pos, w2, k_gain[0], 1, False).astype(BF16)
    o, nsel = _nsa_cmp(tb, q, kc, vc, _cmp_to_sel_t(ncp, ncp - 1, t // SEL_BLOCK), gates, b, t, hkv, g)

    nd_full = -(-(SAT_DIST + TK - 1) // TQ) + 1
    o = _flash_attn(q, slc_t, 0, hkv, _tile_bias(tb, hkv, g, nd_full), b, t, hkv, g, nsel=nsel,
                    onehot_t=_block_onehot_neg_t(t, SEL_BLOCK), acc=o, gates=gates, branch=1)
    look = (WINDOW - 1 + TK - 1) // TK
    o = _flash_attn(q, win_t, 0, hkv, _tile_bias(tb, hkv, g, look + 1, win=WINDOW), b, t, hkv, g, look=look,
                    acc=o, gates=gates, branch=2)

    o_b = _sb_attn(qs, sb_t, b, t, SB_KV_HEADS, G_SB)
    h = _out_proj([o, o_b], P['w_out_even'][j].astype(BF16), h)
    wb = min(WINDOW, t)
    states = (_rows_from_t(cmp_t, b, t, hkv), _rows_from_t(slc_t, b, t, hkv),
              _rows_from_t(win_t[:, :, t - wb:], b, wb, hkv), _rows_from_t(sb_t, b, t, SB_KV_HEADS))
    return h, states


def _odd_prompt(h, b, t, P, j, li):
    hkv, g = MOBA_KV_HEADS, G_MOBA
    w_in = P['w_in_odd'][j]
    w_row = w_in[:, :1024].astype(BF16)
    cs = jnp.tile(P['moba_q_norm'][j], MOBA_HEADS) * SCALE
    w_t = w_in[:, 1024:].T.astype(BF16)
    gt = jnp.concatenate([jnp.tile(P['moba_k_norm'][j], hkv), jnp.ones((hkv * HEAD_DIM,), F32)])
    q, kv_t = _proj(h, b, t, P['norm_mix'][li], w_row, cs, ('norm',) * 8, ((0, 8),), (BF16,),
                    w_t, gt, ('norm',) * 4 + ('raw',) * 4, ((0, 8),))
    tb = P['rel_bias_table']
    nsel = _moba_gate(q, kv_t, b, t, hkv, g)
    nd_full = -(-(SAT_DIST + TK - 1) // TQ) + 1
    o = _flash_attn(q, kv_t, 0, hkv, _tile_bias(tb, hkv, g, nd_full), b, t, hkv, g, nsel=nsel,
                    onehot_t=_block_onehot_neg_t(t, MOBA_BLOCK))
    h = _out_proj([o], P['w_out_odd'][j].astype(BF16), h)
    return h, (_rows_from_t(kv_t, b, t, hkv),)


def _ffn_and_pe(h, pemb, P, li):
    j = li // 2
    if li % 2 == 0:
        h = _ffn(h, P['norm_ffn'][li], P['w_ffn_gu'][j].astype(BF16), P['w_ffn_down'][j].astype(BF16), 256)
    else:
        gates, gates_t = _router(h, P['norm_ffn'][li], P['w_router'][j], P['b_router'][j])
        h = _moe(h, P['norm_ffn'][li], gates, gates_t, P['w_moe_gu'][j].astype(BF16), P['w_moe_down'][j].astype(BF16), 512)
    return _pe_update(h, P['norm_pe'][li], pemb, P['w_pe_gate'][li].astype(BF16), P['w_pe_in'][li].astype(BF16))


def _trunk_prompt(x, p, P):
    b, t, d = x.shape
    h = x.reshape(b * t, d)
    even_states, odd_states = [], []
    for li in range(p.shape[0]):
        j = li // 2
        if li % 2 == 0:
            h, st = _even_prompt(h, b, t, P, j, li)
            even_states.append(st)
        else:
            h, st = _odd_prompt(h, b, t, P, j, li)
            odd_states.append(st)
        h = _ffn_and_pe(h, p[li].reshape(b * t, -1), P, li)
    return h.reshape(b, t, d), even_states, odd_states


def _kv_rows(x_ref, h):
    kv_t = jnp.concatenate([x_ref[0, h], x_ref[1, h]], axis=0)
    n = kv_t.shape[1]
    return jnp.concatenate([kv_t[:, c * LANES:(c + 1) * LANES].T for c in range(n // LANES)], axis=0)


def _gather_kernel(pt_ref, *refs):
    o_ref = refs[-1]
    n_heads = o_ref.shape[1]
    for p, x_ref in enumerate(refs[:-1]):
        for h in range(n_heads):
            o_ref[0, h, p * PAGE_SIZE:(p + 1) * PAGE_SIZE, :] = _kv_rows(x_ref, h)


def _gather_kv(pool_t, layer, page_table):
    db, npg = page_table.shape
    _, _, two, hh, dd, ps = pool_t.shape

    def in_map(p):
        return lambda b, pt: (layer, pt[b * npg + p], 0, 0, 0, 0)

    grid_spec = pltpu.PrefetchScalarGridSpec(
        num_scalar_prefetch=1,
        grid=(db,),
        in_specs=[pl.BlockSpec((None, None, two, hh, dd, ps), in_map(p)) for p in range(npg)],
        out_specs=pl.BlockSpec((1, hh, npg * ps, LANES), lambda b, pt: (b, 0, 0, 0)),
    )
    return pl.pallas_call(
        _gather_kernel,
        grid_spec=grid_spec,
        out_shape=jax.ShapeDtypeStruct((db, hh, npg * ps, LANES), F32),
        compiler_params=_cparams("parallel"),
        name="gather_kv",
    )(page_table.reshape(-1), *([pool_t] * npg))


def _win_rows_kernel(x_ref, o_ref):
    for h in range(o_ref.shape[1]):
        o_ref[0, h] = _kv_rows(x_ref, h)


def _win_rows(win_t, layer):
    _, db, two, hh, dd, w = win_t.shape
    return pl.pallas_call(
        _win_rows_kernel,
        grid=(db,),
        in_specs=[pl.BlockSpec((None, None, two, hh, dd, w), lambda b: (layer, b, 0, 0, 0, 0))],
        out_specs=pl.BlockSpec((1, hh, w, LANES), lambda b: (b, 0, 0, 0)),
        out_shape=jax.ShapeDtypeStruct((db, hh, w, LANES), F32),
        compiler_params=_cparams("parallel"),
        name="win_rows",
    )(win_t)


def _win_shift_kernel(x_ref, new_ref, o_ref):
    x = x_ref[0]
    lane = lax.broadcasted_iota(jnp.int32, x.shape, 1)
    o_ref[0] = jnp.where(lane == x.shape[1] - 1, new_ref[0], pltpu.roll(x, x.shape[1] - 1, 1))


def _win_shift(win_t, layer, new_col):
    _, db, two, hh, dd, w = win_t.shape
    rows = two * hh * dd
    x = win_t.reshape(win_t.shape[0], db, rows, w)
    out = pl.pallas_call(
        _win_shift_kernel,
        grid=(db,),
        in_specs=[pl.BlockSpec((None, 1, rows, w), lambda b: (layer, b, 0, 0)),
                  pl.BlockSpec((1, rows, 1), lambda b: (b, 0, 0))],
        out_specs=pl.BlockSpec((1, rows, w), lambda b: (b, 0, 0)),
        out_shape=jax.ShapeDtypeStruct((db, rows, w), F32),
        compiler_params=_cparams("parallel"),
        name="win_shift",
    )(x, new_col)
    return out.reshape(db, two, hh, dd, w)


def _dec_cmp_mlp_kernel(kv_ref, w1_ref, pos_ref, w1f_ref, w2_ref, g_ref, o_ref):
    n_heads, rows = kv_ref.shape[1], kv_ref.shape[2]
    nc = rows // CMP_STRIDE
    posb = jnp.concatenate([_pos_term(pos_ref.at[0:1], w1f_ref.at[0:1]), _pos_term(pos_ref.at[1:2], w1f_ref.at[1:2])], axis=-1)
    left = _left_half()
    for h in range(n_heads):
        acc = jnp.zeros((nc, 2 * LANES), F32)
        for l in range(CMP_STRIDE):
            xl = kv_ref[0, h, pl.ds(l, nc, stride=CMP_STRIDE), :].astype(BF16)
            acc = acc + _dot(xl, w1_ref[l])
        z = acc[:, :LANES] + pltpu.roll(acc[:, LANES:], nc - 1, 0) + posb
        y = _dot(jax.nn.gelu(z).astype(BF16), w2_ref[...])
        y = y * jnp.where(left, _half_rms_inv(y, False) * g_ref[...], 1.0)
        o_ref[0, h] = y


def _dec_cmp_mlp(kv, w1, pos, w2, gain):
    db, hh, rows, _ = kv.shape
    nc = rows // CMP_STRIDE
    z64 = jnp.zeros((CMP_LEN, HEAD_DIM, HEAD_DIM), F32)
    wd = jnp.concatenate([jnp.concatenate([w1[0], z64], axis=2), jnp.concatenate([z64, w1[1]], axis=2)], axis=1)
    w1d = jnp.concatenate([wd[:CMP_STRIDE], wd[CMP_STRIDE:]], axis=2).astype(BF16)
    z2 = jnp.zeros((HEAD_DIM, HEAD_DIM), F32)
    w2d = jnp.concatenate([jnp.concatenate([w2[0], z2], axis=1), jnp.concatenate([z2, w2[1]], axis=1)], axis=0).astype(BF16)
    w1f = w1.reshape(2, CMP_LEN * HEAD_DIM, HEAD_DIM)
    posf = jnp.broadcast_to(pos.reshape(2, 1, CMP_LEN * HEAD_DIM), (2, SUBLANES, CMP_LEN * HEAD_DIM))
    gpad = jnp.concatenate([gain, jnp.ones((HEAD_DIM,), F32)]).reshape(1, LANES)
    return pl.pallas_call(
        _dec_cmp_mlp_kernel,
        grid=(db,),
        in_specs=[pl.BlockSpec((1, hh, rows, LANES), lambda b: (b, 0, 0, 0)),
                  pl.BlockSpec(w1d.shape, lambda b: (0, 0, 0)),
                  pl.BlockSpec(posf.shape, lambda b: (0, 0, 0)),
                  pl.BlockSpec(w1f.shape, lambda b: (0, 0, 0)),
                  pl.BlockSpec(w2d.shape, lambda b: (0, 0)),
                  pl.BlockSpec((1, LANES), lambda b: (0, 0))],
        out_specs=pl.BlockSpec((1, hh, nc, LANES), lambda b: (b, 0, 0, 0)),
        out_shape=jax.ShapeDtypeStruct((db, hh, nc, LANES), F32),
        compiler_params=_cparams("parallel"),
        name="dec_cmp_mlp",
    )(kv, w1d, posf, w1f, w2d, gpad)


def _dec_call(body, kv, in_specs, args, out_specs, out_shape, name):
    if not isinstance(kv, tuple):
        db, hh, rows, _ = kv.shape
        return pl.pallas_call(
            body, grid=(db,),
            in_specs=[pl.BlockSpec((1, hh, rows, LANES), lambda b: (b, 0, 0, 0))] + in_specs,
            out_specs=out_specs, out_shape=out_shape, compiler_params=_cparams("parallel"), name=name,
        )(kv, *args)
    pool_t, layer, page_table = kv
    db, npg = page_table.shape
    _, _, two, hh, dd, ps = pool_t.shape

    def paged_body(pt_ref, *refs):
        kv_scr = refs[-1]
        for p in range(npg):
            for h in range(hh):
                kv_scr[0, h, p * ps:(p + 1) * ps, :] = _kv_rows(refs[p], h).astype(BF16)
        body(kv_scr, *refs[npg:-1])

    def with_pt(spec):
        return pl.BlockSpec(spec.block_shape, lambda b, pt, f=spec.index_map: f(b))

    def page_map(p):
        return lambda b, pt: (layer, pt[b * npg + p], 0, 0, 0, 0)

    grid_spec = pltpu.PrefetchScalarGridSpec(
        num_scalar_prefetch=1,
        grid=(db,),
        in_specs=[pl.BlockSpec((None, None, two, hh, dd, ps), page_map(p)) for p in range(npg)] + [with_pt(s) for s in in_specs],
        out_specs=[with_pt(s) for s in out_specs],
        scratch_shapes=[pltpu.VMEM((1, hh, npg * ps, LANES), BF16)],
    )
    return pl.pallas_call(
        paged_body, grid_spec=grid_spec, out_shape=out_shape, compiler_params=_cparams("parallel"), name=name + "_paged",
    )(page_table.reshape(-1), *([pool_t] * npg), *args)


def _dec_scores(q, kv_ref, new_ref, h):
    kvb = kv_ref[0, h].astype(BF16)
    s = _dot_nt(q, kvb)
    if new_ref is None:
        return s, kvb, None
    nb = new_ref[0, h].astype(BF16)
    return jnp.concatenate([s, _dot_nt(q, nb)], axis=-1), kvb, nb


def _dec_softmax_out(s, kvb, nb):
    m = jnp.max(s, axis=-1, keepdims=True)
    e = jnp.where(s > 0.5 * NEG_INF, jnp.exp(s - m), 0.0)
    p = e / jnp.maximum(jnp.sum(e, axis=-1, keepdims=True), 1e-30)
    pb = p.astype(BF16)
    nm = kvb.shape[0]
    o = _dot(pb[:, :nm], kvb)
    if nb is not None:
        o = o + _dot(pb[:, nm:], nb)
    return p, o


def _dec_attn_kernel(*refs, g, has_new, aug, gated, imp):
    it = iter(refs)
    kv_ref, q_ref = next(it), next(it)
    new_ref = next(it) if has_new else None
    b_ref = next(it)
    qa_ref, oh_ref = (next(it), next(it)) if aug else (None, None)
    acc_ref, gate_ref = (next(it), next(it)) if gated else (None, None)
    c2s_ref = next(it) if imp else None
    o_ref = next(it)
    imp_ref = next(it) if imp else None
    for h in range(q_ref.shape[1]):
        s, kvb, nb = _dec_scores(q_ref[0, h], kv_ref, new_ref, h)
        s = s + b_ref[h]
        if aug:
            s = s + _dot(qa_ref[0, h], oh_ref[...])
        p, o = _dec_softmax_out(s, kvb, nb)
        if gated:
            o = acc_ref[0, h] + gate_ref[0, h] * o
        o_ref[0, h] = o
        if imp:
            row = lax.broadcasted_iota(jnp.int32, p.shape, 0)
            psum = jnp.sum(jnp.where(row < g, p, 0.0), axis=0, keepdims=True)
            ph, plo = _split2(jnp.broadcast_to(psum, p.shape))
            imp_ref[0, h] = _dot_nt(ph, c2s_ref[...]) + _dot_nt(plo, c2s_ref[...])


def _dec_attn(q, kv, new, bias, g, qa=None, onehot_t=None, acc=None, gate=None, c2s_t=None):
    db, n_kv, _, _ = q.shape
    tp = bias.shape[2]
    has_new, aug, gated, imp = new is not None, qa is not None, acc is not None, c2s_t is not None
    row4 = lambda w: pl.BlockSpec((1, n_kv, DEC_ROWS, w), lambda b: (b, 0, 0, 0))
    in_specs = [row4(LANES)]
    args = [q]
    if has_new:
        in_specs.append(pl.BlockSpec((1, n_kv, LANES, LANES), lambda b: (b, 0, 0, 0)))
        args.append(new)
    in_specs.append(pl.BlockSpec((n_kv, DEC_ROWS, tp), lambda b: (0, 0, 0)))
    args.append(bias)
    if aug:
        in_specs += [row4(NBLK_PAD), pl.BlockSpec((NBLK_PAD, tp), lambda b: (0, 0))]
        args += [qa, onehot_t]
    if gated:
        in_specs += [row4(LANES), row4(1)]
        args += [acc, gate]
    if imp:
        in_specs += [pl.BlockSpec((NBLK_PAD, tp), lambda b: (0, 0))]
        args += [c2s_t]
    out_specs = [row4(LANES)]
    out_shape = [jax.ShapeDtypeStruct((db, n_kv, DEC_ROWS, LANES), F32)]
    if imp:
        out_specs.append(row4(NBLK_PAD))
        out_shape.append(jax.ShapeDtypeStruct((db, n_kv, DEC_ROWS, NBLK_PAD), F32))
    res = _dec_call(functools.partial(_dec_attn_kernel, g=g, has_new=has_new, aug=aug, gated=gated, imp=imp),
                    kv, in_specs, args, out_specs, out_shape, "dec_attn")
    return res if imp else res[0]


def _dec_moba_kernel(kv_ref, q_ref, new_ref, b_ref, oh_ref, o_ref, *, own):
    lane = lax.broadcasted_iota(jnp.int32, (DEC_ROWS, NBLK_PAD), 1)
    for h in range(q_ref.shape[1]):
        s, kvb, nb = _dec_scores(q_ref[0, h], kv_ref, new_ref, h)
        gate = jnp.full((DEC_ROWS, NBLK_PAD), -SEL_BIG, F32)
        for n in range(own):
            gn = jnp.sum(s[:, n * MOBA_BLOCK:(n + 1) * MOBA_BLOCK], axis=-1, keepdims=True) * (1.0 / (SCALE * MOBA_BLOCK))
            gate = jnp.where(lane == n, gn, gate)
        score = jnp.where(lane == own, SEL_BIG, gate)
        cnt = jnp.zeros(score.shape, F32)
        for n in range(own + 1):
            col = score[:, n:n + 1]
            cnt = cnt + jnp.where((col > score) | ((col == score) & (n < lane)), 1.0, 0.0)
        sel = (cnt < MOBA_TOPK + 1) & (score > -SEL_BIG / 2)
        nsel = jnp.where(sel, 0.0, 1.0).astype(BF16)
        s = s + b_ref[h] + _dot(nsel, oh_ref[...])
        _, o = _dec_softmax_out(s, kvb, nb)
        o_ref[0, h] = o


def _dec_moba(q, kv, new, bias, onehot_t, past):
    db, n_kv, _, _ = q.shape
    tp = bias.shape[2]
    row4 = pl.BlockSpec((1, n_kv, DEC_ROWS, LANES), lambda b: (b, 0, 0, 0))
    in_specs = [row4,
                pl.BlockSpec((1, n_kv, LANES, LANES), lambda b: (b, 0, 0, 0)),
                pl.BlockSpec((n_kv, DEC_ROWS, tp), lambda b: (0, 0, 0)),
                pl.BlockSpec((NBLK_PAD, tp), lambda b: (0, 0))]
    return _dec_call(functools.partial(_dec_moba_kernel, own=past // MOBA_BLOCK), kv, in_specs,
                     [q, new, bias, onehot_t], [row4], [jax.ShapeDtypeStruct((db, n_kv, DEC_ROWS, LANES), F32)], "dec_moba")[0]


def _dec_sb_kernel(kv_ref, q_ref, u_ref, o_ref):
    u = u_ref[...]
    nchunk = kv_ref.shape[2] // TK
    for h in range(q_ref.shape[1]):
        kvb = kv_ref[0, h].astype(BF16)
        z = _dot_nt(q_ref[0, h], kvb)
        sp = jnp.maximum(z, 0.0) + jnp.log1p(jnp.exp(-jnp.abs(z)))
        lk = -sp
        hi, lo = _split2(lk)
        cols = [slice(c * TK, (c + 1) * TK) for c in range(nchunk)]
        tot = [jnp.sum(lk[:, sl], axis=-1, keepdims=True) for sl in cols]
        between = []
        carry = jnp.zeros((DEC_ROWS, 1), F32)
        for c in reversed(range(nchunk)):
            between.append(_dot(hi[:, cols[c]], u) + _dot(lo[:, cols[c]], u) + carry)
            carry = carry + tot[c]
        a = jnp.exp(z - sp + jnp.concatenate(between[::-1], axis=-1))
        o_ref[0, h] = _dot(a.astype(BF16), kvb)


def _dec_sb(q, kv):
    db, n_kv, _, _ = q.shape
    idx = np.arange(TK)
    u = jnp.asarray(idx[:, None] > idx[None, :], BF16)
    row4 = pl.BlockSpec((1, n_kv, DEC_ROWS, LANES), lambda b: (b, 0, 0, 0))
    return _dec_call(_dec_sb_kernel, kv, [row4, pl.BlockSpec((TK, TK), lambda b: (0, 0))], [q, u], [row4],
                     [jax.ShapeDtypeStruct((db, n_kv, DEC_ROWS, LANES), F32)], "dec_sb")[0]


def _rank_kernel(s_ref, o_ref, sc_scr, *, qpos, block, kth):
    n = s_ref.shape[2]
    blk = lax.broadcasted_iota(jnp.int32, (NBLK_PAD, n), 0)
    own = qpos // block
    past_score = jnp.where(blk * block <= qpos, s_ref[0], -SEL_BIG)
    score = jnp.where((blk == own) | (blk == 0), SEL_BIG, past_score)
    cols = _rank_nsel(score, sc_scr, min(own + 1, NBLK_PAD), kth, -jnp.inf)
    for c, nsel_t in enumerate(cols):
        o_ref[0, c * LANES:(c + 1) * LANES, :] = nsel_t.T.astype(BF16)


def _rank_select(score_t, qpos, block, kth):
    x, _, n = score_t.shape
    return pl.pallas_call(
        functools.partial(_rank_kernel, qpos=qpos, block=block, kth=kth),
        grid=(x,),
        in_specs=[pl.BlockSpec((1, NBLK_PAD, n), lambda i: (i, 0, 0))],
        out_specs=pl.BlockSpec((1, n, NBLK_PAD), lambda i: (i, 0, 0)),
        out_shape=jax.ShapeDtypeStruct((x, n, NBLK_PAD), BF16),
        scratch_shapes=[pltpu.VMEM((NBLK_PAD, n), F32)],
        compiler_params=_cparams("parallel"),
        name="rank_select",
    )(score_t)


def _dec_q(x, db, n_kv, g):
    x = x.reshape(db, n_kv, g, HEAD_DIM)
    return jnp.pad(x, ((0, 0), (0, 0), (0, DEC_ROWS - g), (0, LANES - HEAD_DIM)))


def _dec_new(x, db, n_heads):
    return jnp.pad(x.reshape(db, n_heads, 1, LANES), ((0, 0), (0, 0), (0, LANES - 1), (0, 0)))


def _dec_bias(tb, n_kv, g, qpos, kpos, win=None):
    t = _bias_table(tb, n_kv * g, np.full(DEC_ROWS, qpos), kpos, win)[:, 0]
    return jnp.pad(t.reshape(n_kv, g, -1), ((0, 0), (0, DEC_ROWS - g), (0, 0)))


def _dec_out(o, db, n_kv, g):
    return o[:, :, :g, HEAD_DIM:].reshape(db, n_kv * g * HEAD_DIM)


def _dec_state(x_t, db, n_heads):
    return x_t.reshape(2, n_heads, HEAD_DIM, db).transpose(3, 0, 1, 2).reshape(db, 1, 2, n_heads, HEAD_DIM)


def _pool_t(pool):
    return pool.transpose(0, 1, 3, 4, 5, 2)


def _even_sample(h, P, j, li, c_cmp, c_slc, c_win, c_sb, page_table):
    db = h.shape[0]
    w_in = P['w_in_even'][j]
    q_gain, k_gain = P['nsa_q_norm'][j], P['nsa_k_norm'][j]
    cut = lambda k: w_in[:, _E_COLS[k][0]:_E_COLS[k][1]]
    hkv, g = NSA_KV_HEADS, G_NSA
    kvw = cut('kv')
    w_row = jnp.concatenate([cut('q'), cut('qs'), _gate_cols(cut('g')),
                             _pair_cols(kvw[:, 256:384], kvw[:, 384:512], hkv),
                             _pair_cols(kvw[:, 512:640], kvw[:, 640:768], hkv),
                             _pair_cols(cut('sb')[:, :256], cut('sb')[:, 256:], SB_KV_HEADS)], axis=1).astype(BF16)
    one = jnp.ones((HEAD_DIM,), F32)
    cs = jnp.concatenate([jnp.tile(q_gain, NSA_HEADS) * SCALE, jnp.full((512,), SCALE, F32), jnp.ones((256,), F32),
                          jnp.tile(jnp.concatenate([k_gain[1], one]), hkv), jnp.tile(jnp.concatenate([k_gain[2], one]), hkv),
                          jnp.ones((512,), F32)])
    row_modes = ('norm',) * 4 + ('scale',) * 4 + ('sigmoid',) * 2 + ('normk',) * 4 + ('scale',) * 4
    w_t, gt = _even_t_weights(w_in, k_gain)
    q, qs, gates, slc_new, win_new, sb_new, cmp_t, slc_t, win_t, sb_t = _proj(
        h, 1, db, P['norm_mix'][li], w_row, cs, row_modes, ((0, 4), (4, 8), (8, 10), (10, 12), (12, 14), (14, 18)),
        (BF16, BF16, F32, F32, F32, F32), w_t, gt, _E_T_MODES, _E_T_OUTS)
    tb = P['rel_bias_table']
    qp = _dec_q(q, db, hkv, g)
    gcol = lambda br: jnp.pad(gates.reshape(db, hkv, LANES)[:, :, br * g:(br + 1) * g], ((0, 0), (0, 0), (0, DEC_ROWS - g)))[..., None]
    past = page_table.shape[1] * PAGE_SIZE
    tp = past + LANES
    kpos = np.where(np.arange(tp) <= past, np.arange(tp), -1)

    ncp = past // CMP_STRIDE
    nc = (past + 1 - CMP_LEN) // CMP_STRIDE + 1
    kcvc = _dec_cmp_mlp(_gather_kv(_pool_t(c_cmp), j, page_table), P['nsa_cmp_w1'][j], P['nsa_cmp_pos'][j],
                        P['nsa_cmp_w2'][j], k_gain[0])
    kpos_c = np.where(np.arange(ncp) < nc, CMP_STRIDE * np.arange(ncp) + CMP_LEN - 1, -1)
    nsb = -(-(past + 1) // SEL_BLOCK)
    o, imp = _dec_attn(qp, kcvc, None, _dec_bias(tb, hkv, g, past, kpos_c), g,
                       acc=jnp.zeros((db, hkv, DEC_ROWS, LANES), F32), gate=gcol(0), c2s_t=_cmp_to_sel_t(ncp, nc, nsb))
    nsel = _rank_select(imp[:, :, 0].transpose(1, 2, 0), past, SEL_BLOCK, N_SEL)
    qa = jnp.broadcast_to(nsel.transpose(1, 0, 2)[:, :, None], (db, hkv, DEC_ROWS, NBLK_PAD))

    o = _dec_attn(qp, (_pool_t(c_slc), j, page_table), _dec_new(slc_new, db, hkv),
                  _dec_bias(tb, hkv, g, past, kpos), g, qa=qa, onehot_t=_block_onehot_neg_t(tp, SEL_BLOCK),
                  acc=o, gate=gcol(1))

    cw_t = _pool_t(c_win)
    wb = c_win.shape[2]
    kpos_w = np.where(np.arange(wb + LANES) <= wb, past - wb + np.arange(wb + LANES), -1)
    o = _dec_attn(qp, _win_rows(cw_t, j), _dec_new(win_new, db, hkv),
                  _dec_bias(tb, hkv, g, past, kpos_w, win=WINDOW), g, acc=o, gate=gcol(2))

    o_b = _dec_sb(_dec_q(qs, db, SB_KV_HEADS, G_SB), (_pool_t(c_sb), j, page_table))
    h = _out_proj([_dec_out(o, db, hkv, g), _dec_out(o_b, db, SB_KV_HEADS, G_SB)], P['w_out_even'][j].astype(BF16), h)

    win_state = _win_shift(cw_t, j, win_t[0].T[:, :, None]).transpose(0, 4, 1, 2, 3)
    return h, (_dec_state(cmp_t, db, hkv), _dec_state(slc_t, db, hkv), win_state, _dec_state(sb_t, db, SB_KV_HEADS))


def _odd_sample(h, P, j, li, c_moba, page_table):
    db = h.shape[0]
    hkv, g = MOBA_KV_HEADS, G_MOBA
    w_in = P['w_in_odd'][j]
    w_row = jnp.concatenate([w_in[:, :1024], _pair_cols(w_in[:, 1024:1280], w_in[:, 1280:], hkv)], axis=1).astype(BF16)
    one = jnp.ones((HEAD_DIM,), F32)
    cs = jnp.concatenate([jnp.tile(P['moba_q_norm'][j], MOBA_HEADS) * SCALE, jnp.tile(jnp.concatenate([P['moba_k_norm'][j], one]), hkv)])
    w_t = w_in[:, 1024:].T.astype(BF16)
    gt = jnp.concatenate([jnp.tile(P['moba_k_norm'][j], hkv), jnp.ones((hkv * HEAD_DIM,), F32)])
    q, kv_new, kv_t = _proj(h, 1, db, P['norm_mix'][li], w_row, cs, ('norm',) * 8 + ('normk',) * 4, ((0, 8), (8, 12)),
                            (BF16, F32), w_t, gt, ('norm',) * 4 + ('raw',) * 4, ((0, 8),))
    past = page_table.shape[1] * PAGE_SIZE
    tp = past + LANES
    kpos = np.where(np.arange(tp) <= past, np.arange(tp), -1)
    o = _dec_moba(_dec_q(q, db, hkv, g), (_pool_t(c_moba), j, page_table), _dec_new(kv_new, db, hkv),
                  _dec_bias(P['rel_bias_table'], hkv, g, past, kpos), _block_onehot_neg_t(tp, MOBA_BLOCK), past)
    h = _out_proj([_dec_out(o, db, hkv, g)], P['w_out_odd'][j].astype(BF16), h)
    return h, (_dec_state(kv_t, db, hkv),)


def _trunk_sample(x, p, P, caches, page_table):
    db, t, d = x.shape
    h = x.reshape(db * t, d)
    c_cmp, c_slc, c_win, c_sb, c_moba = caches
    even_states, odd_states = [], []
    for li in range(p.shape[0]):
        j = li // 2
        if li % 2 == 0:
            h, st = _even_sample(h, P, j, li, c_cmp, c_slc, c_win, c_sb, page_table)
            even_states.append(st)
        else:
            h, st = _odd_sample(h, P, j, li, c_moba, page_table)
            odd_states.append(st)
        h = _ffn_and_pe(h, p[li].reshape(db * t, -1), P, li)
    return h.reshape(db, t, d), even_states, odd_states


def kernel(x_prompt, x_sample, cache_nsa_cmp, cache_nsa_slc, cache_nsa_win, cache_sb, cache_moba, page_table,
           p_prompt, p_sample, rel_bias_table, norm_mix, norm_ffn, norm_pe, w_pe_in, w_pe_gate,
           w_in_even, w_out_even, nsa_q_norm, nsa_k_norm, nsa_cmp_pos, nsa_cmp_w1, nsa_cmp_w2,
           w_ffn_gu, w_ffn_down, w_in_odd, w_out_odd, moba_q_norm, moba_k_norm,
           w_router, b_router, w_moe_gu, w_moe_down):
    P = {'rel_bias_table': rel_bias_table, 'norm_mix': norm_mix, 'norm_ffn': norm_ffn, 'norm_pe': norm_pe,
         'w_pe_in': w_pe_in, 'w_pe_gate': w_pe_gate, 'w_in_even': w_in_even, 'w_out_even': w_out_even,
         'nsa_q_norm': nsa_q_norm, 'nsa_k_norm': nsa_k_norm, 'nsa_cmp_pos': nsa_cmp_pos, 'nsa_cmp_w1': nsa_cmp_w1,
         'nsa_cmp_w2': nsa_cmp_w2, 'w_ffn_gu': w_ffn_gu, 'w_ffn_down': w_ffn_down, 'w_in_odd': w_in_odd,
         'w_out_odd': w_out_odd, 'moba_q_norm': moba_q_norm, 'moba_k_norm': moba_k_norm, 'w_router': w_router,
         'b_router': b_router, 'w_moe_gu': w_moe_gu, 'w_moe_down': w_moe_down}
    y_prompt, pe, po = _trunk_prompt(x_prompt, p_prompt, P)
    caches = (cache_nsa_cmp, cache_nsa_slc, cache_nsa_win, cache_sb, cache_moba)
    y_sample, se, so = _trunk_sample(x_sample, p_sample, P, caches, page_table)
    stack = lambda states, k: jnp.stack([s[k] for s in states])
    return (y_prompt, y_sample,
            stack(pe, 0), stack(se, 0), stack(pe, 1), stack(se, 1),
            stack(pe, 2), stack(se, 2), stack(pe, 3), stack(se, 3),
            stack(po, 0), stack(so, 0))
```

```python
import functools
import math

import numpy as np
import jax
import jax.numpy as jnp
from jax import lax
from jax.experimental import pallas as pl
from jax.experimental.pallas import tpu as pltpu

F32 = jnp.float32
BF16 = jnp.bfloat16

HEAD_DIM = 64
NSA_HEADS = 8
NSA_KV_HEADS = 2
G_NSA = 4
SB_HEADS = 8
SB_KV_HEADS = 4
G_SB = 2
MOBA_HEADS = 16
MOBA_KV_HEADS = 4
G_MOBA = 4
CMP_LEN = 32
CMP_STRIDE = 16
SEL_BLOCK = 64
N_SEL = 16
WINDOW = 512
MOBA_BLOCK = 256
MOBA_TOPK = 3
N_BUCKETS = 32
REL_MAX_DIST = 1024
N_EXPERTS = 8
PAGE_SIZE = 128
SCALE = HEAD_DIM ** -0.5
NEG_INF = -1e30
SEL_BIG = 1e9
EPS = 1e-6

LANES = 128
SUBLANES = 8
NBLK_PAD = LANES
TQ = 256
TK = 256
FLASH_HEADS = 4
SB_TQ = 256
SB_CUT = 110.0
DEC_ROWS = 16
VMEM_LIMIT = 56 * 1024 * 1024


def _bucket_thresholds():
    n = np.arange(0, 2 * REL_MAX_DIST, dtype=np.float64)
    exact = N_BUCKETS // 2
    nf = np.maximum(n, 1.0)
    big = exact + (np.log(nf / exact) / math.log(REL_MAX_DIST / exact) * (N_BUCKETS - exact)).astype(np.int32)
    bucket = np.where(n < exact, n.astype(np.int32), np.minimum(big, N_BUCKETS - 1))
    return tuple(int(np.argmax(bucket >= k)) for k in range(1, N_BUCKETS))


BUCKET_THR = _bucket_thresholds()
SAT_DIST = BUCKET_THR[-1]


def _cparams(*sem):
    return pltpu.CompilerParams(dimension_semantics=sem, vmem_limit_bytes=VMEM_LIMIT)


def _row_tile(m, cap):
    t = cap
    while m % t:
        t //= 2
    return t


def _split2(x):
    hi = x.astype(BF16)
    lo = (x - hi.astype(F32)).astype(BF16)
    return hi, lo


def _split3(x):
    hi = x.astype(BF16)
    r = x - hi.astype(F32)
    mid = r.astype(BF16)
    lo = (r - mid.astype(F32)).astype(BF16)
    return hi, mid, lo


def _dot(a, b):
    return jnp.dot(a, b, preferred_element_type=F32)


def _dot_nt(a, b):
    return lax.dot_general(a, b, (((1,), (1,)), ((), ())), preferred_element_type=F32)


def _rms(x, g):
    return x * lax.rsqrt(jnp.mean(x * x, axis=-1, keepdims=True) + EPS) * g


def _bias_from_dist(dist, tb_ref, h):
    val = jnp.full(dist.shape, tb_ref[0, h], F32)
    for k in range(1, N_BUCKETS):
        val = jnp.where(dist >= BUCKET_THR[k - 1], tb_ref[k, h], val)
    return val


def _left_half():
    return lax.broadcasted_iota(jnp.int32, (1, LANES), 1) < HEAD_DIM


def _half_rms_inv(x, both):
    left = _left_half()
    sq = x * x
    s_left = jnp.sum(jnp.where(left, sq, 0.0), axis=-1, keepdims=True)
    inv_l = lax.rsqrt(s_left * (1.0 / HEAD_DIM) + EPS)
    if both:
        s_all = jnp.sum(sq, axis=-1, keepdims=True)
        inv_r = lax.rsqrt((s_all - s_left) * (1.0 / HEAD_DIM) + EPS)
    else:
        inv_r = 1.0
    return jnp.where(left, inv_l, inv_r)


def _proj_kernel(h_ref, g_ref, wr_ref, cs_ref, wt_ref, gt_ref, *outs, row_modes, row_outs, t_modes, t_outs):
    xn = _rms(h_ref[...], g_ref[...]).astype(BF16)
    y = _dot(xn, wr_ref[...])
    for (c0, c1), o_ref in zip(row_outs, outs[:len(row_outs)]):
        for c in range(c0, c1):
            sl = slice(c * LANES, (c + 1) * LANES)
            x = y[:, sl]
            mode = row_modes[c]
            if mode in ('norm', 'normk'):
                x = x * _half_rms_inv(x, mode == 'norm')
            x = x * cs_ref[:, sl]
            if mode == 'sigmoid':
                x = jax.nn.sigmoid(x)
            o_ref[:, (c - c0) * LANES:(c - c0 + 1) * LANES] = x.astype(o_ref.dtype)
    yt = _dot_nt(wt_ref[...], xn)
    for (r0, r1), o_ref in zip(t_outs, outs[len(row_outs):]):
        for r in range(r0, r1):
            sl = slice(r * HEAD_DIM, (r + 1) * HEAD_DIM)
            x = yt[sl, :]
            if t_modes[r] == 'norm':
                x = x * lax.rsqrt(jnp.mean(x * x, axis=0, keepdims=True) + EPS) * gt_ref[sl, :]
            o_ref[0, (r - r0) * HEAD_DIM:(r - r0 + 1) * HEAD_DIM, :] = x


def _proj(h, b, t, g, w_row, cs, row_modes, row_outs, row_dtypes, w_t, gt, t_modes, t_outs):
    n, d = h.shape
    tm = min(512, t)
    nt = t // tm
    cr, ct = w_row.shape[1], w_t.shape[0]
    row = lambda bi, i: (bi * nt + i, 0)
    const = lambda bi, i: (0, 0)
    out_specs = [pl.BlockSpec((tm, (c1 - c0) * LANES), row) for c0, c1 in row_outs]
    out_shape = [jax.ShapeDtypeStruct((n, (c1 - c0) * LANES), dt) for (c0, c1), dt in zip(row_outs, row_dtypes)]
    out_specs += [pl.BlockSpec((1, (r1 - r0) * HEAD_DIM, tm), lambda bi, i: (bi, 0, i)) for r0, r1 in t_outs]
    out_shape += [jax.ShapeDtypeStruct((b, (r1 - r0) * HEAD_DIM, t), F32) for r0, r1 in t_outs]
    return pl.pallas_call(
        functools.partial(_proj_kernel, row_modes=row_modes, row_outs=row_outs, t_modes=t_modes, t_outs=t_outs),
        grid=(b, nt),
        in_specs=[pl.BlockSpec((tm, d), row),
                  pl.BlockSpec((1, d), const),
                  pl.BlockSpec((d, cr), const),
                  pl.BlockSpec((1, cr), const),
                  pl.BlockSpec((ct, d), const),
                  pl.BlockSpec((ct, 1), const)],
        out_specs=out_specs,
        out_shape=out_shape,
        compiler_params=_cparams("parallel", "parallel"),
        name="proj",
    )(h, g.reshape(1, d), w_row, cs.reshape(1, cr), w_t, gt.reshape(ct, 1))


def _out_proj_kernel(*refs):
    *x_refs, w_ref, r_ref, o_ref = refs
    acc = r_ref[...]
    k0 = 0
    for x_ref in x_refs:
        k = x_ref.shape[1]
        acc = acc + _dot(x_ref[...].astype(BF16), w_ref[k0:k0 + k, :])
        k0 += k
    o_ref[...] = acc


def _out_proj(xs, w, res):
    m, n = res.shape
    tm = _row_tile(m, 512)
    return pl.pallas_call(
        _out_proj_kernel,
        grid=(m // tm,),
        in_specs=[pl.BlockSpec((tm, x.shape[1]), lambda i: (i, 0)) for x in xs]
        + [pl.BlockSpec(w.shape, lambda i: (0, 0)), pl.BlockSpec((tm, n), lambda i: (i, 0))],
        out_specs=pl.BlockSpec((tm, n), lambda i: (i, 0)),
        out_shape=jax.ShapeDtypeStruct((m, n), F32),
        compiler_params=_cparams("parallel"),
        name="out_proj",
    )(*xs, w, res)


def _ffn_kernel(h_ref, g_ref, wg_ref, wu_ref, wd_ref, o_ref, xn_scr):
    f = pl.program_id(1)

    @pl.when(f == 0)
    def _():
        xn_scr[...] = _rms(h_ref[...], g_ref[...]).astype(BF16)
        o_ref[...] = h_ref[...]

    xn = xn_scr[...]
    gate = _dot(xn, wg_ref[...])
    up = _dot(xn, wu_ref[...])
    act = (gate * jax.nn.sigmoid(gate) * up).astype(BF16)
    o_ref[...] += _dot(act, wd_ref[...])


def _ffn(h, g, w_gu, w_down, tf):
    m, d = h.shape
    fdim = w_down.shape[0]
    nf = fdim // tf
    tm = _row_tile(m, 1024)
    return pl.pallas_call(
        _ffn_kernel,
        grid=(m // tm, nf),
        in_specs=[pl.BlockSpec((tm, d), lambda i, f: (i, 0)),
                  pl.BlockSpec((1, d), lambda i, f: (0, 0)),
                  pl.BlockSpec((d, tf), lambda i, f: (0, f)),
                  pl.BlockSpec((d, tf), lambda i, f: (0, f + nf)),
                  pl.BlockSpec((tf, d), lambda i, f: (f, 0))],
        out_specs=pl.BlockSpec((tm, d), lambda i, f: (i, 0)),
        out_shape=jax.ShapeDtypeStruct((m, d), F32),
        scratch_shapes=[pltpu.VMEM((tm, d), BF16)],
        compiler_params=_cparams("parallel", "arbitrary"),
        name="ffn",
    )(h, g.reshape(1, d), w_gu, w_gu, w_down)


def _router_kernel(h_ref, g_ref, w_ref, b_ref, o_ref, ot_ref):
    xn = _rms(h_ref[...], g_ref[...])
    xh, xl = _split2(xn)
    wh, wl = _split2(w_ref[...])
    logits = _dot(xh, wh) + _dot(xh, wl) + _dot(xl, wh) + b_ref[...]
    lane = lax.broadcasted_iota(jnp.int32, logits.shape, 1)
    logits = jnp.where(lane < N_EXPERTS, logits, -jnp.inf)
    v1 = jnp.max(logits, axis=-1, keepdims=True)
    i1 = jnp.min(jnp.where(logits == v1, lane, LANES), axis=-1, keepdims=True)
    rest = jnp.where(lane == i1, -jnp.inf, logits)
    v2 = jnp.max(rest, axis=-1, keepdims=True)
    i2 = jnp.min(jnp.where(rest == v2, lane, LANES), axis=-1, keepdims=True)
    e2 = jnp.exp(v2 - v1)
    w1 = 1.0 / (1.0 + e2)
    w2 = e2 / (1.0 + e2)
    gates = jnp.where(lane == i1, w1, 0.0) + jnp.where(lane == i2, w2, 0.0)
    o_ref[...] = gates
    for c in range(gates.shape[0] // LANES):
        ot_ref[:, c * LANES:(c + 1) * LANES] = gates[c * LANES:(c + 1) * LANES, :].T


def _router(h, g, w_r, b_r):
    m, d = h.shape
    tm = _row_tile(m, 512)
    w_pad = jnp.pad(w_r, ((0, 0), (0, LANES - N_EXPERTS)))
    b_pad = jnp.pad(b_r, (0, LANES - N_EXPERTS)).reshape(1, LANES)
    return pl.pallas_call(
        _router_kernel,
        grid=(m // tm,),
        in_specs=[pl.BlockSpec((tm, d), lambda i: (i, 0)),
                  pl.BlockSpec((1, d), lambda i: (0, 0)),
                  pl.BlockSpec((d, LANES), lambda i: (0, 0)),
                  pl.BlockSpec((1, LANES), lambda i: (0, 0))],
        out_specs=[pl.BlockSpec((tm, LANES), lambda i: (i, 0)), pl.BlockSpec((LANES, tm), lambda i: (0, i))],
        out_shape=[jax.ShapeDtypeStruct((m, LANES), F32), jax.ShapeDtypeStruct((LANES, m), F32)],
        compiler_params=_cparams("parallel"),
        name="router",
    )(h, g.reshape(1, d), w_pad, b_pad)


MOE_CAP = 288


def _moe_kernel(h_ref, g_ref, gate_ref, gate_t_ref, wg_ref, wu_ref, wd_ref, o_ref,
                xn_scr, pos_scr, pos_t_scr, gate_t_scr, posc_scr, gatec_scr, xg_scr, acc_scr, cnt_scr):
    e = pl.program_id(1)
    f = pl.program_id(2)
    tm = h_ref.shape[0]
    cap = xg_scr.shape[1]

    @pl.when((e == 0) & (f == 0))
    def _():
        xn_scr[...] = _rms(h_ref[...], g_ref[...]).astype(BF16)
        o_ref[...] = h_ref[...]
        r = lax.broadcasted_iota(jnp.int32, (tm, tm), 0)
        c = lax.broadcasted_iota(jnp.int32, (tm, tm), 1)
        mask = jnp.where(gate_ref[...] > 0, 1.0, 0.0).astype(BF16)
        pos_scr[...] = _dot(jnp.where(c < r, 1.0, 0.0).astype(BF16), mask)
        mask_t = jnp.where(gate_t_ref[...] > 0, 1.0, 0.0).astype(BF16)
        pos_t = _dot(mask_t, jnp.where(r < c, 1.0, 0.0).astype(BF16))
        for ee in range(N_EXPERTS):
            pos_t_scr[ee] = pos_t[ee:ee + 1, :]
            gate_t_scr[ee] = gate_t_ref[ee:ee + 1, :]

    @pl.when(f == 0)
    def _():
        sel = lax.broadcasted_iota(jnp.int32, (tm, LANES), 1) == e
        posc_scr[...] = jnp.sum(jnp.where(sel, pos_scr[...], 0.0), axis=-1, keepdims=True)
        gatec_scr[...] = jnp.sum(jnp.where(sel, gate_ref[...], 0.0), axis=-1, keepdims=True)
        pos_r = pos_t_scr[e]
        routed = gate_t_scr[e] > 0
        cnt = jnp.sum(jnp.where(routed, 1.0, 0.0)).astype(jnp.int32)
        cnt_scr[0] = cnt

        def gather(ci, carry):
            slot = (ci * cap + lax.broadcasted_iota(jnp.int32, (cap, tm), 0)).astype(F32)
            gmat = jnp.where((pos_r == slot) & routed, 1.0, 0.0).astype(BF16)
            xg_scr[ci] = _dot(gmat, xn_scr[...]).astype(BF16)
            acc_scr[ci] = jnp.zeros(acc_scr.shape[1:], F32)
            return carry

        lax.fori_loop(0, (cnt + cap - 1) // cap, gather, 0)

    nchunk = (cnt_scr[0] + cap - 1) // cap

    def expert(ci, carry):
        x = xg_scr[ci]
        gate = _dot(x, wg_ref[0])
        up = _dot(x, wu_ref[0])
        act = (gate * jax.nn.sigmoid(gate) * up).astype(BF16)
        acc_scr[ci] += _dot(act, wd_ref[0])
        return carry

    lax.fori_loop(0, nchunk, expert, 0)

    @pl.when(f == pl.num_programs(2) - 1)
    def _():
        posc = posc_scr[...]
        gatec = gatec_scr[...]

        def scatter(ci, carry):
            slot = (ci * cap + lax.broadcasted_iota(jnp.int32, (tm, cap), 1)).astype(F32)
            smat = jnp.where((posc == slot) & (gatec > 0), 1.0, 0.0).astype(BF16)
            yh, yl = _split2(acc_scr[ci])
            o_ref[...] += gatec * (_dot(smat, yh) + _dot(smat, yl))
            return carry

        lax.fori_loop(0, nchunk, scatter, 0)


def _moe(h, g, gates, gates_t, w_gu, w_down, tf):
    m, d = h.shape
    ne, fdim, _ = w_down.shape
    nf = fdim // tf
    tm = _row_tile(m, 1024)
    cap = min(MOE_CAP, tm)
    nch = -(-tm // cap)
    return pl.pallas_call(
        _moe_kernel,
        grid=(m // tm, ne, nf),
        in_specs=[pl.BlockSpec((tm, d), lambda i, e, f: (i, 0)),
                  pl.BlockSpec((1, d), lambda i, e, f: (0, 0)),
                  pl.BlockSpec((tm, LANES), lambda i, e, f: (i, 0)),
                  pl.BlockSpec((LANES, tm), lambda i, e, f: (0, i)),
                  pl.BlockSpec((1, d, tf), lambda i, e, f: (e, 0, f)),
                  pl.BlockSpec((1, d, tf), lambda i, e, f: (e, 0, f + nf)),
                  pl.BlockSpec((1, tf, d), lambda i, e, f: (e, f, 0))],
        out_specs=pl.BlockSpec((tm, d), lambda i, e, f: (i, 0)),
        out_shape=jax.ShapeDtypeStruct((m, d), F32),
        scratch_shapes=[pltpu.VMEM((tm, d), BF16), pltpu.VMEM((tm, LANES), F32),
                        pltpu.VMEM((N_EXPERTS, 1, tm), F32), pltpu.VMEM((N_EXPERTS, 1, tm), F32),
                        pltpu.VMEM((tm, 1), F32), pltpu.VMEM((tm, 1), F32),
                        pltpu.VMEM((nch, cap, d), BF16), pltpu.VMEM((nch, cap, d), F32),
                        pltpu.SMEM((1,), jnp.int32)],
        compiler_params=_cparams("parallel", "arbitrary", "arbitrary"),
        name="moe",
    )(h, g.reshape(1, d), gates, gates_t, w_gu, w_gu, w_down)


def _pe_kernel(h_ref, g_ref, p_ref, wg_ref, wi_ref, o_ref):
    h = h_ref[...]
    hn = _rms(h, g_ref[...]).astype(BF16)
    gate = jax.nn.sigmoid(_dot(hn, wg_ref[...]))
    o_ref[...] = h + _dot(p_ref[...].astype(BF16), wi_ref[...]) * gate


def _pe_update(h, g, p, w_gate, w_in):
    m, d = h.shape
    pd = p.shape[1]
    tm = _row_tile(m, 512)
    return pl.pallas_call(
        _pe_kernel,
        grid=(m // tm,),
        in_specs=[pl.BlockSpec((tm, d), lambda i: (i, 0)),
                  pl.BlockSpec((1, d), lambda i: (0, 0)),
                  pl.BlockSpec((tm, pd), lambda i: (i, 0)),
                  pl.BlockSpec((d, d), lambda i: (0, 0)),
                  pl.BlockSpec((pd, d), lambda i: (0, 0))],
        out_specs=pl.BlockSpec((tm, d), lambda i: (i, 0)),
        out_shape=jax.ShapeDtypeStruct((m, d), F32),
        compiler_params=_cparams("parallel"),
        name="pe_update",
    )(h, g.reshape(1, d), p, w_gate, w_in)


def _bias_kernel(tb_ref, qp_ref, kp_ref, o_ref, *, win):
    h = pl.program_id(0)
    kp = kp_ref[...]
    dist = qp_ref[...] - kp
    ok = (kp >= 0) & (dist >= 0)
    if win is not None:
        ok = ok & (dist < win)
    o_ref[0] = jnp.where(ok, _bias_from_dist(dist, tb_ref, h), NEG_INF)


def _bias_table(tb, n_heads, qpos, kpos, win=None):
    r, c = qpos.shape[0], kpos.shape[0]
    rb = _row_tile(r, 256)
    return pl.pallas_call(
        functools.partial(_bias_kernel, win=win),
        grid=(n_heads, r // rb),
        in_specs=[pl.BlockSpec(memory_space=pltpu.SMEM),
                  pl.BlockSpec((rb, 1), lambda h, i: (i, 0)),
                  pl.BlockSpec((1, c), lambda h, i: (0, 0))],
        out_specs=pl.BlockSpec((1, rb, c), lambda h, i: (h, i, 0)),
        out_shape=jax.ShapeDtypeStruct((n_heads, r, c), F32),
        compiler_params=_cparams("parallel", "parallel"),
        name="bias_table",
    )(tb, jnp.asarray(qpos, jnp.int32).reshape(r, 1), jnp.asarray(kpos, jnp.int32).reshape(1, c))


def _tile_bias(tb, n_kv, g, nd, win=None):
    t = _bias_table(tb, n_kv * g, np.arange(nd * TQ), np.arange(TK), win)
    t = t.reshape(n_kv, g, nd, TQ, TK).transpose(0, 2, 1, 3, 4)
    return jnp.concatenate([t, jnp.full((n_kv, 1, g, TQ, TK), NEG_INF, F32)], axis=1)


def _rank_nsel(score, sc_scr, nloop, kth, thresh):
    nb, n = score.shape
    sc_scr[...] = score
    out = []
    for c in range(n // LANES):
        sl = slice(c * LANES, (c + 1) * LANES)
        sc = score[:, sl]
        blk = lax.broadcasted_iota(jnp.int32, sc.shape, 0)

        def body(jg, cnt, sc=sc, blk=blk, sl=sl):
            grp = sc_scr[pl.ds(pl.multiple_of(jg * SUBLANES, SUBLANES), SUBLANES), sl]
            for r in range(SUBLANES):
                row = grp[r:r + 1, :]
                beats = (row > sc) | ((row == sc) & (jg * SUBLANES + r < blk))
                cnt = cnt + jnp.where(beats, 1.0, 0.0)
            return cnt

        cnt = lax.fori_loop(0, (nloop + SUBLANES - 1) // SUBLANES, body, jnp.zeros(sc.shape, F32))
        sel = (cnt < kth) & (sc > thresh)
        out.append(jnp.where(sel, 0.0, 1.0))
    return out


def _pos_term(pos_ref, w1_ref):
    ph, plo = _split2(pos_ref[0])
    wh, wl = _split2(w1_ref[0])
    return (_dot(ph, wh) + _dot(ph, wl) + _dot(plo, wh))[0:1]


def _cmp_mlp_kernel(x_ref, w1a_ref, w1b_ref, pos_ref, w1_ref, w2_ref, g_ref, o_ref, *, norm):
    nb, nc, _ = x_ref.shape
    x = x_ref[...].reshape(nb * nc, x_ref.shape[2])
    a = _dot(x, w1a_ref[0])
    b = _dot(x, w1b_ref[0])
    z = a + pltpu.roll(b, nb * nc - 1, 0) + _pos_term(pos_ref, w1_ref)
    y = _dot(jax.nn.gelu(z).astype(BF16), w2_ref[0])
    if norm:
        y = _rms(y, g_ref[...])
    o_ref[...] = y.reshape(nb, nc, HEAD_DIM)


def _cmp_mlp(x, w1, pos, w2, gain, kv, norm):
    s, nc, cw = x.shape
    nb = _row_tile(s, max(1, 1024 // nc))
    half = CMP_STRIDE * HEAD_DIM
    w1f = w1.reshape(2, CMP_LEN * HEAD_DIM, HEAD_DIM)
    w1a = w1f[:, :half].astype(BF16)
    w1b = w1f[:, half:].astype(BF16)
    posf = jnp.broadcast_to(pos.reshape(2, 1, CMP_LEN * HEAD_DIM), (2, SUBLANES, CMP_LEN * HEAD_DIM))
    return pl.pallas_call(
        functools.partial(_cmp_mlp_kernel, norm=norm),
        grid=(s // nb,),
        in_specs=[pl.BlockSpec((nb, nc, cw), lambda i: (i, 0, 0)),
                  pl.BlockSpec((1, half, HEAD_DIM), lambda i: (kv, 0, 0)),
                  pl.BlockSpec((1, half, HEAD_DIM), lambda i: (kv, 0, 0)),
                  pl.BlockSpec((1, SUBLANES, 2 * half), lambda i: (kv, 0, 0)),
                  pl.BlockSpec((1, 2 * half, HEAD_DIM), lambda i: (kv, 0, 0)),
                  pl.BlockSpec((1, HEAD_DIM, HEAD_DIM), lambda i: (kv, 0, 0)),
                  pl.BlockSpec((1, HEAD_DIM), lambda i: (0, 0))],
        out_specs=pl.BlockSpec((nb, nc, HEAD_DIM), lambda i: (i, 0, 0)),
        out_shape=jax.ShapeDtypeStruct((s, nc, HEAD_DIM), F32),
        compiler_params=_cparams("parallel"),
        name="cmp_mlp",
    )(x, w1a, w1b, posf, w1f, w2.astype(BF16), gain.reshape(1, HEAD_DIM))


def _stack_heads(q_ref, g):
    return jnp.concatenate([q_ref[:, gi * HEAD_DIM:(gi + 1) * HEAD_DIM] for gi in range(g)], axis=0)


def _unstack_heads(o, g, tq):
    return [o[gi * tq:(gi + 1) * tq] for gi in range(g)]


def _nsa_cmp_kernel(tb_ref, q_ref, kc_ref, vc_ref, c2s_ref, gate_ref, o_ref, nsel_ref, s_scr, sc_scr, *, n_kv, g):
    tq = q_ref.shape[0]
    ncp = kc_ref.shape[1]
    r = g * tq
    kvh = pl.program_id(0) % n_kv
    t0 = pl.program_id(1) * tq
    q = _stack_heads(q_ref, g)
    for cc in range(ncp // LANES):
        c0 = cc * LANES
        sl = slice(c0, c0 + LANES)
        min_kp = CMP_STRIDE * c0 + CMP_LEN - 1
        max_kp = CMP_STRIDE * (c0 + LANES - 1) + CMP_LEN - 1
        future = min_kp > t0 + (tq - 1)
        sat = t0 - max_kp >= SAT_DIST

        @pl.when(future)
        def _():
            s_scr[:, sl] = jnp.full((r, LANES), NEG_INF, F32)

        @pl.when(sat)
        def _():
            s = _dot_nt(q, kc_ref[0, sl, :]).reshape(g, tq, LANES)
            for gi in range(g):
                s_scr[gi * tq:(gi + 1) * tq, sl] = s[gi] + tb_ref[N_BUCKETS - 1, kvh * g + gi]

        @pl.when(jnp.logical_not(future | sat))
        def _():
            s = _dot_nt(q, kc_ref[0, sl, :]).reshape(g, tq, LANES)
            t = t0 + lax.broadcasted_iota(jnp.int32, (tq, LANES), 0)
            kp = CMP_STRIDE * (c0 + lax.broadcasted_iota(jnp.int32, (tq, LANES), 1)) + (CMP_LEN - 1)
            dist = t - kp
            for gi in range(g):
                b = _bias_from_dist(dist, tb_ref, kvh * g + gi)
                s_scr[gi * tq:(gi + 1) * tq, sl] = jnp.where(dist >= 0, s[gi] + b, NEG_INF)

    s = s_scr[...]
    m = jnp.max(s, axis=-1, keepdims=True)
    e = jnp.where(s > 0.5 * NEG_INF, jnp.exp(s - m), 0.0)
    p = e / jnp.maximum(jnp.sum(e, axis=-1, keepdims=True), 1e-30)
    o = _unstack_heads(_dot(p.astype(BF16), vc_ref[0]), g, tq)
    o_ref[...] = jnp.concatenate([o[gi] * gate_ref[:, gi:gi + 1] for gi in range(g)], axis=-1)

    psum = jnp.sum(p.reshape(g, tq, ncp), axis=0)
    ph, plo = _split2(psum)
    c2s = c2s_ref[...]
    imp_t = _dot_nt(c2s, ph) + _dot_nt(c2s, plo)
    blk = lax.broadcasted_iota(jnp.int32, (NBLK_PAD, tq), 0)
    t = t0 + lax.broadcasted_iota(jnp.int32, (NBLK_PAD, tq), 1)
    valid = SEL_BLOCK * blk <= t
    forced = (blk == t // SEL_BLOCK) | (blk == 0)
    score = jnp.where(forced, SEL_BIG, jnp.where(valid, imp_t, -SEL_BIG))
    nloop = jnp.minimum((t0 + tq - 1) // SEL_BLOCK + 1, NBLK_PAD)
    cols = _rank_nsel(score, sc_scr, nloop, N_SEL, -jnp.inf)
    for c, nsel_t in enumerate(cols):
        nsel_ref[0, c * LANES:(c + 1) * LANES, :] = nsel_t.T.astype(BF16)


def _nsa_cmp(tb, q, kc, vc, c2s_t, gates, b, t, n_kv, g):
    ncp = kc.shape[1]
    tq = min(TQ, t)
    nq = t // tq
    gw = g * HEAD_DIM
    row = lambda bg, i: ((bg // n_kv) * nq + i, bg % n_kv)
    return pl.pallas_call(
        functools.partial(_nsa_cmp_kernel, n_kv=n_kv, g=g),
        grid=(b * n_kv, nq),
        in_specs=[pl.BlockSpec(memory_space=pltpu.SMEM),
                  pl.BlockSpec((tq, gw), row),
                  pl.BlockSpec((1, ncp, HEAD_DIM), lambda bg, i: (bg, 0, 0)),
                  pl.BlockSpec((1, ncp, HEAD_DIM), lambda bg, i: (bg, 0, 0)),
                  pl.BlockSpec((NBLK_PAD, ncp), lambda bg, i: (0, 0)),
                  pl.BlockSpec((tq, LANES), row)],
        out_specs=[pl.BlockSpec((tq, gw), row),
                   pl.BlockSpec((1, tq, NBLK_PAD), lambda bg, i: (bg, i, 0))],
        out_shape=[jax.ShapeDtypeStruct((b * t, n_kv * gw), F32),
                   jax.ShapeDtypeStruct((b * n_kv, t, NBLK_PAD), BF16)],
        scratch_shapes=[pltpu.VMEM((g * tq, ncp), F32), pltpu.VMEM((NBLK_PAD, tq), F32)],
        compiler_params=_cparams("parallel", "parallel"),
        name="nsa_cmp",
    )(tb, q, kc, vc, c2s_t, gates)


def _flash_kernel(*refs, g, look, aug, shared_nsel, branch):
    it = iter(refs)
    q_ref = next(it)
    nsel_ref = next(it) if aug else None
    kt_ref, vt_ref = next(it), next(it)
    oh_ref = next(it) if aug else None
    b_ref = next(it)
    acc_ref, gate_ref = (next(it), next(it)) if branch is not None else (None, None)
    o_ref, k_scr, v_scr, q_scr, m_scr, acc_scr = it
    tq = q_ref.shape[0]
    nt = k_scr.shape[0]
    nd = b_ref.shape[1]
    r = g * tq
    i = pl.program_id(1)

    @pl.when(i == 0)
    def _():
        for j in range(nt):
            cols = slice(j * TK, (j + 1) * TK)
            if aug:
                k_scr[j, 0:NBLK_PAD, :] = oh_ref[:, cols]
                k_scr[j, NBLK_PAD:NBLK_PAD + HEAD_DIM, :] = kt_ref[0, :, cols].astype(BF16)
            else:
                k_scr[j] = kt_ref[0, :, cols].astype(BF16)
            v_scr[j, 0:HEAD_DIM, :] = vt_ref[0, :, cols].astype(BF16)
            v_scr[j, HEAD_DIM:LANES, :] = jnp.ones((LANES - HEAD_DIM, TK), BF16)

    for gi in range(g):
        rows = slice(gi * tq, (gi + 1) * tq)
        qg = q_ref[:, gi * HEAD_DIM:(gi + 1) * HEAD_DIM]
        if aug:
            q_scr[rows, 0:NBLK_PAD] = nsel_ref[0] if shared_nsel else nsel_ref[:, gi * NBLK_PAD:(gi + 1) * NBLK_PAD]
            q_scr[rows, NBLK_PAD:NBLK_PAD + HEAD_DIM] = qg
        else:
            q_scr[rows, :] = qg
    m_scr[...] = jnp.full((r, LANES), NEG_INF, F32)
    acc_scr[...] = jnp.zeros((r, LANES), F32)
    lo = 0 if look is None else jnp.maximum(i - look, 0)

    def body(jj, carry):
        ja = lo + 2 * jj
        jb = ja + 1
        jb_c = jnp.minimum(jb, nt - 1)
        da = jnp.minimum(i - ja, nd - 2)
        db = jnp.where(jb > i, nd - 1, jnp.minimum(i - jb, nd - 2))
        ka, kb, va, vb = k_scr[ja], k_scr[jb_c], v_scr[ja], v_scr[jb_c]
        for gs in range(0, g, FLASH_HEADS):
            rows = slice(gs * tq, (gs + FLASH_HEADS) * tq)
            rr = FLASH_HEADS * tq
            q = q_scr[rows, :]
            sa = (_dot(q, ka).reshape(FLASH_HEADS, tq, TK) + b_ref[0, da, gs:gs + FLASH_HEADS]).reshape(rr, TK)
            sb = (_dot(q, kb).reshape(FLASH_HEADS, tq, TK) + b_ref[0, db, gs:gs + FLASH_HEADS]).reshape(rr, TK)
            m_prev = m_scr[rows, :]
            m_cur = jnp.maximum(jnp.max(sa, axis=-1, keepdims=True), jnp.max(sb, axis=-1, keepdims=True))
            m_new = jnp.maximum(m_prev, m_cur)
            alpha = jnp.exp(m_prev - m_new)
            m_rep = jnp.concatenate([m_new] * (TK // LANES), axis=-1)
            pa = jnp.exp(sa - m_rep).astype(BF16)
            pb = jnp.exp(sb - m_rep).astype(BF16)
            acc_scr[rows, :] = alpha * acc_scr[rows, :] + _dot_nt(pa, va) + _dot_nt(pb, vb)
            m_scr[rows, :] = m_new
        return carry

    lax.fori_loop(0, (i + 2 - lo) // 2, body, 0)
    acc = acc_scr[...]
    o = _unstack_heads(acc[:, :HEAD_DIM] / jnp.maximum(acc[:, HEAD_DIM:], 1e-30), g, tq)
    if branch is not None:
        o = [acc_ref[:, gi * HEAD_DIM:(gi + 1) * HEAD_DIM] + gate_ref[:, branch * g + gi:branch * g + gi + 1] * o[gi]
             for gi in range(g)]
    o_ref[...] = jnp.concatenate(o, axis=-1)


def _flash_attn(q, kv_t, k_blk, v_blk, bias, b, t, n_kv, g, nsel=None, onehot_t=None, look=None,
                acc=None, gates=None, branch=None):
    aug = nsel is not None
    shared = aug and nsel.ndim == 3
    nq = t // TQ
    gw = g * HEAD_DIM
    dk = HEAD_DIM + (NBLK_PAD if aug else 0)
    row = lambda bg, i: ((bg // n_kv) * nq + i, bg % n_kv)
    in_specs = [pl.BlockSpec((TQ, gw), row)]
    args = [q]
    if aug:
        in_specs.append(pl.BlockSpec((1, TQ, NBLK_PAD), lambda bg, i: (bg, i, 0)) if shared
                        else pl.BlockSpec((TQ, g * NBLK_PAD), row))
        args.append(nsel)
    in_specs += [pl.BlockSpec((1, HEAD_DIM, t), lambda bg, i: (bg // n_kv, k_blk + bg % n_kv, 0)),
                 pl.BlockSpec((1, HEAD_DIM, t), lambda bg, i: (bg // n_kv, v_blk + bg % n_kv, 0))]
    args += [kv_t, kv_t]
    if aug:
        in_specs.append(pl.BlockSpec((NBLK_PAD, t), lambda bg, i: (0, 0)))
        args.append(onehot_t)
    nd = bias.shape[1]
    in_specs.append(pl.BlockSpec((1, nd, g, TQ, TK), lambda bg, i: (bg % n_kv, 0, 0, 0, 0)))
    args.append(bias)
    if branch is not None:
        in_specs += [pl.BlockSpec((TQ, gw), row), pl.BlockSpec((TQ, LANES), row)]
        args += [acc, gates]
    return pl.pallas_call(
        functools.partial(_flash_kernel, g=g, look=look, aug=aug, shared_nsel=shared, branch=branch),
        grid=(b * n_kv, nq),
        in_specs=in_specs,
        out_specs=pl.BlockSpec((TQ, gw), row),
        out_shape=jax.ShapeDtypeStruct((b * t, n_kv * gw), F32),
        scratch_shapes=[pltpu.VMEM((t // TK, dk, TK), BF16), pltpu.VMEM((t // TK, LANES, TK), BF16),
                        pltpu.VMEM((g * TQ, dk), BF16), pltpu.VMEM((g * TQ, LANES), F32),
                        pltpu.VMEM((g * TQ, LANES), F32)],
        compiler_params=_cparams("parallel", "arbitrary"),
        name="flash_attn",
    )(*args)


def _sb_kernel(q_ref, kt_ref, vt_ref, u_ref, o_ref, k_scr, v_scr, carry_scr, acc_scr, *, g):
    tq = q_ref.shape[0]
    nt = k_scr.shape[0]
    r = g * tq
    i = pl.program_id(1)

    @pl.when(i == 0)
    def _():
        for j in range(nt):
            cols = slice(j * TK, (j + 1) * TK)
            k_scr[j] = kt_ref[0, :, cols].astype(BF16)
            v_scr[j] = vt_ref[0, :, cols].astype(BF16)

    q = _stack_heads(q_ref, g)
    carry_scr[...] = jnp.zeros((r, LANES), F32)
    acc_scr[...] = jnp.zeros((r, HEAD_DIM), F32)
    t = i * tq + lax.broadcasted_iota(jnp.int32, (g, tq, TK), 1).reshape(r, TK)
    lane = lax.broadcasted_iota(jnp.int32, (r, TK), 1)
    u = u_ref[...]

    def cond(st):
        return (st[0] >= 0) & (st[1] > 0)

    def body(st):
        j = st[0]
        z = _dot(q, k_scr[j])
        before = (j * TK + lane) < t
        sp = jnp.maximum(z, 0.0) + jnp.log1p(jnp.exp(-jnp.abs(z)))
        lk = jnp.where(before, -sp, 0.0)
        hi, lo = _split2(lk)
        carry = carry_scr[...]
        between = _dot(hi, u) + _dot(lo, u) + jnp.concatenate([carry] * (TK // LANES), axis=-1)
        a = jnp.where(before, jnp.exp(z - sp + between), 0.0)
        acc_scr[...] += _dot_nt(a.astype(BF16), v_scr[j])
        carry = carry + jnp.sum(lk, axis=-1, keepdims=True)
        carry_scr[...] = carry
        return j - 1, (jnp.max(carry) > -SB_CUT).astype(jnp.int32)

    lax.while_loop(cond, body, (((i + 1) * tq - 1) // TK, jnp.int32(1)))
    o_ref[...] = jnp.concatenate(_unstack_heads(acc_scr[...], g, tq), axis=-1)


def _sb_attn(q, kv_t, b, t, n_kv, g):
    tq = min(SB_TQ, t)
    nq = t // tq
    gw = g * HEAD_DIM
    idx = np.arange(TK)
    u = jnp.asarray(idx[:, None] > idx[None, :], BF16)
    row = lambda bg, i: ((bg // n_kv) * nq + i, bg % n_kv)
    return pl.pallas_call(
        functools.partial(_sb_kernel, g=g),
        grid=(b * n_kv, nq),
        in_specs=[pl.BlockSpec((tq, gw), row),
                  pl.BlockSpec((1, HEAD_DIM, t), lambda bg, i: (bg // n_kv, bg % n_kv, 0)),
                  pl.BlockSpec((1, HEAD_DIM, t), lambda bg, i: (bg // n_kv, n_kv + bg % n_kv, 0)),
                  pl.BlockSpec((TK, TK), lambda bg, i: (0, 0))],
        out_specs=pl.BlockSpec((tq, gw), row),
        out_shape=jax.ShapeDtypeStruct((b * t, n_kv * gw), F32),
        scratch_shapes=[pltpu.VMEM((t // TK, HEAD_DIM, TK), BF16), pltpu.VMEM((t // TK, HEAD_DIM, TK), BF16),
                        pltpu.VMEM((g * tq, LANES), F32), pltpu.VMEM((g * tq, HEAD_DIM), F32)],
        compiler_params=_cparams("parallel", "arbitrary"),
        name="sb_attn",
    )(q, kv_t, kv_t, u)


def _moba_gate_kernel(q_ref, kt_ref, a_ref, nsel_ref, km_scr, sc_scr, *, g):
    tq = q_ref.shape[0]
    i = pl.program_id(1)

    @pl.when(i == 0)
    def _():
        a = a_ref[...]
        h1, h2, h3 = _split3(kt_ref[0])
        km_t = _dot(h1, a) + _dot(h2, a) + _dot(h3, a)
        km_sq = jnp.concatenate([km_t, jnp.zeros_like(km_t)], axis=0).T
        km_scr[...] = km_sq[:, :HEAD_DIM]

    nb = sc_scr.shape[0]
    kh, kl = _split2(km_scr[0:nb, :])
    blk = lax.broadcasted_iota(jnp.int32, (nb, tq), 0)
    own = (i * tq + lax.broadcasted_iota(jnp.int32, (nb, tq), 1)) // MOBA_BLOCK
    nloop = jnp.minimum((i * tq + tq - 1) // MOBA_BLOCK + 1, nb)
    pad_rows = jnp.ones((NBLK_PAD - nb, LANES), F32)
    for gi in range(g):
        qg = q_ref[:, gi * HEAD_DIM:(gi + 1) * HEAD_DIM]
        gate_t = (_dot_nt(kh, qg) + _dot_nt(kl, qg)) * (1.0 / SCALE)
        score = jnp.where(blk == own, SEL_BIG, jnp.where(blk < own, gate_t, -SEL_BIG))
        cols = _rank_nsel(score, sc_scr, nloop, MOBA_TOPK + 1, -SEL_BIG / 2)
        for c, nsel_t in enumerate(cols):
            full = jnp.concatenate([nsel_t, pad_rows], axis=0) if nb < NBLK_PAD else nsel_t
            nsel_ref[c * LANES:(c + 1) * LANES, gi * NBLK_PAD:(gi + 1) * NBLK_PAD] = full.T.astype(BF16)


def _moba_gate(q, kv_t, b, t, n_kv, g):
    tq = min(TQ, t)
    nq = t // tq
    avg = (np.arange(t)[:, None] // MOBA_BLOCK) == np.arange(NBLK_PAD)[None, :]
    avg = jnp.asarray(np.where(avg, 1.0 / MOBA_BLOCK, 0.0), BF16)
    row = lambda bg, i: ((bg // n_kv) * nq + i, bg % n_kv)
    return pl.pallas_call(
        functools.partial(_moba_gate_kernel, g=g),
        grid=(b * n_kv, nq),
        in_specs=[pl.BlockSpec((tq, g * HEAD_DIM), row),
                  pl.BlockSpec((1, HEAD_DIM, t), lambda bg, i: (bg // n_kv, bg % n_kv, 0)),
                  pl.BlockSpec((t, NBLK_PAD), lambda bg, i: (0, 0))],
        out_specs=pl.BlockSpec((tq, g * NBLK_PAD), row),
        out_shape=jax.ShapeDtypeStruct((b * t, n_kv * g * NBLK_PAD), BF16),
        scratch_shapes=[pltpu.VMEM((NBLK_PAD, HEAD_DIM), F32),
                        pltpu.VMEM((-(-(t // MOBA_BLOCK) // SUBLANES) * SUBLANES, tq), F32)],
        compiler_params=_cparams("parallel", "arbitrary"),
        name="moba_gate",
    )(q, kv_t, avg)


def _block_onehot_neg_t(t, block):
    oh = np.arange(NBLK_PAD)[:, None] == (np.arange(t)[None, :] // block)
    return jnp.asarray(np.where(oh, NEG_INF, 0.0), BF16)


def _cmp_to_sel_t(ncp, nc, nsb):
    cs = CMP_STRIDE * np.arange(ncp)[None, :]
    ss = SEL_BLOCK * np.arange(NBLK_PAD)[:, None]
    ov = np.clip(np.minimum(cs + CMP_LEN, ss + SEL_BLOCK) - np.maximum(cs, ss), 0, None) / CMP_LEN
    ov = np.where((np.arange(ncp)[None, :] < nc) & (np.arange(NBLK_PAD)[:, None] < nsb), ov, 0.0)
    return jnp.asarray(ov, BF16)


def _rows_from_t(x_t, b, t, n_heads):
    return x_t.reshape(b, 2, n_heads, HEAD_DIM, t).transpose(0, 4, 1, 2, 3)


def _gate_cols(w_g):
    d = w_g.shape[0]
    w = w_g.reshape(d, NSA_KV_HEADS, G_NSA, 3).transpose(0, 1, 3, 2).reshape(d, NSA_KV_HEADS, 3 * G_NSA)
    return jnp.pad(w, ((0, 0), (0, 0), (0, LANES - 3 * G_NSA))).reshape(d, NSA_KV_HEADS * LANES)


_E_COLS = dict(q=(0, 512), kv=(512, 1280), g=(1280, 1304), qs=(1304, 1816), sb=(1816, 2328))
_E_T_MODES = ('raw',) * 4 + ('norm',) * 2 + ('raw',) * 2 + ('norm',) * 2 + ('raw',) * 2 + ('raw',) * 8
_E_T_OUTS = ((0, 4), (4, 8), (8, 12), (12, 20))


def _even_t_weights(w_in, k_gain):
    cut = lambda k: w_in[:, _E_COLS[k][0]:_E_COLS[k][1]]
    w_t = jnp.concatenate([cut('kv'), cut('sb')], axis=1).T.astype(BF16)
    one = jnp.ones((HEAD_DIM,), F32)
    gt = jnp.concatenate([one] * 4 + [k_gain[1]] * 2 + [one] * 2 + [k_gain[2]] * 2 + [one] * 10)
    return w_t, gt


def _pair_cols(w_k, w_v, n_heads):
    d = w_k.shape[0]
    return jnp.concatenate([w_k.reshape(d, n_heads, HEAD_DIM), w_v.reshape(d, n_heads, HEAD_DIM)], axis=2).reshape(d, n_heads * LANES)


def _even_prompt(h, b, t, P, j, li):
    w_in = P['w_in_even'][j]
    q_gain, k_gain = P['nsa_q_norm'][j], P['nsa_k_norm'][j]
    cut = lambda k: w_in[:, _E_COLS[k][0]:_E_COLS[k][1]]
    w_row = jnp.concatenate([cut('q'), cut('qs'), _gate_cols(cut('g')), cut('kv')[:, :256]], axis=1).astype(BF16)
    cs = jnp.concatenate([jnp.tile(q_gain, NSA_HEADS) * SCALE, jnp.full((512,), SCALE, F32), jnp.ones((512,), F32)])
    row_modes = ('norm',) * 4 + ('scale',) * 4 + ('sigmoid',) * 2 + ('scale',) * 2
    w_t, gt = _even_t_weights(w_in, k_gain)
    q, qs, gates, kcvc, cmp_t, slc_t, win_t, sb_t = _proj(
        h, b, t, P['norm_mix'][li], w_row, cs, row_modes, ((0, 4), (4, 8), (8, 10), (10, 12)),
        (BF16, BF16, F32, BF16), w_t, gt, _E_T_MODES, _E_T_OUTS)
    tb = P['rel_bias_table']
    hkv, g = NSA_KV_HEADS, G_NSA

    ncp = t // CMP_STRIDE
    x4 = kcvc.reshape(b, ncp, CMP_STRIDE, 2 * hkv, HEAD_DIM).transpose(0, 3, 1, 2, 4).reshape(b, 2, hkv, ncp, CMP_STRIDE * HEAD_DIM)
    w1, pos, w2 = P['nsa_cmp_w1'][j], P['nsa_cmp_pos'][j], P['nsa_cmp_w2'][j]
    kc = _cmp_mlp(x4[:, 0].reshape(b * hkv, ncp, -1), w1, pos, w2, k_gain[0], 0, True).astype(BF16)
    vc = _cmp_mlp(x4[:, 1].reshape(b * hkv, ncp, -1), w1, pos, w2, k_gain[0], 1, False).astype(BF16)
    o, nsel = _nsa_cmp(tb, q, kc, vc, _cmp_to_sel_t(ncp, ncp - 1, t // SEL_BLOCK), gates, b, t, hkv, g)

    nd_full = -(-(SAT_DIST + TK - 1) // TQ) + 1
    o = _flash_attn(q, slc_t, 0, hkv, _tile_bias(tb, hkv, g, nd_full), b, t, hkv, g, nsel=nsel,
                    onehot_t=_block_onehot_neg_t(t, SEL_BLOCK), acc=o, gates=gates, branch=1)
    look = (WINDOW - 1 + TK - 1) // TK
    o = _flash_attn(q, win_t, 0, hkv, _tile_bias(tb, hkv, g, look + 1, win=WINDOW), b, t, hkv, g, look=look,
                    acc=o, gates=gates, branch=2)

    o_b = _sb_attn(qs, sb_t, b, t, SB_KV_HEADS, G_SB)
    h = _out_proj([o, o_b], P['w_out_even'][j].astype(BF16), h)
    wb = min(WINDOW, t)
    states = (_rows_from_t(cmp_t, b, t, hkv), _rows_from_t(slc_t, b, t, hkv),
              _rows_from_t(win_t[:, :, t - wb:], b, wb, hkv), _rows_from_t(sb_t, b, t, SB_KV_HEADS))
    return h, states


def _odd_prompt(h, b, t, P, j, li):
    hkv, g = MOBA_KV_HEADS, G_MOBA
    w_in = P['w_in_odd'][j]
    w_row = w_in[:, :1024].astype(BF16)
    cs = jnp.tile(P['moba_q_norm'][j], MOBA_HEADS) * SCALE
    w_t = w_in[:, 1024:].T.astype(BF16)
    gt = jnp.concatenate([jnp.tile(P['moba_k_norm'][j], hkv), jnp.ones((hkv * HEAD_DIM,), F32)])
    q, kv_t = _proj(h, b, t, P['norm_mix'][li], w_row, cs, ('norm',) * 8, ((0, 8),), (BF16,),
                    w_t, gt, ('norm',) * 4 + ('raw',) * 4, ((0, 8),))
    tb = P['rel_bias_table']
    nsel = _moba_gate(q, kv_t, b, t, hkv, g)
    nd_full = -(-(SAT_DIST + TK - 1) // TQ) + 1
    o = _flash_attn(q, kv_t, 0, hkv, _tile_bias(tb, hkv, g, nd_full), b, t, hkv, g, nsel=nsel,
                    onehot_t=_block_onehot_neg_t(t, MOBA_BLOCK))
    h = _out_proj([o], P['w_out_odd'][j].astype(BF16), h)
    return h, (_rows_from_t(kv_t, b, t, hkv),)


def _ffn_and_pe(h, pemb, P, li):
    j = li // 2
    if li % 2 == 0:
        h = _ffn(h, P['norm_ffn'][li], P['w_ffn_gu'][j].astype(BF16), P['w_ffn_down'][j].astype(BF16), 256)
    else:
        gates, gates_t = _router(h, P['norm_ffn'][li], P['w_router'][j], P['b_router'][j])
        h = _moe(h, P['norm_ffn'][li], gates, gates_t, P['w_moe_gu'][j].astype(BF16), P['w_moe_down'][j].astype(BF16), 512)
    return _pe_update(h, P['norm_pe'][li], pemb, P['w_pe_gate'][li].astype(BF16), P['w_pe_in'][li].astype(BF16))


def _trunk_prompt(x, p, P):
    b, t, d = x.shape
    h = x.reshape(b * t, d)
    even_states, odd_states = [], []
    for li in range(p.shape[0]):
        j = li // 2
        if li % 2 == 0:
            h, st = _even_prompt(h, b, t, P, j, li)
            even_states.append(st)
        else:
            h, st = _odd_prompt(h, b, t, P, j, li)
            odd_states.append(st)
        h = _ffn_and_pe(h, p[li].reshape(b * t, -1), P, li)
    return h.reshape(b, t, d), even_states, odd_states


def _kv_rows(x_ref, h):
    kv_t = jnp.concatenate([x_ref[0, h], x_ref[1, h]], axis=0)
    n = kv_t.shape[1]
    return jnp.concatenate([kv_t[:, c * LANES:(c + 1) * LANES].T for c in range(n // LANES)], axis=0)


def _gather_kernel(pt_ref, *refs):
    o_ref = refs[-1]
    n_heads = o_ref.shape[1]
    for p, x_ref in enumerate(refs[:-1]):
        for h in range(n_heads):
            o_ref[0, h, p * PAGE_SIZE:(p + 1) * PAGE_SIZE, :] = _kv_rows(x_ref, h)


def _gather_kv(pool_t, layer, page_table):
    db, npg = page_table.shape
    _, _, two, hh, dd, ps = pool_t.shape

    def in_map(p):
        return lambda b, pt: (layer, pt[b * npg + p], 0, 0, 0, 0)

    grid_spec = pltpu.PrefetchScalarGridSpec(
        num_scalar_prefetch=1,
        grid=(db,),
        in_specs=[pl.BlockSpec((None, None, two, hh, dd, ps), in_map(p)) for p in range(npg)],
        out_specs=pl.BlockSpec((1, hh, npg * ps, LANES), lambda b, pt: (b, 0, 0, 0)),
    )
    return pl.pallas_call(
        _gather_kernel,
        grid_spec=grid_spec,
        out_shape=jax.ShapeDtypeStruct((db, hh, npg * ps, LANES), F32),
        compiler_params=_cparams("parallel"),
        name="gather_kv",
    )(page_table.reshape(-1), *([pool_t] * npg))


def _win_rows_kernel(x_ref, o_ref):
    for h in range(o_ref.shape[1]):
        o_ref[0, h] = _kv_rows(x_ref, h)


def _win_rows(win_t, layer):
    _, db, two, hh, dd, w = win_t.shape
    return pl.pallas_call(
        _win_rows_kernel,
        grid=(db,),
        in_specs=[pl.BlockSpec((None, None, two, hh, dd, w), lambda b: (layer, b, 0, 0, 0, 0))],
        out_specs=pl.BlockSpec((1, hh, w, LANES), lambda b: (b, 0, 0, 0)),
        out_shape=jax.ShapeDtypeStruct((db, hh, w, LANES), F32),
        compiler_params=_cparams("parallel"),
        name="win_rows",
    )(win_t)


def _win_shift_kernel(x_ref, new_ref, o_ref):
    x = x_ref[0]
    lane = lax.broadcasted_iota(jnp.int32, x.shape, 1)
    o_ref[0] = jnp.where(lane == x.shape[1] - 1, new_ref[0], pltpu.roll(x, x.shape[1] - 1, 1))


def _win_shift(win_t, layer, new_col):
    _, db, two, hh, dd, w = win_t.shape
    rows = two * hh * dd
    x = win_t.reshape(win_t.shape[0], db, rows, w)
    out = pl.pallas_call(
        _win_shift_kernel,
        grid=(db,),
        in_specs=[pl.BlockSpec((None, 1, rows, w), lambda b: (layer, b, 0, 0)),
                  pl.BlockSpec((1, rows, 1), lambda b: (b, 0, 0))],
        out_specs=pl.BlockSpec((1, rows, w), lambda b: (b, 0, 0)),
        out_shape=jax.ShapeDtypeStruct((db, rows, w), F32),
        compiler_params=_cparams("parallel"),
        name="win_shift",
    )(x, new_col)
    return out.reshape(db, two, hh, dd, w)


def _dec_cmp_mlp_kernel(kv_ref, w1_ref, pos_ref, w1f_ref, w2_ref, g_ref, o_ref):
    n_heads, rows = kv_ref.shape[1], kv_ref.shape[2]
    nc = rows // CMP_STRIDE
    posb = jnp.concatenate([_pos_term(pos_ref.at[0:1], w1f_ref.at[0:1]), _pos_term(pos_ref.at[1:2], w1f_ref.at[1:2])], axis=-1)
    left = _left_half()
    for h in range(n_heads):
        acc = jnp.zeros((nc, 2 * LANES), F32)
        for l in range(CMP_STRIDE):
            xl = kv_ref[0, h, pl.ds(l, nc, stride=CMP_STRIDE), :].astype(BF16)
            acc = acc + _dot(xl, w1_ref[l])
        z = acc[:, :LANES] + pltpu.roll(acc[:, LANES:], nc - 1, 0) + posb
        y = _dot(jax.nn.gelu(z).astype(BF16), w2_ref[...])
        y = y * jnp.where(left, _half_rms_inv(y, False) * g_ref[...], 1.0)
        o_ref[0, h] = y


def _dec_cmp_mlp(kv, w1, pos, w2, gain):
    db, hh, rows, _ = kv.shape
    nc = rows // CMP_STRIDE
    z64 = jnp.zeros((CMP_LEN, HEAD_DIM, HEAD_DIM), F32)
    wd = jnp.concatenate([jnp.concatenate([w1[0], z64], axis=2), jnp.concatenate([z64, w1[1]], axis=2)], axis=1)
    w1d = jnp.concatenate([wd[:CMP_STRIDE], wd[CMP_STRIDE:]], axis=2).astype(BF16)
    z2 = jnp.zeros((HEAD_DIM, HEAD_DIM), F32)
    w2d = jnp.concatenate([jnp.concatenate([w2[0], z2], axis=1), jnp.concatenate([z2, w2[1]], axis=1)], axis=0).astype(BF16)
    w1f = w1.reshape(2, CMP_LEN * HEAD_DIM, HEAD_DIM)
    posf = jnp.broadcast_to(pos.reshape(2, 1, CMP_LEN * HEAD_DIM), (2, SUBLANES, CMP_LEN * HEAD_DIM))
    gpad = jnp.concatenate([gain, jnp.ones((HEAD_DIM,), F32)]).reshape(1, LANES)
    return pl.pallas_call(
        _dec_cmp_mlp_kernel,
        grid=(db,),
        in_specs=[pl.BlockSpec((1, hh, rows, LANES), lambda b: (b, 0, 0, 0)),
                  pl.BlockSpec(w1d.shape, lambda b: (0, 0, 0)),
                  pl.BlockSpec(posf.shape, lambda b: (0, 0, 0)),
                  pl.BlockSpec(w1f.shape, lambda b: (0, 0, 0)),
                  pl.BlockSpec(w2d.shape, lambda b: (0, 0)),
                  pl.BlockSpec((1, LANES), lambda b: (0, 0))],
        out_specs=pl.BlockSpec((1, hh, nc, LANES), lambda b: (b, 0, 0, 0)),
        out_shape=jax.ShapeDtypeStruct((db, hh, nc, LANES), F32),
        compiler_params=_cparams("parallel"),
        name="dec_cmp_mlp",
    )(kv, w1d, posf, w1f, w2d, gpad)


def _dec_call(body, kv, in_specs, args, out_specs, out_shape, name):
    if not isinstance(kv, tuple):
        db, hh, rows, _ = kv.shape
        return pl.pallas_call(
            body, grid=(db,),
            in_specs=[pl.BlockSpec((1, hh, rows, LANES), lambda b: (b, 0, 0, 0))] + in_specs,
            out_specs=out_specs, out_shape=out_shape, compiler_params=_cparams("parallel"), name=name,
        )(kv, *args)
    pool_t, layer, page_table = kv
    db, npg = page_table.shape
    _, _, two, hh, dd, ps = pool_t.shape

    def paged_body(pt_ref, *refs):
        kv_scr = refs[-1]
        for p in range(npg):
            for h in range(hh):
                kv_scr[0, h, p * ps:(p + 1) * ps, :] = _kv_rows(refs[p], h).astype(BF16)
        body(kv_scr, *refs[npg:-1])

    def with_pt(spec):
        return pl.BlockSpec(spec.block_shape, lambda b, pt, f=spec.index_map: f(b))

    def page_map(p):
        return lambda b, pt: (layer, pt[b * npg + p], 0, 0, 0, 0)

    grid_spec = pltpu.PrefetchScalarGridSpec(
        num_scalar_prefetch=1,
        grid=(db,),
        in_specs=[pl.BlockSpec((None, None, two, hh, dd, ps), page_map(p)) for p in range(npg)] + [with_pt(s) for s in in_specs],
        out_specs=[with_pt(s) for s in out_specs],
        scratch_shapes=[pltpu.VMEM((1, hh, npg * ps, LANES), BF16)],
    )
    return pl.pallas_call(
        paged_body, grid_spec=grid_spec, out_shape=out_shape, compiler_params=_cparams("parallel"), name=name + "_paged",
    )(page_table.reshape(-1), *([pool_t] * npg), *args)


def _dec_scores(q, kv_ref, new_ref, h):
    kvb = kv_ref[0, h].astype(BF16)
    s = _dot_nt(q, kvb)
    if new_ref is None:
        return s, kvb, None
    nb = new_ref[0, h].astype(BF16)
    return jnp.concatenate([s, _dot_nt(q, nb)], axis=-1), kvb, nb


def _dec_softmax_out(s, kvb, nb):
    m = jnp.max(s, axis=-1, keepdims=True)
    e = jnp.where(s > 0.5 * NEG_INF, jnp.exp(s - m), 0.0)
    p = e / jnp.maximum(jnp.sum(e, axis=-1, keepdims=True), 1e-30)
    pb = p.astype(BF16)
    nm = kvb.shape[0]
    o = _dot(pb[:, :nm], kvb)
    if nb is not None:
        o = o + _dot(pb[:, nm:], nb)
    return p, o


def _dec_attn_kernel(*refs, g, has_new, aug, gated, imp):
    it = iter(refs)
    kv_ref, q_ref = next(it), next(it)
    new_ref = next(it) if has_new else None
    b_ref = next(it)
    qa_ref, oh_ref = (next(it), next(it)) if aug else (None, None)
    acc_ref, gate_ref = (next(it), next(it)) if gated else (None, None)
    c2s_ref = next(it) if imp else None
    o_ref = next(it)
    imp_ref = next(it) if imp else None
    for h in range(q_ref.shape[1]):
        s, kvb, nb = _dec_scores(q_ref[0, h], kv_ref, new_ref, h)
        s = s + b_ref[h]
        if aug:
            s = s + _dot(qa_ref[0, h], oh_ref[...])
        p, o = _dec_softmax_out(s, kvb, nb)
        if gated:
            o = acc_ref[0, h] + gate_ref[0, h] * o
        o_ref[0, h] = o
        if imp:
            row = lax.broadcasted_iota(jnp.int32, p.shape, 0)
            psum = jnp.sum(jnp.where(row < g, p, 0.0), axis=0, keepdims=True)
            ph, plo = _split2(jnp.broadcast_to(psum, p.shape))
            imp_ref[0, h] = _dot_nt(ph, c2s_ref[...]) + _dot_nt(plo, c2s_ref[...])


def _dec_attn(q, kv, new, bias, g, qa=None, onehot_t=None, acc=None, gate=None, c2s_t=None):
    db, n_kv, _, _ = q.shape
    tp = bias.shape[2]
    has_new, aug, gated, imp = new is not None, qa is not None, acc is not None, c2s_t is not None
    row4 = lambda w: pl.BlockSpec((1, n_kv, DEC_ROWS, w), lambda b: (b, 0, 0, 0))
    in_specs = [row4(LANES)]
    args = [q]
    if has_new:
        in_specs.append(pl.BlockSpec((1, n_kv, LANES, LANES), lambda b: (b, 0, 0, 0)))
        args.append(new)
    in_specs.append(pl.BlockSpec((n_kv, DEC_ROWS, tp), lambda b: (0, 0, 0)))
    args.append(bias)
    if aug:
        in_specs += [row4(NBLK_PAD), pl.BlockSpec((NBLK_PAD, tp), lambda b: (0, 0))]
        args += [qa, onehot_t]
    if gated:
        in_specs += [row4(LANES), row4(1)]
        args += [acc, gate]
    if imp:
        in_specs += [pl.BlockSpec((NBLK_PAD, tp), lambda b: (0, 0))]
        args += [c2s_t]
    out_specs = [row4(LANES)]
    out_shape = [jax.ShapeDtypeStruct((db, n_kv, DEC_ROWS, LANES), F32)]
    if imp:
        out_specs.append(row4(NBLK_PAD))
        out_shape.append(jax.ShapeDtypeStruct((db, n_kv, DEC_ROWS, NBLK_PAD), F32))
    res = _dec_call(functools.partial(_dec_attn_kernel, g=g, has_new=has_new, aug=aug, gated=gated, imp=imp),
                    kv, in_specs, args, out_specs, out_shape, "dec_attn")
    return res if imp else res[0]


def _dec_moba_kernel(kv_ref, q_ref, new_ref, b_ref, oh_ref, o_ref, *, own):
    lane = lax.broadcasted_iota(jnp.int32, (DEC_ROWS, NBLK_PAD), 1)
    for h in range(q_ref.shape[1]):
        s, kvb, nb = _dec_scores(q_ref[0, h], kv_ref, new_ref, h)
        gate = jnp.full((DEC_ROWS, NBLK_PAD), -SEL_BIG, F32)
        for n in range(own):
            gn = jnp.sum(s[:, n * MOBA_BLOCK:(n + 1) * MOBA_BLOCK], axis=-1, keepdims=True) * (1.0 / (SCALE * MOBA_BLOCK))
            gate = jnp.where(lane == n, gn, gate)
        score = jnp.where(lane == own, SEL_BIG, gate)
        cnt = jnp.zeros(score.shape, F32)
        for n in range(own + 1):
            col = score[:, n:n + 1]
            cnt = cnt + jnp.where((col > score) | ((col == score) & (n < lane)), 1.0, 0.0)
        sel = (cnt < MOBA_TOPK + 1) & (score > -SEL_BIG / 2)
        nsel = jnp.where(sel, 0.0, 1.0).astype(BF16)
        s = s + b_ref[h] + _dot(nsel, oh_ref[...])
        _, o = _dec_softmax_out(s, kvb, nb)
        o_ref[0, h] = o


def _dec_moba(q, kv, new, bias, onehot_t, past):
    db, n_kv, _, _ = q.shape
    tp = bias.shape[2]
    row4 = pl.BlockSpec((1, n_kv, DEC_ROWS, LANES), lambda b: (b, 0, 0, 0))
    in_specs = [row4,
                pl.BlockSpec((1, n_kv, LANES, LANES), lambda b: (b, 0, 0, 0)),
                pl.BlockSpec((n_kv, DEC_ROWS, tp), lambda b: (0, 0, 0)),
                pl.BlockSpec((NBLK_PAD, tp), lambda b: (0, 0))]
    return _dec_call(functools.partial(_dec_moba_kernel, own=past // MOBA_BLOCK), kv, in_specs,
                     [q, new, bias, onehot_t], [row4], [jax.ShapeDtypeStruct((db, n_kv, DEC_ROWS, LANES), F32)], "dec_moba")[0]


def _dec_sb_kernel(kv_ref, q_ref, u_ref, o_ref):
    u = u_ref[...]
    nchunk = kv_ref.shape[2] // TK
    for h in range(q_ref.shape[1]):
        kvb = kv_ref[0, h].astype(BF16)
        z = _dot_nt(q_ref[0, h], kvb)
        sp = jnp.maximum(z, 0.0) + jnp.log1p(jnp.exp(-jnp.abs(z)))
        lk = -sp
        hi, lo = _split2(lk)
        cols = [slice(c * TK, (c + 1) * TK) for c in range(nchunk)]
        tot = [jnp.sum(lk[:, sl], axis=-1, keepdims=True) for sl in cols]
        between = []
        carry = jnp.zeros((DEC_ROWS, 1), F32)
        for c in reversed(range(nchunk)):
            between.append(_dot(hi[:, cols[c]], u) + _dot(lo[:, cols[c]], u) + carry)
            carry = carry + tot[c]
        a = jnp.exp(z - sp + jnp.concatenate(between[::-1], axis=-1))
        o_ref[0, h] = _dot(a.astype(BF16), kvb)


def _dec_sb(q, kv):
    db, n_kv, _, _ = q.shape
    idx = np.arange(TK)
    u = jnp.asarray(idx[:, None] > idx[None, :], BF16)
    row4 = pl.BlockSpec((1, n_kv, DEC_ROWS, LANES), lambda b: (b, 0, 0, 0))
    return _dec_call(_dec_sb_kernel, kv, [row4, pl.BlockSpec((TK, TK), lambda b: (0, 0))], [q, u], [row4],
                     [jax.ShapeDtypeStruct((db, n_kv, DEC_ROWS, LANES), F32)], "dec_sb")[0]


def _rank_kernel(s_ref, o_ref, sc_scr, *, qpos, block, kth):
    n = s_ref.shape[2]
    blk = lax.broadcasted_iota(jnp.int32, (NBLK_PAD, n), 0)
    own = qpos // block
    past_score = jnp.where(blk * block <= qpos, s_ref[0], -SEL_BIG)
    score = jnp.where((blk == own) | (blk == 0), SEL_BIG, past_score)
    cols = _rank_nsel(score, sc_scr, min(own + 1, NBLK_PAD), kth, -jnp.inf)
    for c, nsel_t in enumerate(cols):
        o_ref[0, c * LANES:(c + 1) * LANES, :] = nsel_t.T.astype(BF16)


def _rank_select(score_t, qpos, block, kth):
    x, _, n = score_t.shape
    return pl.pallas_call(
        functools.partial(_rank_kernel, qpos=qpos, block=block, kth=kth),
        grid=(x,),
        in_specs=[pl.BlockSpec((1, NBLK_PAD, n), lambda i: (i, 0, 0))],
        out_specs=pl.BlockSpec((1, n, NBLK_PAD), lambda i: (i, 0, 0)),
        out_shape=jax.ShapeDtypeStruct((x, n, NBLK_PAD), BF16),
        scratch_shapes=[pltpu.VMEM((NBLK_PAD, n), F32)],
        compiler_params=_cparams("parallel"),
        name="rank_select",
    )(score_t)


def _dec_q(x, db, n_kv, g):
    x = x.reshape(db, n_kv, g, HEAD_DIM)
    return jnp.pad(x, ((0, 0), (0, 0), (0, DEC_ROWS - g), (0, LANES - HEAD_DIM)))


def _dec_new(x, db, n_heads):
    return jnp.pad(x.reshape(db, n_heads, 1, LANES), ((0, 0), (0, 0), (0, LANES - 1), (0, 0)))


def _dec_bias(tb, n_kv, g, qpos, kpos, win=None):
    t = _bias_table(tb, n_kv * g, np.full(DEC_ROWS, qpos), kpos, win)[:, 0]
    return jnp.pad(t.reshape(n_kv, g, -1), ((0, 0), (0, DEC_ROWS - g), (0, 0)))


def _dec_out(o, db, n_kv, g):
    return o[:, :, :g, HEAD_DIM:].reshape(db, n_kv * g * HEAD_DIM)


def _dec_state(x_t, db, n_heads):
    return x_t.reshape(2, n_heads, HEAD_DIM, db).transpose(3, 0, 1, 2).reshape(db, 1, 2, n_heads, HEAD_DIM)


def _pool_t(pool):
    return pool.transpose(0, 1, 3, 4, 5, 2)


def _even_sample(h, P, j, li, c_cmp, c_slc, c_win, c_sb, page_table):
    db = h.shape[0]
    w_in = P['w_in_even'][j]
    q_gain, k_gain = P['nsa_q_norm'][j], P['nsa_k_norm'][j]
    cut = lambda k: w_in[:, _E_COLS[k][0]:_E_COLS[k][1]]
    hkv, g = NSA_KV_HEADS, G_NSA
    kvw = cut('kv')
    w_row = jnp.concatenate([cut('q'), cut('qs'), _gate_cols(cut('g')),
                             _pair_cols(kvw[:, 256:384], kvw[:, 384:512], hkv),
                             _pair_cols(kvw[:, 512:640], kvw[:, 640:768], hkv),
                             _pair_cols(cut('sb')[:, :256], cut('sb')[:, 256:], SB_KV_HEADS)], axis=1).astype(BF16)
    one = jnp.ones((HEAD_DIM,), F32)
    cs = jnp.concatenate([jnp.tile(q_gain, NSA_HEADS) * SCALE, jnp.full((512,), SCALE, F32), jnp.ones((256,), F32),
                          jnp.tile(jnp.concatenate([k_gain[1], one]), hkv), jnp.tile(jnp.concatenate([k_gain[2], one]), hkv),
                          jnp.ones((512,), F32)])
    row_modes = ('norm',) * 4 + ('scale',) * 4 + ('sigmoid',) * 2 + ('normk',) * 4 + ('scale',) * 4
    w_t, gt = _even_t_weights(w_in, k_gain)
    q, qs, gates, slc_new, win_new, sb_new, cmp_t, slc_t, win_t, sb_t = _proj(
        h, 1, db, P['norm_mix'][li], w_row, cs, row_modes, ((0, 4), (4, 8), (8, 10), (10, 12), (12, 14), (14, 18)),
        (BF16, BF16, F32, F32, F32, F32), w_t, gt, _E_T_MODES, _E_T_OUTS)
    tb = P['rel_bias_table']
    qp = _dec_q(q, db, hkv, g)
    gcol = lambda br: jnp.pad(gates.reshape(db, hkv, LANES)[:, :, br * g:(br + 1) * g], ((0, 0), (0, 0), (0, DEC_ROWS - g)))[..., None]
    past = page_table.shape[1] * PAGE_SIZE
    tp = past + LANES
    kpos = np.where(np.arange(tp) <= past, np.arange(tp), -1)

    ncp = past // CMP_STRIDE
    nc = (past + 1 - CMP_LEN) // CMP_STRIDE + 1
    kcvc = _dec_cmp_mlp(_gather_kv(_pool_t(c_cmp), j, page_table), P['nsa_cmp_w1'][j], P['nsa_cmp_pos'][j],
                        P['nsa_cmp_w2'][j], k_gain[0])
    kpos_c = np.where(np.arange(ncp) < nc, CMP_STRIDE * np.arange(ncp) + CMP_LEN - 1, -1)
    nsb = -(-(past + 1) // SEL_BLOCK)
    o, imp = _dec_attn(qp, kcvc, None, _dec_bias(tb, hkv, g, past, kpos_c), g,
                       acc=jnp.zeros((db, hkv, DEC_ROWS, LANES), F32), gate=gcol(0), c2s_t=_cmp_to_sel_t(ncp, nc, nsb))
    nsel = _rank_select(imp[:, :, 0].transpose(1, 2, 0), past, SEL_BLOCK, N_SEL)
    qa = jnp.broadcast_to(nsel.transpose(1, 0, 2)[:, :, None], (db, hkv, DEC_ROWS, NBLK_PAD))

    o = _dec_attn(qp, (_pool_t(c_slc), j, page_table), _dec_new(slc_new, db, hkv),
                  _dec_bias(tb, hkv, g, past, kpos), g, qa=qa, onehot_t=_block_onehot_neg_t(tp, SEL_BLOCK),
                  acc=o, gate=gcol(1))

    cw_t = _pool_t(c_win)
    wb = c_win.shape[2]
    kpos_w = np.where(np.arange(wb + LANES) <= wb, past - wb + np.arange(wb + LANES), -1)
    o = _dec_attn(qp, _win_rows(cw_t, j), _dec_new(win_new, db, hkv),
                  _dec_bias(tb, hkv, g, past, kpos_w, win=WINDOW), g, acc=o, gate=gcol(2))

    o_b = _dec_sb(_dec_q(qs, db, SB_KV_HEADS, G_SB), (_pool_t(c_sb), j, page_table))
    h = _out_proj([_dec_out(o, db, hkv, g), _dec_out(o_b, db, SB_KV_HEADS, G_SB)], P['w_out_even'][j].astype(BF16), h)

    win_state = _win_shift(cw_t, j, win_t[0].T[:, :, None]).transpose(0, 4, 1, 2, 3)
    return h, (_dec_state(cmp_t, db, hkv), _dec_state(slc_t, db, hkv), win_state, _dec_state(sb_t, db, SB_KV_HEADS))


def _odd_sample(h, P, j, li, c_moba, page_table):
    db = h.shape[0]
    hkv, g = MOBA_KV_HEADS, G_MOBA
    w_in = P['w_in_odd'][j]
    w_row = jnp.concatenate([w_in[:, :1024], _pair_cols(w_in[:, 1024:1280], w_in[:, 1280:], hkv)], axis=1).astype(BF16)
    one = jnp.ones((HEAD_DIM,), F32)
    cs = jnp.concatenate([jnp.tile(P['moba_q_norm'][j], MOBA_HEADS) * SCALE, jnp.tile(jnp.concatenate([P['moba_k_norm'][j], one]), hkv)])
    w_t = w_in[:, 1024:].T.astype(BF16)
    gt = jnp.concatenate([jnp.tile(P['moba_k_norm'][j], hkv), jnp.ones((hkv * HEAD_DIM,), F32)])
    q, kv_new, kv_t = _proj(h, 1, db, P['norm_mix'][li], w_row, cs, ('norm',) * 8 + ('normk',) * 4, ((0, 8), (8, 12)),
                            (BF16, F32), w_t, gt, ('norm',) * 4 + ('raw',) * 4, ((0, 8),))
    past = page_table.shape[1] * PAGE_SIZE
    tp = past + LANES
    kpos = np.where(np.arange(tp) <= past, np.arange(tp), -1)
    o = _dec_moba(_dec_q(q, db, hkv, g), (_pool_t(c_moba), j, page_table), _dec_new(kv_new, db, hkv),
                  _dec_bias(P['rel_bias_table'], hkv, g, past, kpos), _block_onehot_neg_t(tp, MOBA_BLOCK), past)
    h = _out_proj([_dec_out(o, db, hkv, g)], P['w_out_odd'][j].astype(BF16), h)
    return h, (_dec_state(kv_t, db, hkv),)


def _trunk_sample(x, p, P, caches, page_table):
    db, t, d = x.shape
    h = x.reshape(db * t, d)
    c_cmp, c_slc, c_win, c_sb, c_moba = caches
    even_states, odd_states = [], []
    for li in range(p.shape[0]):
        j = li // 2
        if li % 2 == 0:
            h, st = _even_sample(h, P, j, li, c_cmp, c_slc, c_win, c_sb, page_table)
            even_states.append(st)
        else:
            h, st = _odd_sample(h, P, j, li, c_moba, page_table)
            odd_states.append(st)
        h = _ffn_and_pe(h, p[li].reshape(db * t, -1), P, li)
    return h.reshape(db, t, d), even_states, odd_states


def kernel(x_prompt, x_sample, cache_nsa_cmp, cache_nsa_slc, cache_nsa_win, cache_sb, cache_moba, page_table,
           p_prompt, p_sample, rel_bias_table, norm_mix, norm_ffn, norm_pe, w_pe_in, w_pe_gate,
           w_in_even, w_out_even, nsa_q_norm, nsa_k_norm, nsa_cmp_pos, nsa_cmp_w1, nsa_cmp_w2,
           w_ffn_gu, w_ffn_down, w_in_odd, w_out_odd, moba_q_norm, moba_k_norm,
           w_router, b_router, w_moe_gu, w_moe_down):
    P = {'rel_bias_table': rel_bias_table, 'norm_mix': norm_mix, 'norm_ffn': norm_ffn, 'norm_pe': norm_pe,
         'w_pe_in': w_pe_in, 'w_pe_gate': w_pe_gate, 'w_in_even': w_in_even, 'w_out_even': w_out_even,
         'nsa_q_norm': nsa_q_norm, 'nsa_k_norm': nsa_k_norm, 'nsa_cmp_pos': nsa_cmp_pos, 'nsa_cmp_w1': nsa_cmp_w1,
         'nsa_cmp_w2': nsa_cmp_w2, 'w_ffn_gu': w_ffn_gu, 'w_ffn_down': w_ffn_down, 'w_in_odd': w_in_odd,
         'w_out_odd': w_out_odd, 'moba_q_norm': moba_q_norm, 'moba_k_norm': moba_k_norm, 'w_router': w_router,
         'b_router': b_router, 'w_moe_gu': w_moe_gu, 'w_moe_down': w_moe_down}
    y_prompt, pe, po = _trunk_prompt(x_prompt, p_prompt, P)
    caches = (cache_nsa_cmp, cache_nsa_slc, cache_nsa_win, cache_sb, cache_moba)
    y_sample, se, so = _trunk_sample(x_sample, p_sample, P, caches, page_table)
    stack = lambda states, k: jnp.stack([s[k] for s in states])
    return (y_prompt, y_sample,
            stack(pe, 0), stack(se, 0), stack(pe, 1), stack(se, 1),
            stack(pe, 2), stack(se, 2), stack(pe, 3), stack(se, 3),
            stack(po, 0), stack(so, 0))
```

```python
import functools
import math

import numpy as np
import jax
import jax.numpy as jnp
from jax import lax
from jax.experimental import pallas as pl
from jax.experimental.pallas import tpu as pltpu

F32 = jnp.float32
BF16 = jnp.bfloat16

HEAD_DIM = 64
NSA_HEADS = 8
NSA_KV_HEADS = 2
G_NSA = 4
SB_HEADS = 8
SB_KV_HEADS = 4
G_SB = 2
MOBA_HEADS = 16
MOBA_KV_HEADS = 4
G_MOBA = 4
CMP_LEN = 32
CMP_STRIDE = 16
SEL_BLOCK = 64
N_SEL = 16
WINDOW = 512
MOBA_BLOCK = 256
MOBA_TOPK = 3
N_BUCKETS = 32
REL_MAX_DIST = 1024
N_EXPERTS = 8
PAGE_SIZE = 128
SCALE = HEAD_DIM ** -0.5
NEG_INF = -1e30
SEL_BIG = 1e9
EPS = 1e-6

LANES = 128
SUBLANES = 8
NBLK_PAD = LANES
TQ = 256
TK = 256
FLASH_HEADS = 4
SB_TQ = 512
SB_CUT = 110.0
DEC_ROWS = 16
VMEM_LIMIT = 56 * 1024 * 1024


def _bucket_thresholds():
    n = np.arange(0, 2 * REL_MAX_DIST, dtype=np.float64)
    exact = N_BUCKETS // 2
    nf = np.maximum(n, 1.0)
    big = exact + (np.log(nf / exact) / math.log(REL_MAX_DIST / exact) * (N_BUCKETS - exact)).astype(np.int32)
    bucket = np.where(n < exact, n.astype(np.int32), np.minimum(big, N_BUCKETS - 1))
    return tuple(int(np.argmax(bucket >= k)) for k in range(1, N_BUCKETS))


BUCKET_THR = _bucket_thresholds()
SAT_DIST = BUCKET_THR[-1]


def _cparams(*sem):
    return pltpu.CompilerParams(dimension_semantics=sem, vmem_limit_bytes=VMEM_LIMIT)


def _row_tile(m, cap):
    t = cap
    while m % t:
        t //= 2
    return t


def _split2(x):
    hi = x.astype(BF16)
    lo = (x - hi.astype(F32)).astype(BF16)
    return hi, lo


def _split3(x):
    hi = x.astype(BF16)
    r = x - hi.astype(F32)
    mid = r.astype(BF16)
    lo = (r - mid.astype(F32)).astype(BF16)
    return hi, mid, lo


def _dot(a, b):
    return jnp.dot(a, b, preferred_element_type=F32)


def _dot_nt(a, b):
    return lax.dot_general(a, b, (((1,), (1,)), ((), ())), preferred_element_type=F32)


def _rms(x, g):
    return x * lax.rsqrt(jnp.mean(x * x, axis=-1, keepdims=True) + EPS) * g


def _bias_from_dist(dist, tb_ref, h):
    val = jnp.full(dist.shape, tb_ref[0, h], F32)
    for k in range(1, N_BUCKETS):
        val = jnp.where(dist >= BUCKET_THR[k - 1], tb_ref[k, h], val)
    return val


def _left_half():
    return lax.broadcasted_iota(jnp.int32, (1, LANES), 1) < HEAD_DIM


def _half_rms_inv(x, both):
    left = _left_half()
    sq = x * x
    s_left = jnp.sum(jnp.where(left, sq, 0.0), axis=-1, keepdims=True)
    inv_l = lax.rsqrt(s_left * (1.0 / HEAD_DIM) + EPS)
    if both:
        s_all = jnp.sum(sq, axis=-1, keepdims=True)
        inv_r = lax.rsqrt((s_all - s_left) * (1.0 / HEAD_DIM) + EPS)
    else:
        inv_r = 1.0
    return jnp.where(left, inv_l, inv_r)


def _proj_kernel(h_ref, g_ref, wr_ref, cs_ref, wt_ref, gt_ref, *outs, row_modes, row_outs, t_modes, t_outs):
    xn = _rms(h_ref[...], g_ref[...]).astype(BF16)
    y = _dot(xn, wr_ref[...])
    for (c0, c1), o_ref in zip(row_outs, outs[:len(row_outs)]):
        for c in range(c0, c1):
            sl = slice(c * LANES, (c + 1) * LANES)
            x = y[:, sl]
            mode = row_modes[c]
            if mode in ('norm', 'normk'):
                x = x * _half_rms_inv(x, mode == 'norm')
            x = x * cs_ref[:, sl]
            if mode == 'sigmoid':
                x = jax.nn.sigmoid(x)
            o_ref[:, (c - c0) * LANES:(c - c0 + 1) * LANES] = x.astype(o_ref.dtype)
    yt = _dot_nt(wt_ref[...], xn)
    for (r0, r1), o_ref in zip(t_outs, outs[len(row_outs):]):
        for r in range(r0, r1):
            sl = slice(r * HEAD_DIM, (r + 1) * HEAD_DIM)
            x = yt[sl, :]
            if t_modes[r] == 'norm':
                x = x * lax.rsqrt(jnp.mean(x * x, axis=0, keepdims=True) + EPS) * gt_ref[sl, :]
            o_ref[0, (r - r0) * HEAD_DIM:(r - r0 + 1) * HEAD_DIM, :] = x


def _proj(h, b, t, g, w_row, cs, row_modes, row_outs, row_dtypes, w_t, gt, t_modes, t_outs):
    n, d = h.shape
    tm = min(512, t)
    nt = t // tm
    cr, ct = w_row.shape[1], w_t.shape[0]
    row = lambda bi, i: (bi * nt + i, 0)
    const = lambda bi, i: (0, 0)
    out_specs = [pl.BlockSpec((tm, (c1 - c0) * LANES), row) for c0, c1 in row_outs]
    out_shape = [jax.ShapeDtypeStruct((n, (c1 - c0) * LANES), dt) for (c0, c1), dt in zip(row_outs, row_dtypes)]
    out_specs += [pl.BlockSpec((1, (r1 - r0) * HEAD_DIM, tm), lambda bi, i: (bi, 0, i)) for r0, r1 in t_outs]
    out_shape += [jax.ShapeDtypeStruct((b, (r1 - r0) * HEAD_DIM, t), F32) for r0, r1 in t_outs]
    return pl.pallas_call(
        functools.partial(_proj_kernel, row_modes=row_modes, row_outs=row_outs, t_modes=t_modes, t_outs=t_outs),
        grid=(b, nt),
        in_specs=[pl.BlockSpec((tm, d), row),
                  pl.BlockSpec((1, d), const),
                  pl.BlockSpec((d, cr), const),
                  pl.BlockSpec((1, cr), const),
                  pl.BlockSpec((ct, d), const),
                  pl.BlockSpec((ct, 1), const)],
        out_specs=out_specs,
        out_shape=out_shape,
        compiler_params=_cparams("parallel", "parallel"),
        name="proj",
    )(h, g.reshape(1, d), w_row, cs.reshape(1, cr), w_t, gt.reshape(ct, 1))


def _out_proj_kernel(*refs):
    *x_refs, w_ref, r_ref, o_ref = refs
    acc = r_ref[...]
    k0 = 0
    for x_ref in x_refs:
        k = x_ref.shape[1]
        acc = acc + _dot(x_ref[...].astype(BF16), w_ref[k0:k0 + k, :])
        k0 += k
    o_ref[...] = acc


def _out_proj(xs, w, res):
    m, n = res.shape
    tm = _row_tile(m, 512)
    return pl.pallas_call(
        _out_proj_kernel,
        grid=(m // tm,),
        in_specs=[pl.BlockSpec((tm, x.shape[1]), lambda i: (i, 0)) for x in xs]
        + [pl.BlockSpec(w.shape, lambda i: (0, 0)), pl.BlockSpec((tm, n), lambda i: (i, 0))],
        out_specs=pl.BlockSpec((tm, n), lambda i: (i, 0)),
        out_shape=jax.ShapeDtypeStruct((m, n), F32),
        compiler_params=_cparams("parallel"),
        name="out_proj",
    )(*xs, w, res)


def _ffn_kernel(h_ref, g_ref, wg_ref, wu_ref, wd_ref, o_ref, xn_scr):
    f = pl.program_id(1)

    @pl.when(f == 0)
    def _():
        xn_scr[...] = _rms(h_ref[...], g_ref[...]).astype(BF16)
        o_ref[...] = h_ref[...]

    xn = xn_scr[...]
    gate = _dot(xn, wg_ref[...])
    up = _dot(xn, wu_ref[...])
    act = (gate * jax.nn.sigmoid(gate) * up).astype(BF16)
    o_ref[...] += _dot(act, wd_ref[...])


def _ffn(h, g, w_gu, w_down, tf):
    m, d = h.shape
    fdim = w_down.shape[0]
    nf = fdim // tf
    tm = _row_tile(m, 1024)
    return pl.pallas_call(
        _ffn_kernel,
        grid=(m // tm, nf),
        in_specs=[pl.BlockSpec((tm, d), lambda i, f: (i, 0)),
                  pl.BlockSpec((1, d), lambda i, f: (0, 0)),
                  pl.BlockSpec((d, tf), lambda i, f: (0, f)),
                  pl.BlockSpec((d, tf), lambda i, f: (0, f + nf)),
                  pl.BlockSpec((tf, d), lambda i, f: (f, 0))],
        out_specs=pl.BlockSpec((tm, d), lambda i, f: (i, 0)),
        out_shape=jax.ShapeDtypeStruct((m, d), F32),
        scratch_shapes=[pltpu.VMEM((tm, d), BF16)],
        compiler_params=_cparams("parallel", "arbitrary"),
        name="ffn",
    )(h, g.reshape(1, d), w_gu, w_gu, w_down)


def _router_kernel(h_ref, g_ref, w_ref, b_ref, o_ref, ot_ref):
    xn = _rms(h_ref[...], g_ref[...])
    xh, xl = _split2(xn)
    wh, wl = _split2(w_ref[...])
    logits = _dot(xh, wh) + _dot(xh, wl) + _dot(xl, wh) + b_ref[...]
    lane = lax.broadcasted_iota(jnp.int32, logits.shape, 1)
    logits = jnp.where(lane < N_EXPERTS, logits, -jnp.inf)
    v1 = jnp.max(logits, axis=-1, keepdims=True)
    i1 = jnp.min(jnp.where(logits == v1, lane, LANES), axis=-1, keepdims=True)
    rest = jnp.where(lane == i1, -jnp.inf, logits)
    v2 = jnp.max(rest, axis=-1, keepdims=True)
    i2 = jnp.min(jnp.where(rest == v2, lane, LANES), axis=-1, keepdims=True)
    e2 = jnp.exp(v2 - v1)
    w1 = 1.0 / (1.0 + e2)
    w2 = e2 / (1.0 + e2)
    gates = jnp.where(lane == i1, w1, 0.0) + jnp.where(lane == i2, w2, 0.0)
    o_ref[...] = gates
    for c in range(gates.shape[0] // LANES):
        ot_ref[:, c * LANES:(c + 1) * LANES] = gates[c * LANES:(c + 1) * LANES, :].T


def _router(h, g, w_r, b_r):
    m, d = h.shape
    tm = _row_tile(m, 512)
    w_pad = jnp.pad(w_r, ((0, 0), (0, LANES - N_EXPERTS)))
    b_pad = jnp.pad(b_r, (0, LANES - N_EXPERTS)).reshape(1, LANES)
    return pl.pallas_call(
        _router_kernel,
        grid=(m // tm,),
        in_specs=[pl.BlockSpec((tm, d), lambda i: (i, 0)),
                  pl.BlockSpec((1, d), lambda i: (0, 0)),
                  pl.BlockSpec((d, LANES), lambda i: (0, 0)),
                  pl.BlockSpec((1, LANES), lambda i: (0, 0))],
        out_specs=[pl.BlockSpec((tm, LANES), lambda i: (i, 0)), pl.BlockSpec((LANES, tm), lambda i: (0, i))],
        out_shape=[jax.ShapeDtypeStruct((m, LANES), F32), jax.ShapeDtypeStruct((LANES, m), F32)],
        compiler_params=_cparams("parallel"),
        name="router",
    )(h, g.reshape(1, d), w_pad, b_pad)


MOE_CAP = 320


def _moe_kernel(h_ref, g_ref, gate_ref, gate_t_ref, wg_ref, wu_ref, wd_ref, o_ref,
                xn_scr, pos_scr, pos_t_scr, gate_t_scr, posc_scr, gatec_scr, xg_scr, acc_scr, cnt_scr):
    e = pl.program_id(1)
    f = pl.program_id(2)
    tm = h_ref.shape[0]
    cap = xg_scr.shape[1]

    @pl.when((e == 0) & (f == 0))
    def _():
        xn_scr[...] = _rms(h_ref[...], g_ref[...]).astype(BF16)
        o_ref[...] = h_ref[...]
        r = lax.broadcasted_iota(jnp.int32, (tm, tm), 0)
        c = lax.broadcasted_iota(jnp.int32, (tm, tm), 1)
        mask = jnp.where(gate_ref[...] > 0, 1.0, 0.0).astype(BF16)
        pos_scr[...] = _dot(jnp.where(c < r, 1.0, 0.0).astype(BF16), mask)
        mask_t = jnp.where(gate_t_ref[...] > 0, 1.0, 0.0).astype(BF16)
        pos_t = _dot(mask_t, jnp.where(r < c, 1.0, 0.0).astype(BF16))
        for ee in range(N_EXPERTS):
            pos_t_scr[ee] = pos_t[ee:ee + 1, :]
            gate_t_scr[ee] = gate_t_ref[ee:ee + 1, :]

    @pl.when(f == 0)
    def _():
        sel = lax.broadcasted_iota(jnp.int32, (tm, LANES), 1) == e
        posc_scr[...] = jnp.sum(jnp.where(sel, pos_scr[...], 0.0), axis=-1, keepdims=True)
        gatec_scr[...] = jnp.sum(jnp.where(sel, gate_ref[...], 0.0), axis=-1, keepdims=True)
        pos_r = pos_t_scr[e]
        routed = gate_t_scr[e] > 0
        cnt = jnp.sum(jnp.where(routed, 1.0, 0.0)).astype(jnp.int32)
        cnt_scr[0] = cnt

        def gather(ci, carry):
            slot = (ci * cap + lax.broadcasted_iota(jnp.int32, (cap, tm), 0)).astype(F32)
            gmat = jnp.where((pos_r == slot) & routed, 1.0, 0.0).astype(BF16)
            xg_scr[ci] = _dot(gmat, xn_scr[...]).astype(BF16)
            acc_scr[ci] = jnp.zeros(acc_scr.shape[1:], F32)
            return carry

        lax.fori_loop(0, (cnt + cap - 1) // cap, gather, 0)

    nchunk = (cnt_scr[0] + cap - 1) // cap

    def expert(ci, carry):
        x = xg_scr[ci]
        gate = _dot(x, wg_ref[0])
        up = _dot(x, wu_ref[0])
        act = (gate * jax.nn.sigmoid(gate) * up).astype(BF16)
        acc_scr[ci] += _dot(act, wd_ref[0])
        return carry

    lax.fori_loop(0, nchunk, expert, 0)

    @pl.when(f == pl.num_programs(2) - 1)
    def _():
        posc = posc_scr[...]
        gatec = gatec_scr[...]

        def scatter(ci, carry):
            slot = (ci * cap + lax.broadcasted_iota(jnp.int32, (tm, cap), 1)).astype(F32)
            smat = jnp.where((posc == slot) & (gatec > 0), 1.0, 0.0).astype(BF16)
            yh, yl = _split2(acc_scr[ci])
            o_ref[...] += gatec * (_dot(smat, yh) + _dot(smat, yl))
            return carry

        lax.fori_loop(0, nchunk, scatter, 0)


def _moe(h, g, gates, gates_t, w_gu, w_down, tf):
    m, d = h.shape
    ne, fdim, _ = w_down.shape
    nf = fdim // tf
    tm = _row_tile(m, 1024)
    cap = min(MOE_CAP, tm)
    nch = -(-tm // cap)
    return pl.pallas_call(
        _moe_kernel,
        grid=(m // tm, ne, nf),
        in_specs=[pl.BlockSpec((tm, d), lambda i, e, f: (i, 0)),
                  pl.BlockSpec((1, d), lambda i, e, f: (0, 0)),
                  pl.BlockSpec((tm, LANES), lambda i, e, f: (i, 0)),
                  pl.BlockSpec((LANES, tm), lambda i, e, f: (0, i)),
                  pl.BlockSpec((1, d, tf), lambda i, e, f: (e, 0, f)),
                  pl.BlockSpec((1, d, tf), lambda i, e, f: (e, 0, f + nf)),
                  pl.BlockSpec((1, tf, d), lambda i, e, f: (e, f, 0))],
        out_specs=pl.BlockSpec((tm, d), lambda i, e, f: (i, 0)),
        out_shape=jax.ShapeDtypeStruct((m, d), F32),
        scratch_shapes=[pltpu.VMEM((tm, d), BF16), pltpu.VMEM((tm, LANES), F32),
                        pltpu.VMEM((N_EXPERTS, 1, tm), F32), pltpu.VMEM((N_EXPERTS, 1, tm), F32),
                        pltpu.VMEM((tm, 1), F32), pltpu.VMEM((tm, 1), F32),
                        pltpu.VMEM((nch, cap, d), BF16), pltpu.VMEM((nch, cap, d), F32),
                        pltpu.SMEM((1,), jnp.int32)],
        compiler_params=_cparams("parallel", "arbitrary", "arbitrary"),
        name="moe",
    )(h, g.reshape(1, d), gates, gates_t, w_gu, w_gu, w_down)


def _pe_kernel(h_ref, g_ref, p_ref, wg_ref, wi_ref, o_ref):
    h = h_ref[...]
    hn = _rms(h, g_ref[...]).astype(BF16)
    gate = jax.nn.sigmoid(_dot(hn, wg_ref[...]))
    o_ref[...] = h + _dot(p_ref[...].astype(BF16), wi_ref[...]) * gate


def _pe_update(h, g, p, w_gate, w_in):
    m, d = h.shape
    pd = p.shape[1]
    tm = _row_tile(m, 512)
    return pl.pallas_call(
        _pe_kernel,
        grid=(m // tm,),
        in_specs=[pl.BlockSpec((tm, d), lambda i: (i, 0)),
                  pl.BlockSpec((1, d), lambda i: (0, 0)),
                  pl.BlockSpec((tm, pd), lambda i: (i, 0)),
                  pl.BlockSpec((d, d), lambda i: (0, 0)),
                  pl.BlockSpec((pd, d), lambda i: (0, 0))],
        out_specs=pl.BlockSpec((tm, d), lambda i: (i, 0)),
        out_shape=jax.ShapeDtypeStruct((m, d), F32),
        compiler_params=_cparams("parallel"),
        name="pe_update",
    )(h, g.reshape(1, d), p, w_gate, w_in)


def _bias_kernel(tb_ref, qp_ref, kp_ref, o_ref, *, win):
    h = pl.program_id(0)
    kp = kp_ref[...]
    dist = qp_ref[...] - kp
    ok = (kp >= 0) & (dist >= 0)
    if win is not None:
        ok = ok & (dist < win)
    o_ref[0] = jnp.where(ok, _bias_from_dist(dist, tb_ref, h), NEG_INF)


def _bias_table(tb, n_heads, qpos, kpos, win=None):
    r, c = qpos.shape[0], kpos.shape[0]
    rb = _row_tile(r, 256)
    return pl.pallas_call(
        functools.partial(_bias_kernel, win=win),
        grid=(n_heads, r // rb),
        in_specs=[pl.BlockSpec(memory_space=pltpu.SMEM),
                  pl.BlockSpec((rb, 1), lambda h, i: (i, 0)),
                  pl.BlockSpec((1, c), lambda h, i: (0, 0))],
        out_specs=pl.BlockSpec((1, rb, c), lambda h, i: (h, i, 0)),
        out_shape=jax.ShapeDtypeStruct((n_heads, r, c), F32),
        compiler_params=_cparams("parallel", "parallel"),
        name="bias_table",
    )(tb, jnp.asarray(qpos, jnp.int32).reshape(r, 1), jnp.asarray(kpos, jnp.int32).reshape(1, c))


def _tile_bias(tb, n_kv, g, nd, win=None):
    t = _bias_table(tb, n_kv * g, np.arange(nd * TQ), np.arange(TK), win)
    t = t.reshape(n_kv, g, nd, TQ, TK).transpose(0, 2, 1, 3, 4)
    return jnp.concatenate([t, jnp.full((n_kv, 1, g, TQ, TK), NEG_INF, F32)], axis=1)


def _rank_nsel(score, sc_scr, nloop, kth, thresh):
    nb, n = score.shape
    sc_scr[...] = score
    out = []
    for c in range(n // LANES):
        sl = slice(c * LANES, (c + 1) * LANES)
        sc = score[:, sl]
        blk = lax.broadcasted_iota(jnp.int32, sc.shape, 0)

        def body(jg, cnt, sc=sc, blk=blk, sl=sl):
            grp = sc_scr[pl.ds(pl.multiple_of(jg * SUBLANES, SUBLANES), SUBLANES), sl]
            for r in range(SUBLANES):
                row = grp[r:r + 1, :]
                beats = (row > sc) | ((row == sc) & (jg * SUBLANES + r < blk))
                cnt = cnt + jnp.where(beats, 1.0, 0.0)
            return cnt

        cnt = lax.fori_loop(0, (nloop + SUBLANES - 1) // SUBLANES, body, jnp.zeros(sc.shape, F32))
        sel = (cnt < kth) & (sc > thresh)
        out.append(jnp.where(sel, 0.0, 1.0))
    return out


def _pos_term(pos_ref, w1_ref):
    ph, plo = _split2(pos_ref[0])
    wh, wl = _split2(w1_ref[0])
    return (_dot(ph, wh) + _dot(ph, wl) + _dot(plo, wh))[0:1]


def _cmp_mlp_kernel(x_ref, w1a_ref, w1b_ref, pos_ref, w1_ref, w2_ref, g_ref, o_ref, *, norm):
    nb, nc, _ = x_ref.shape
    x = x_ref[...].reshape(nb * nc, x_ref.shape[2])
    a = _dot(x, w1a_ref[0])
    b = _dot(x, w1b_ref[0])
    z = a + pltpu.roll(b, nb * nc - 1, 0) + _pos_term(pos_ref, w1_ref)
    y = _dot(jax.nn.gelu(z).astype(BF16), w2_ref[0])
    if norm:
        y = _rms(y, g_ref[...])
    o_ref[...] = y.reshape(nb, nc, HEAD_DIM)


def _cmp_mlp(x, w1, pos, w2, gain, kv, norm):
    s, nc, cw = x.shape
    nb = _row_tile(s, max(1, 1024 // nc))
    half = CMP_STRIDE * HEAD_DIM
    w1f = w1.reshape(2, CMP_LEN * HEAD_DIM, HEAD_DIM)
    w1a = w1f[:, :half].astype(BF16)
    w1b = w1f[:, half:].astype(BF16)
    posf = jnp.broadcast_to(pos.reshape(2, 1, CMP_LEN * HEAD_DIM), (2, SUBLANES, CMP_LEN * HEAD_DIM))
    return pl.pallas_call(
        functools.partial(_cmp_mlp_kernel, norm=norm),
        grid=(s // nb,),
        in_specs=[pl.BlockSpec((nb, nc, cw), lambda i: (i, 0, 0)),
                  pl.BlockSpec((1, half, HEAD_DIM), lambda i: (kv, 0, 0)),
                  pl.BlockSpec((1, half, HEAD_DIM), lambda i: (kv, 0, 0)),
                  pl.BlockSpec((1, SUBLANES, 2 * half), lambda i: (kv, 0, 0)),
                  pl.BlockSpec((1, 2 * half, HEAD_DIM), lambda i: (kv, 0, 0)),
                  pl.BlockSpec((1, HEAD_DIM, HEAD_DIM), lambda i: (kv, 0, 0)),
                  pl.BlockSpec((1, HEAD_DIM), lambda i: (0, 0))],
        out_specs=pl.BlockSpec((nb, nc, HEAD_DIM), lambda i: (i, 0, 0)),
        out_shape=jax.ShapeDtypeStruct((s, nc, HEAD_DIM), F32),
        compiler_params=_cparams("parallel"),
        name="cmp_mlp",
    )(x, w1a, w1b, posf, w1f, w2.astype(BF16), gain.reshape(1, HEAD_DIM))


def _stack_heads(q_ref, g):
    return jnp.concatenate([q_ref[:, gi * HEAD_DIM:(gi + 1) * HEAD_DIM] for gi in range(g)], axis=0)


def _unstack_heads(o, g, tq):
    return [o[gi * tq:(gi + 1) * tq] for gi in range(g)]


def _nsa_cmp_kernel(tb_ref, q_ref, kc_ref, vc_ref, c2s_ref, gate_ref, o_ref, nsel_ref, s_scr, sc_scr, *, n_kv, g):
    tq = q_ref.shape[0]
    ncp = kc_ref.shape[1]
    r = g * tq
    kvh = pl.program_id(0) % n_kv
    t0 = pl.program_id(1) * tq
    q = _stack_heads(q_ref, g)
    for cc in range(ncp // LANES):
        c0 = cc * LANES
        sl = slice(c0, c0 + LANES)
        min_kp = CMP_STRIDE * c0 + CMP_LEN - 1
        max_kp = CMP_STRIDE * (c0 + LANES - 1) + CMP_LEN - 1
        future = min_kp > t0 + (tq - 1)
        sat = t0 - max_kp >= SAT_DIST

        @pl.when(future)
        def _():
            s_scr[:, sl] = jnp.full((r, LANES), NEG_INF, F32)

        @pl.when(sat)
        def _():
            s = _dot_nt(q, kc_ref[0, sl, :]).reshape(g, tq, LANES)
            for gi in range(g):
                s_scr[gi * tq:(gi + 1) * tq, sl] = s[gi] + tb_ref[N_BUCKETS - 1, kvh * g + gi]

        @pl.when(jnp.logical_not(future | sat))
        def _():
            s = _dot_nt(q, kc_ref[0, sl, :]).reshape(g, tq, LANES)
            t = t0 + lax.broadcasted_iota(jnp.int32, (tq, LANES), 0)
            kp = CMP_STRIDE * (c0 + lax.broadcasted_iota(jnp.int32, (tq, LANES), 1)) + (CMP_LEN - 1)
            dist = t - kp
            for gi in range(g):
                b = _bias_from_dist(dist, tb_ref, kvh * g + gi)
                s_scr[gi * tq:(gi + 1) * tq, sl] = jnp.where(dist >= 0, s[gi] + b, NEG_INF)

    s = s_scr[...]
    m = jnp.max(s, axis=-1, keepdims=True)
    e = jnp.where(s > 0.5 * NEG_INF, jnp.exp(s - m), 0.0)
    p = e / jnp.maximum(jnp.sum(e, axis=-1, keepdims=True), 1e-30)
    o = _unstack_heads(_dot(p.astype(BF16), vc_ref[0]), g, tq)
    o_ref[...] = jnp.concatenate([o[gi] * gate_ref[:, gi:gi + 1] for gi in range(g)], axis=-1)

    psum = jnp.sum(p.reshape(g, tq, ncp), axis=0)
    ph, plo = _split2(psum)
    c2s = c2s_ref[...]
    imp_t = _dot_nt(c2s, ph) + _dot_nt(c2s, plo)
    blk = lax.broadcasted_iota(jnp.int32, (NBLK_PAD, tq), 0)
    t = t0 + lax.broadcasted_iota(jnp.int32, (NBLK_PAD, tq), 1)
    valid = SEL_BLOCK * blk <= t
    forced = (blk == t // SEL_BLOCK) | (blk == 0)
    score = jnp.where(forced, SEL_BIG, jnp.where(valid, imp_t, -SEL_BIG))
    nloop = jnp.minimum((t0 + tq - 1) // SEL_BLOCK + 1, NBLK_PAD)
    cols = _rank_nsel(score, sc_scr, nloop, N_SEL, -jnp.inf)
    for c, nsel_t in enumerate(cols):
        nsel_ref[0, c * LANES:(c + 1) * LANES, :] = nsel_t.T.astype(BF16)


def _nsa_cmp(tb, q, kc, vc, c2s_t, gates, b, t, n_kv, g):
    ncp = kc.shape[1]
    tq = min(TQ, t)
    nq = t // tq
    gw = g * HEAD_DIM
    row = lambda bg, i: ((bg // n_kv) * nq + i, bg % n_kv)
    return pl.pallas_call(
        functools.partial(_nsa_cmp_kernel, n_kv=n_kv, g=g),
        grid=(b * n_kv, nq),
        in_specs=[pl.BlockSpec(memory_space=pltpu.SMEM),
                  pl.BlockSpec((tq, gw), row),
                  pl.BlockSpec((1, ncp, HEAD_DIM), lambda bg, i: (bg, 0, 0)),
                  pl.BlockSpec((1, ncp, HEAD_DIM), lambda bg, i: (bg, 0, 0)),
                  pl.BlockSpec((NBLK_PAD, ncp), lambda bg, i: (0, 0)),
                  pl.BlockSpec((tq, LANES), row)],
        out_specs=[pl.BlockSpec((tq, gw), row),
                   pl.BlockSpec((1, tq, NBLK_PAD), lambda bg, i: (bg, i, 0))],
        out_shape=[jax.ShapeDtypeStruct((b * t, n_kv * gw), F32),
                   jax.ShapeDtypeStruct((b * n_kv, t, NBLK_PAD), BF16)],
        scratch_shapes=[pltpu.VMEM((g * tq, ncp), F32), pltpu.VMEM((NBLK_PAD, tq), F32)],
        compiler_params=_cparams("parallel", "parallel"),
        name="nsa_cmp",
    )(tb, q, kc, vc, c2s_t, gates)


def _flash_kernel(*refs, g, look, aug, shared_nsel, branch):
    it = iter(refs)
    q_ref = next(it)
    nsel_ref = next(it) if aug else None
    kt_ref, vt_ref = next(it), next(it)
    oh_ref = next(it) if aug else None
    b_ref = next(it)
    acc_ref, gate_ref = (next(it), next(it)) if branch is not None else (None, None)
    o_ref, k_scr, v_scr, q_scr, m_scr, acc_scr = it
    tq = q_ref.shape[0]
    nt = k_scr.shape[0]
    nd = b_ref.shape[1]
    r = g * tq
    i = pl.program_id(1)

    @pl.when(i == 0)
    def _():
        for j in range(nt):
            cols = slice(j * TK, (j + 1) * TK)
            if aug:
                k_scr[j, 0:NBLK_PAD, :] = oh_ref[:, cols]
                k_scr[j, NBLK_PAD:NBLK_PAD + HEAD_DIM, :] = kt_ref[0, :, cols].astype(BF16)
            else:
                k_scr[j] = kt_ref[0, :, cols].astype(BF16)
            v_scr[j, 0:HEAD_DIM, :] = vt_ref[0, :, cols].astype(BF16)
            v_scr[j, HEAD_DIM:LANES, :] = jnp.ones((LANES - HEAD_DIM, TK), BF16)

    for gi in range(g):
        rows = slice(gi * tq, (gi + 1) * tq)
        qg = q_ref[:, gi * HEAD_DIM:(gi + 1) * HEAD_DIM]
        if aug:
            q_scr[rows, 0:NBLK_PAD] = nsel_ref[0] if shared_nsel else nsel_ref[:, gi * NBLK_PAD:(gi + 1) * NBLK_PAD]
            q_scr[rows, NBLK_PAD:NBLK_PAD + HEAD_DIM] = qg
        else:
            q_scr[rows, :] = qg
    m_scr[...] = jnp.full((r, LANES), NEG_INF, F32)
    acc_scr[...] = jnp.zeros((r, LANES), F32)
    lo = 0 if look is None else jnp.maximum(i - look, 0)

    def body(jj, carry):
        ja = lo + 2 * jj
        jb = ja + 1
        jb_c = jnp.minimum(jb, nt - 1)
        da = jnp.minimum(i - ja, nd - 2)
        db = jnp.where(jb > i, nd - 1, jnp.minimum(i - jb, nd - 2))
        ka, kb, va, vb = k_scr[ja], k_scr[jb_c], v_scr[ja], v_scr[jb_c]
        for gs in range(0, g, FLASH_HEADS):
            rows = slice(gs * tq, (gs + FLASH_HEADS) * tq)
            rr = FLASH_HEADS * tq
            q = q_scr[rows, :]
            sa = (_dot(q, ka).reshape(FLASH_HEADS, tq, TK) + b_ref[0, da, gs:gs + FLASH_HEADS]).reshape(rr, TK)
            sb = (_dot(q, kb).reshape(FLASH_HEADS, tq, TK) + b_ref[0, db, gs:gs + FLASH_HEADS]).reshape(rr, TK)
            m_prev = m_scr[rows, :]
            m_cur = jnp.maximum(jnp.max(sa, axis=-1, keepdims=True), jnp.max(sb, axis=-1, keepdims=True))
            m_new = jnp.maximum(m_prev, m_cur)
            alpha = jnp.exp(m_prev - m_new)
            m_rep = jnp.concatenate([m_new] * (TK // LANES), axis=-1)
            pa = jnp.exp(sa - m_rep).astype(BF16)
            pb = jnp.exp(sb - m_rep).astype(BF16)
            acc_scr[rows, :] = alpha * acc_scr[rows, :] + _dot_nt(pa, va) + _dot_nt(pb, vb)
            m_scr[rows, :] = m_new
        return carry

    lax.fori_loop(0, (i + 2 - lo) // 2, body, 0)
    acc = acc_scr[...]
    o = _unstack_heads(acc[:, :HEAD_DIM] / jnp.maximum(acc[:, HEAD_DIM:], 1e-30), g, tq)
    if branch is not None:
        o = [acc_ref[:, gi * HEAD_DIM:(gi + 1) * HEAD_DIM] + gate_ref[:, branch * g + gi:branch * g + gi + 1] * o[gi]
             for gi in range(g)]
    o_ref[...] = jnp.concatenate(o, axis=-1)


def _flash_attn(q, kv_t, k_blk, v_blk, bias, b, t, n_kv, g, nsel=None, onehot_t=None, look=None,
                acc=None, gates=None, branch=None):
    aug = nsel is not None
    shared = aug and nsel.ndim == 3
    nq = t // TQ
    gw = g * HEAD_DIM
    dk = HEAD_DIM + (NBLK_PAD if aug else 0)
    row = lambda bg, i: ((bg // n_kv) * nq + i, bg % n_kv)
    in_specs = [pl.BlockSpec((TQ, gw), row)]
    args = [q]
    if aug:
        in_specs.append(pl.BlockSpec((1, TQ, NBLK_PAD), lambda bg, i: (bg, i, 0)) if shared
                        else pl.BlockSpec((TQ, g * NBLK_PAD), row))
        args.append(nsel)
    in_specs += [pl.BlockSpec((1, HEAD_DIM, t), lambda bg, i: (bg // n_kv, k_blk + bg % n_kv, 0)),
                 pl.BlockSpec((1, HEAD_DIM, t), lambda bg, i: (bg // n_kv, v_blk + bg % n_kv, 0))]
    args += [kv_t, kv_t]
    if aug:
        in_specs.append(pl.BlockSpec((NBLK_PAD, t), lambda bg, i: (0, 0)))
        args.append(onehot_t)
    nd = bias.shape[1]
    in_specs.append(pl.BlockSpec((1, nd, g, TQ, TK), lambda bg, i: (bg % n_kv, 0, 0, 0, 0)))
    args.append(bias)
    if branch is not None:
        in_specs += [pl.BlockSpec((TQ, gw), row), pl.BlockSpec((TQ, LANES), row)]
        args += [acc, gates]
    return pl.pallas_call(
        functools.partial(_flash_kernel, g=g, look=look, aug=aug, shared_nsel=shared, branch=branch),
        grid=(b * n_kv, nq),
        in_specs=in_specs,
        out_specs=pl.BlockSpec((TQ, gw), row),
        out_shape=jax.ShapeDtypeStruct((b * t, n_kv * gw), F32),
        scratch_shapes=[pltpu.VMEM((t // TK, dk, TK), BF16), pltpu.VMEM((t // TK, LANES, TK), BF16),
                        pltpu.VMEM((g * TQ, dk), BF16), pltpu.VMEM((g * TQ, LANES), F32),
                        pltpu.VMEM((g * TQ, LANES), F32)],
        compiler_params=_cparams("parallel", "arbitrary"),
        name="flash_attn",
    )(*args)


def _sb_kernel(q_ref, kt_ref, vt_ref, u_ref, o_ref, k_scr, v_scr, carry_scr, acc_scr, *, g):
    tq = q_ref.shape[0]
    nt = k_scr.shape[0]
    r = g * tq
    i = pl.program_id(1)

    @pl.when(i == 0)
    def _():
        for j in range(nt):
            cols = slice(j * TK, (j + 1) * TK)
            k_scr[j] = kt_ref[0, :, cols].astype(BF16)
            v_scr[j] = vt_ref[0, :, cols].astype(BF16)

    q = _stack_heads(q_ref, g)
    carry_scr[...] = jnp.zeros((r, LANES), F32)
    acc_scr[...] = jnp.zeros((r, HEAD_DIM), F32)
    t = i * tq + lax.broadcasted_iota(jnp.int32, (g, tq, TK), 1).reshape(r, TK)
    lane = lax.broadcasted_iota(jnp.int32, (r, TK), 1)
    u = u_ref[...]

    def cond(st):
        return (st[0] >= 0) & (st[1] > 0)

    def body(st):
        j = st[0]
        z = _dot(q, k_scr[j])
        before = (j * TK + lane) < t
        sp = jnp.maximum(z, 0.0) + jnp.log1p(jnp.exp(-jnp.abs(z)))
        lk = jnp.where(before, -sp, 0.0)
        hi, lo = _split2(lk)
        carry = carry_scr[...]
        between = _dot(hi, u) + _dot(lo, u) + jnp.concatenate([carry] * (TK // LANES), axis=-1)
        a = jnp.where(before, jnp.exp(z - sp + between), 0.0)
        acc_scr[...] += _dot_nt(a.astype(BF16), v_scr[j])
        carry = carry + jnp.sum(lk, axis=-1, keepdims=True)
        carry_scr[...] = carry
        return j - 1, (jnp.max(carry) > -SB_CUT).astype(jnp.int32)

    lax.while_loop(cond, body, (((i + 1) * tq - 1) // TK, jnp.int32(1)))
    o_ref[...] = jnp.concatenate(_unstack_heads(acc_scr[...], g, tq), axis=-1)


def _sb_attn(q, kv_t, b, t, n_kv, g):
    tq = min(SB_TQ, t)
    nq = t // tq
    gw = g * HEAD_DIM
    idx = np.arange(TK)
    u = jnp.asarray(idx[:, None] > idx[None, :], BF16)
    row = lambda bg, i: ((bg // n_kv) * nq + i, bg % n_kv)
    return pl.pallas_call(
        functools.partial(_sb_kernel, g=g),
        grid=(b * n_kv, nq),
        in_specs=[pl.BlockSpec((tq, gw), row),
                  pl.BlockSpec((1, HEAD_DIM, t), lambda bg, i: (bg // n_kv, bg % n_kv, 0)),
                  pl.BlockSpec((1, HEAD_DIM, t), lambda bg, i: (bg // n_kv, n_kv + bg % n_kv, 0)),
                  pl.BlockSpec((TK, TK), lambda bg, i: (0, 0))],
        out_specs=pl.BlockSpec((tq, gw), row),
        out_shape=jax.ShapeDtypeStruct((b * t, n_kv * gw), F32),
        scratch_shapes=[pltpu.VMEM((t // TK, HEAD_DIM, TK), BF16), pltpu.VMEM((t // TK, HEAD_DIM, TK), BF16),
                        pltpu.VMEM((g * tq, LANES), F32), pltpu.VMEM((g * tq, HEAD_DIM), F32)],
        compiler_params=_cparams("parallel", "arbitrary"),
        name="sb_attn",
    )(q, kv_t, kv_t, u)


def _moba_gate_kernel(q_ref, kt_ref, a_ref, nsel_ref, km_scr, sc_scr, *, g):
    tq = q_ref.shape[0]
    i = pl.program_id(1)

    @pl.when(i == 0)
    def _():
        a = a_ref[...]
        h1, h2, h3 = _split3(kt_ref[0])
        km_t = _dot(h1, a) + _dot(h2, a) + _dot(h3, a)
        km_sq = jnp.concatenate([km_t, jnp.zeros_like(km_t)], axis=0).T
        km_scr[...] = km_sq[:, :HEAD_DIM]

    nb = sc_scr.shape[0]
    kh, kl = _split2(km_scr[0:nb, :])
    blk = lax.broadcasted_iota(jnp.int32, (nb, tq), 0)
    own = (i * tq + lax.broadcasted_iota(jnp.int32, (nb, tq), 1)) // MOBA_BLOCK
    nloop = jnp.minimum((i * tq + tq - 1) // MOBA_BLOCK + 1, nb)
    pad_rows = jnp.ones((NBLK_PAD - nb, LANES), F32)
    for gi in range(g):
        qg = q_ref[:, gi * HEAD_DIM:(gi + 1) * HEAD_DIM]
        gate_t = (_dot_nt(kh, qg) + _dot_nt(kl, qg)) * (1.0 / SCALE)
        score = jnp.where(blk == own, SEL_BIG, jnp.where(blk < own, gate_t, -SEL_BIG))
        cols = _rank_nsel(score, sc_scr, nloop, MOBA_TOPK + 1, -SEL_BIG / 2)
        for c, nsel_t in enumerate(cols):
            full = jnp.concatenate([nsel_t, pad_rows], axis=0) if nb < NBLK_PAD else nsel_t
            nsel_ref[c * LANES:(c + 1) * LANES, gi * NBLK_PAD:(gi + 1) * NBLK_PAD] = full.T.astype(BF16)


def _moba_gate(q, kv_t, b, t, n_kv, g):
    tq = min(TQ, t)
    nq = t // tq
    avg = (np.arange(t)[:, None] // MOBA_BLOCK) == np.arange(NBLK_PAD)[None, :]
    avg = jnp.asarray(np.where(avg, 1.0 / MOBA_BLOCK, 0.0), BF16)
    row = lambda bg, i: ((bg // n_kv) * nq + i, bg % n_kv)
    return pl.pallas_call(
        functools.partial(_moba_gate_kernel, g=g),
        grid=(b * n_kv, nq),
        in_specs=[pl.BlockSpec((tq, g * HEAD_DIM), row),
                  pl.BlockSpec((1, HEAD_DIM, t), lambda bg, i: (bg // n_kv, bg % n_kv, 0)),
                  pl.BlockSpec((t, NBLK_PAD), lambda bg, i: (0, 0))],
        out_specs=pl.BlockSpec((tq, g * NBLK_PAD), row),
        out_shape=jax.ShapeDtypeStruct((b * t, n_kv * g * NBLK_PAD), BF16),
        scratch_shapes=[pltpu.VMEM((NBLK_PAD, HEAD_DIM), F32),
                        pltpu.VMEM((-(-(t // MOBA_BLOCK) // SUBLANES) * SUBLANES, tq), F32)],
        compiler_params=_cparams("parallel", "arbitrary"),
        name="moba_gate",
    )(q, kv_t, avg)


def _block_onehot_neg_t(t, block):
    oh = np.arange(NBLK_PAD)[:, None] == (np.arange(t)[None, :] // block)
    return jnp.asarray(np.where(oh, NEG_INF, 0.0), BF16)


def _cmp_to_sel_t(ncp, nc, nsb):
    cs = CMP_STRIDE * np.arange(ncp)[None, :]
    ss = SEL_BLOCK * np.arange(NBLK_PAD)[:, None]
    ov = np.clip(np.minimum(cs + CMP_LEN, ss + SEL_BLOCK) - np.maximum(cs, ss), 0, None) / CMP_LEN
    ov = np.where((np.arange(ncp)[None, :] < nc) & (np.arange(NBLK_PAD)[:, None] < nsb), ov, 0.0)
    return jnp.asarray(ov, BF16)


def _rows_from_t(x_t, b, t, n_heads):
    return x_t.reshape(b, 2, n_heads, HEAD_DIM, t).transpose(0, 4, 1, 2, 3)


def _gate_cols(w_g):
    d = w_g.shape[0]
    w = w_g.reshape(d, NSA_KV_HEADS, G_NSA, 3).transpose(0, 1, 3, 2).reshape(d, NSA_KV_HEADS, 3 * G_NSA)
    return jnp.pad(w, ((0, 0), (0, 0), (0, LANES - 3 * G_NSA))).reshape(d, NSA_KV_HEADS * LANES)


_E_COLS = dict(q=(0, 512), kv=(512, 1280), g=(1280, 1304), qs=(1304, 1816), sb=(1816, 2328))
_E_T_MODES = ('raw',) * 4 + ('norm',) * 2 + ('raw',) * 2 + ('norm',) * 2 + ('raw',) * 2 + ('raw',) * 8
_E_T_OUTS = ((0, 4), (4, 8), (8, 12), (12, 20))


def _even_t_weights(w_in, k_gain):
    cut = lambda k: w_in[:, _E_COLS[k][0]:_E_COLS[k][1]]
    w_t = jnp.concatenate([cut('kv'), cut('sb')], axis=1).T.astype(BF16)
    one = jnp.ones((HEAD_DIM,), F32)
    gt = jnp.concatenate([one] * 4 + [k_gain[1]] * 2 + [one] * 2 + [k_gain[2]] * 2 + [one] * 10)
    return w_t, gt


def _pair_cols(w_k, w_v, n_heads):
    d = w_k.shape[0]
    return jnp.concatenate([w_k.reshape(d, n_heads, HEAD_DIM), w_v.reshape(d, n_heads, HEAD_DIM)], axis=2).reshape(d, n_heads * LANES)


def _even_prompt(h, b, t, P, j, li):
    w_in = P['w_in_even'][j]
    q_gain, k_gain = P['nsa_q_norm'][j], P['nsa_k_norm'][j]
    cut = lambda k: w_in[:, _E_COLS[k][0]:_E_COLS[k][1]]
    w_row = jnp.concatenate([cut('q'), cut('qs'), _gate_cols(cut('g')), cut('kv')[:, :256]], axis=1).astype(BF16)
    cs = jnp.concatenate([jnp.tile(q_gain, NSA_HEADS) * SCALE, jnp.full((512,), SCALE, F32), jnp.ones((512,), F32)])
    row_modes = ('norm',) * 4 + ('scale',) * 4 + ('sigmoid',) * 2 + ('scale',) * 2
    w_t, gt = _even_t_weights(w_in, k_gain)
    q, qs, gates, kcvc, cmp_t, slc_t, win_t, sb_t = _proj(
        h, b, t, P['norm_mix'][li], w_row, cs, row_modes, ((0, 4), (4, 8), (8, 10), (10, 12)),
        (BF16, BF16, F32, BF16), w_t, gt, _E_T_MODES, _E_T_OUTS)
    tb = P['rel_bias_table']
    hkv, g = NSA_KV_HEADS, G_NSA

    ncp = t // CMP_STRIDE
    x4 = kcvc.reshape(b, ncp, CMP_STRIDE, 2 * hkv, HEAD_DIM).transpose(0, 3, 1, 2, 4).reshape(b, 2, hkv, ncp, CMP_STRIDE * HEAD_DIM)
    w1, pos, w2 = P['nsa_cmp_w1'][j], P['nsa_cmp_pos'][j], P['nsa_cmp_w2'][j]
    kc = _cmp_mlp(x4[:, 0].reshape(b * hkv, ncp, -1), w1, pos, w2, k_gain[0], 0, True).astype(BF16)
    vc = _cmp_mlp(x4[:, 1].reshape(b * hkv, ncp, -1), w1, pos, w2, k_gain[0], 1, False).astype(BF16)
    o, nsel = _nsa_cmp(tb, q, kc, vc, _cmp_to_sel_t(ncp, ncp - 1, t // SEL_BLOCK), gates, b, t, hkv, g)

    nd_full = -(-(SAT_DIST + TK - 1) // TQ) + 1
    o = _flash_attn(q, slc_t, 0, hkv, _tile_bias(tb, hkv, g, nd_full), b, t, hkv, g, nsel=nsel,
                    onehot_t=_block_onehot_neg_t(t, SEL_BLOCK), acc=o, gates=gates, branch=1)
    look = (WINDOW - 1 + TK - 1) // TK
    o = _flash_attn(q, win_t, 0, hkv, _tile_bias(tb, hkv, g, look + 1, win=WINDOW), b, t, hkv, g, look=look,
                    acc=o, gates=gates, branch=2)

    o_b = _sb_attn(qs, sb_t, b, t, SB_KV_HEADS, G_SB)
    h = _out_proj([o, o_b], P['w_out_even'][j].astype(BF16), h)
    wb = min(WINDOW, t)
    states = (_rows_from_t(cmp_t, b, t, hkv), _rows_from_t(slc_t, b, t, hkv),
              _rows_from_t(win_t[:, :, t - wb:], b, wb, hkv), _rows_from_t(sb_t, b, t, SB_KV_HEADS))
    return h, states


def _odd_prompt(h, b, t, P, j, li):
    hkv, g = MOBA_KV_HEADS, G_MOBA
    w_in = P['w_in_odd'][j]
    w_row = w_in[:, :1024].astype(BF16)
    cs = jnp.tile(P['moba_q_norm'][j], MOBA_HEADS) * SCALE
    w_t = w_in[:, 1024:].T.astype(BF16)
    gt = jnp.concatenate([jnp.tile(P['moba_k_norm'][j], hkv), jnp.ones((hkv * HEAD_DIM,), F32)])
    q, kv_t = _proj(h, b, t, P['norm_mix'][li], w_row, cs, ('norm',) * 8, ((0, 8),), (BF16,),
                    w_t, gt, ('norm',) * 4 + ('raw',) * 4, ((0, 8),))
    tb = P['rel_bias_table']
    nsel = _moba_gate(q, kv_t, b, t, hkv, g)
    nd_full = -(-(SAT_DIST + TK - 1) // TQ) + 1
    o = _flash_attn(q, kv_t, 0, hkv, _tile_bias(tb, hkv, g, nd_full), b, t, hkv, g, nsel=nsel,
                    onehot_t=_block_onehot_neg_t(t, MOBA_BLOCK))
    h = _out_proj([o], P['w_out_odd'][j].astype(BF16), h)
    return h, (_rows_from_t(kv_t, b, t, hkv),)


def _ffn_and_pe(h, pemb, P, li):
    j = li // 2
    if li % 2 == 0:
        h = _ffn(h, P['norm_ffn'][li], P['w_ffn_gu'][j].astype(BF16), P['w_ffn_down'][j].astype(BF16), 256)
    else:
        gates, gates_t = _router(h, P['norm_ffn'][li], P['w_router'][j], P['b_router'][j])
        h = _moe(h, P['norm_ffn'][li], gates, gates_t, P['w_moe_gu'][j].astype(BF16), P['w_moe_down'][j].astype(BF16), 512)
    return _pe_update(h, P['norm_pe'][li], pemb, P['w_pe_gate'][li].astype(BF16), P['w_pe_in'][li].astype(BF16))


def _trunk_prompt(x, p, P):
    b, t, d = x.shape
    h = x.reshape(b * t, d)
    even_states, odd_states = [], []
    for li in range(p.shape[0]):
        j = li // 2
        if li % 2 == 0:
            h, st = _even_prompt(h, b, t, P, j, li)
            even_states.append(st)
        else:
            h, st = _odd_prompt(h, b, t, P, j, li)
            odd_states.append(st)
        h = _ffn_and_pe(h, p[li].reshape(b * t, -1), P, li)
    return h.reshape(b, t, d), even_states, odd_states


def _kv_rows(x_ref, h):
    kv_t = jnp.concatenate([x_ref[0, h], x_ref[1, h]], axis=0)
    n = kv_t.shape[1]
    return jnp.concatenate([kv_t[:, c * LANES:(c + 1) * LANES].T for c in range(n // LANES)], axis=0)


def _gather_kernel(pt_ref, *refs):
    o_ref = refs[-1]
    n_heads = o_ref.shape[1]
    for p, x_ref in enumerate(refs[:-1]):
        for h in range(n_heads):
            o_ref[0, h, p * PAGE_SIZE:(p + 1) * PAGE_SIZE, :] = _kv_rows(x_ref, h)


def _gather_kv(pool_t, layer, page_table):
    db, npg = page_table.shape
    _, _, two, hh, dd, ps = pool_t.shape

    def in_map(p):
        return lambda b, pt: (layer, pt[b * npg + p], 0, 0, 0, 0)

    grid_spec = pltpu.PrefetchScalarGridSpec(
        num_scalar_prefetch=1,
        grid=(db,),
        in_specs=[pl.BlockSpec((None, None, two, hh, dd, ps), in_map(p)) for p in range(npg)],
        out_specs=pl.BlockSpec((1, hh, npg * ps, LANES), lambda b, pt: (b, 0, 0, 0)),
    )
    return pl.pallas_call(
        _gather_kernel,
        grid_spec=grid_spec,
        out_shape=jax.ShapeDtypeStruct((db, hh, npg * ps, LANES), F32),
        compiler_params=_cparams("parallel"),
        name="gather_kv",
    )(page_table.reshape(-1), *([pool_t] * npg))


def _win_rows_kernel(x_ref, o_ref):
    for h in range(o_ref.shape[1]):
        o_ref[0, h] = _kv_rows(x_ref, h)


def _win_rows(win_t, layer):
    _, db, two, hh, dd, w = win_t.shape
    return pl.pallas_call(
        _win_rows_kernel,
        grid=(db,),
        in_specs=[pl.BlockSpec((None, None, two, hh, dd, w), lambda b: (layer, b, 0, 0, 0, 0))],
        out_specs=pl.BlockSpec((1, hh, w, LANES), lambda b: (b, 0, 0, 0)),
        out_shape=jax.ShapeDtypeStruct((db, hh, w, LANES), F32),
        compiler_params=_cparams("parallel"),
        name="win_rows",
    )(win_t)


def _win_shift_kernel(x_ref, new_ref, o_ref):
    x = x_ref[0]
    lane = lax.broadcasted_iota(jnp.int32, x.shape, 1)
    o_ref[0] = jnp.where(lane == x.shape[1] - 1, new_ref[0], pltpu.roll(x, x.shape[1] - 1, 1))


def _win_shift(win_t, layer, new_col):
    _, db, two, hh, dd, w = win_t.shape
    rows = two * hh * dd
    x = win_t.reshape(win_t.shape[0], db, rows, w)
    out = pl.pallas_call(
        _win_shift_kernel,
        grid=(db,),
        in_specs=[pl.BlockSpec((None, 1, rows, w), lambda b: (layer, b, 0, 0)),
                  pl.BlockSpec((1, rows, 1), lambda b: (b, 0, 0))],
        out_specs=pl.BlockSpec((1, rows, w), lambda b: (b, 0, 0)),
        out_shape=jax.ShapeDtypeStruct((db, rows, w), F32),
        compiler_params=_cparams("parallel"),
        name="win_shift",
    )(x, new_col)
    return out.reshape(db, two, hh, dd, w)


def _dec_cmp_mlp_kernel(kv_ref, w1_ref, pos_ref, w1f_ref, w2_ref, g_ref, o_ref):
    n_heads, rows = kv_ref.shape[1], kv_ref.shape[2]
    nc = rows // CMP_STRIDE
    posb = jnp.concatenate([_pos_term(pos_ref.at[0:1], w1f_ref.at[0:1]), _pos_term(pos_ref.at[1:2], w1f_ref.at[1:2])], axis=-1)
    left = _left_half()
    for h in range(n_heads):
        acc = jnp.zeros((nc, 2 * LANES), F32)
        for l in range(CMP_STRIDE):
            xl = kv_ref[0, h, pl.ds(l, nc, stride=CMP_STRIDE), :].astype(BF16)
            acc = acc + _dot(xl, w1_ref[l])
        z = acc[:, :LANES] + pltpu.roll(acc[:, LANES:], nc - 1, 0) + posb
        y = _dot(jax.nn.gelu(z).astype(BF16), w2_ref[...])
        y = y * jnp.where(left, _half_rms_inv(y, False) * g_ref[...], 1.0)
        o_ref[0, h] = y


def _dec_cmp_mlp(kv, w1, pos, w2, gain):
    db, hh, rows, _ = kv.shape
    nc = rows // CMP_STRIDE
    z64 = jnp.zeros((CMP_LEN, HEAD_DIM, HEAD_DIM), F32)
    wd = jnp.concatenate([jnp.concatenate([w1[0], z64], axis=2), jnp.concatenate([z64, w1[1]], axis=2)], axis=1)
    w1d = jnp.concatenate([wd[:CMP_STRIDE], wd[CMP_STRIDE:]], axis=2).astype(BF16)
    z2 = jnp.zeros((HEAD_DIM, HEAD_DIM), F32)
    w2d = jnp.concatenate([jnp.concatenate([w2[0], z2], axis=1), jnp.concatenate([z2, w2[1]], axis=1)], axis=0).astype(BF16)
    w1f = w1.reshape(2, CMP_LEN * HEAD_DIM, HEAD_DIM)
    posf = jnp.broadcast_to(pos.reshape(2, 1, CMP_LEN * HEAD_DIM), (2, SUBLANES, CMP_LEN * HEAD_DIM))
    gpad = jnp.concatenate([gain, jnp.ones((HEAD_DIM,), F32)]).reshape(1, LANES)
    return pl.pallas_call(
        _dec_cmp_mlp_kernel,
        grid=(db,),
        in_specs=[pl.BlockSpec((1, hh, rows, LANES), lambda b: (b, 0, 0, 0)),
                  pl.BlockSpec(w1d.shape, lambda b: (0, 0, 0)),
                  pl.BlockSpec(posf.shape, lambda b: (0, 0, 0)),
                  pl.BlockSpec(w1f.shape, lambda b: (0, 0, 0)),
                  pl.BlockSpec(w2d.shape, lambda b: (0, 0)),
                  pl.BlockSpec((1, LANES), lambda b: (0, 0))],
        out_specs=pl.BlockSpec((1, hh, nc, LANES), lambda b: (b, 0, 0, 0)),
        out_shape=jax.ShapeDtypeStruct((db, hh, nc, LANES), F32),
        compiler_params=_cparams("parallel"),
        name="dec_cmp_mlp",
    )(kv, w1d, posf, w1f, w2d, gpad)


def _dec_call(body, kv, in_specs, args, out_specs, out_shape, name):
    if not isinstance(kv, tuple):
        db, hh, rows, _ = kv.shape
        return pl.pallas_call(
            body, grid=(db,),
            in_specs=[pl.BlockSpec((1, hh, rows, LANES), lambda b: (b, 0, 0, 0))] + in_specs,
            out_specs=out_specs, out_shape=out_shape, compiler_params=_cparams("parallel"), name=name,
        )(kv, *args)
    pool_t, layer, page_table = kv
    db, npg = page_table.shape
    _, _, two, hh, dd, ps = pool_t.shape

    def paged_body(pt_ref, *refs):
        kv_scr = refs[-1]
        for p in range(npg):
            for h in range(hh):
                kv_scr[0, h, p * ps:(p + 1) * ps, :] = _kv_rows(refs[p], h).astype(BF16)
        body(kv_scr, *refs[npg:-1])

    def with_pt(spec):
        return pl.BlockSpec(spec.block_shape, lambda b, pt, f=spec.index_map: f(b))

    def page_map(p):
        return lambda b, pt: (layer, pt[b * npg + p], 0, 0, 0, 0)

    grid_spec = pltpu.PrefetchScalarGridSpec(
        num_scalar_prefetch=1,
        grid=(db,),
        in_specs=[pl.BlockSpec((None, None, two, hh, dd, ps), page_map(p)) for p in range(npg)] + [with_pt(s) for s in in_specs],
        out_specs=[with_pt(s) for s in out_specs],
        scratch_shapes=[pltpu.VMEM((1, hh, npg * ps, LANES), BF16)],
    )
    return pl.pallas_call(
        paged_body, grid_spec=grid_spec, out_shape=out_shape, compiler_params=_cparams("parallel"), name=name + "_paged",
    )(page_table.reshape(-1), *([pool_t] * npg), *args)


def _dec_scores(q, kv_ref, new_ref, h):
    kvb = kv_ref[0, h].astype(BF16)
    s = _dot_nt(q, kvb)
    if new_ref is None:
        return s, kvb, None
    nb = new_ref[0, h].astype(BF16)
    return jnp.concatenate([s, _dot_nt(q, nb)], axis=-1), kvb, nb


def _dec_softmax_out(s, kvb, nb):
    m = jnp.max(s, axis=-1, keepdims=True)
    e = jnp.where(s > 0.5 * NEG_INF, jnp.exp(s - m), 0.0)
    p = e / jnp.maximum(jnp.sum(e, axis=-1, keepdims=True), 1e-30)
    pb = p.astype(BF16)
    nm = kvb.shape[0]
    o = _dot(pb[:, :nm], kvb)
    if nb is not None:
        o = o + _dot(pb[:, nm:], nb)
    return p, o


def _dec_attn_kernel(*refs, g, has_new, aug, gated, imp):
    it = iter(refs)
    kv_ref, q_ref = next(it), next(it)
    new_ref = next(it) if has_new else None
    b_ref = next(it)
    qa_ref, oh_ref = (next(it), next(it)) if aug else (None, None)
    acc_ref, gate_ref = (next(it), next(it)) if gated else (None, None)
    c2s_ref = next(it) if imp else None
    o_ref = next(it)
    imp_ref = next(it) if imp else None
    for h in range(q_ref.shape[1]):
        s, kvb, nb = _dec_scores(q_ref[0, h], kv_ref, new_ref, h)
        s = s + b_ref[h]
        if aug:
            s = s + _dot(qa_ref[0, h], oh_ref[...])
        p, o = _dec_softmax_out(s, kvb, nb)
        if gated:
            o = acc_ref[0, h] + gate_ref[0, h] * o
        o_ref[0, h] = o
        if imp:
            row = lax.broadcasted_iota(jnp.int32, p.shape, 0)
            psum = jnp.sum(jnp.where(row < g, p, 0.0), axis=0, keepdims=True)
            ph, plo = _split2(jnp.broadcast_to(psum, p.shape))
            imp_ref[0, h] = _dot_nt(ph, c2s_ref[...]) + _dot_nt(plo, c2s_ref[...])


def _dec_attn(q, kv, new, bias, g, qa=None, onehot_t=None, acc=None, gate=None, c2s_t=None):
    db, n_kv, _, _ = q.shape
    tp = bias.shape[2]
    has_new, aug, gated, imp = new is not None, qa is not None, acc is not None, c2s_t is not None
    row4 = lambda w: pl.BlockSpec((1, n_kv, DEC_ROWS, w), lambda b: (b, 0, 0, 0))
    in_specs = [row4(LANES)]
    args = [q]
    if has_new:
        in_specs.append(pl.BlockSpec((1, n_kv, LANES, LANES), lambda b: (b, 0, 0, 0)))
        args.append(new)
    in_specs.append(pl.BlockSpec((n_kv, DEC_ROWS, tp), lambda b: (0, 0, 0)))
    args.append(bias)
    if aug:
        in_specs += [row4(NBLK_PAD), pl.BlockSpec((NBLK_PAD, tp), lambda b: (0, 0))]
        args += [qa, onehot_t]
    if gated:
        in_specs += [row4(LANES), row4(1)]
        args += [acc, gate]
    if imp:
        in_specs += [pl.BlockSpec((NBLK_PAD, tp), lambda b: (0, 0))]
        args += [c2s_t]
    out_specs = [row4(LANES)]
    out_shape = [jax.ShapeDtypeStruct((db, n_kv, DEC_ROWS, LANES), F32)]
    if imp:
        out_specs.append(row4(NBLK_PAD))
        out_shape.append(jax.ShapeDtypeStruct((db, n_kv, DEC_ROWS, NBLK_PAD), F32))
    res = _dec_call(functools.partial(_dec_attn_kernel, g=g, has_new=has_new, aug=aug, gated=gated, imp=imp),
                    kv, in_specs, args, out_specs, out_shape, "dec_attn")
    return res if imp else res[0]


def _dec_moba_kernel(kv_ref, q_ref, new_ref, b_ref, oh_ref, o_ref, *, own):
    lane = lax.broadcasted_iota(jnp.int32, (DEC_ROWS, NBLK_PAD), 1)
    for h in range(q_ref.shape[1]):
        s, kvb, nb = _dec_scores(q_ref[0, h], kv_ref, new_ref, h)
        gate = jnp.full((DEC_ROWS, NBLK_PAD), -SEL_BIG, F32)
        for n in range(own):
            gn = jnp.sum(s[:, n * MOBA_BLOCK:(n + 1) * MOBA_BLOCK], axis=-1, keepdims=True) * (1.0 / (SCALE * MOBA_BLOCK))
            gate = jnp.where(lane == n, gn, gate)
        score = jnp.where(lane == own, SEL_BIG, gate)
        cnt = jnp.zeros(score.shape, F32)
        for n in range(own + 1):
            col = score[:, n:n + 1]
            cnt = cnt + jnp.where((col > score) | ((col == score) & (n < lane)), 1.0, 0.0)
        sel = (cnt < MOBA_TOPK + 1) & (score > -SEL_BIG / 2)
        nsel = jnp.where(sel, 0.0, 1.0).astype(BF16)
        s = s + b_ref[h] + _dot(nsel, oh_ref[...])
        _, o = _dec_softmax_out(s, kvb, nb)
        o_ref[0, h] = o


def _dec_moba(q, kv, new, bias, onehot_t, past):
    db, n_kv, _, _ = q.shape
    tp = bias.shape[2]
    row4 = pl.BlockSpec((1, n_kv, DEC_ROWS, LANES), lambda b: (b, 0, 0, 0))
    in_specs = [row4,
                pl.BlockSpec((1, n_kv, LANES, LANES), lambda b: (b, 0, 0, 0)),
                pl.BlockSpec((n_kv, DEC_ROWS, tp), lambda b: (0, 0, 0)),
                pl.BlockSpec((NBLK_PAD, tp), lambda b: (0, 0))]
    return _dec_call(functools.partial(_dec_moba_kernel, own=past // MOBA_BLOCK), kv, in_specs,
                     [q, new, bias, onehot_t], [row4], [jax.ShapeDtypeStruct((db, n_kv, DEC_ROWS, LANES), F32)], "dec_moba")[0]


def _dec_sb_kernel(kv_ref, q_ref, u_ref, o_ref):
    u = u_ref[...]
    nchunk = kv_ref.shape[2] // TK
    for h in range(q_ref.shape[1]):
        kvb = kv_ref[0, h].astype(BF16)
        z = _dot_nt(q_ref[0, h], kvb)
        sp = jnp.maximum(z, 0.0) + jnp.log1p(jnp.exp(-jnp.abs(z)))
        lk = -sp
        hi, lo = _split2(lk)
        cols = [slice(c * TK, (c + 1) * TK) for c in range(nchunk)]
        tot = [jnp.sum(lk[:, sl], axis=-1, keepdims=True) for sl in cols]
        between = []
        carry = jnp.zeros((DEC_ROWS, 1), F32)
        for c in reversed(range(nchunk)):
            between.append(_dot(hi[:, cols[c]], u) + _dot(lo[:, cols[c]], u) + carry)
            carry = carry + tot[c]
        a = jnp.exp(z - sp + jnp.concatenate(between[::-1], axis=-1))
        o_ref[0, h] = _dot(a.astype(BF16), kvb)


def _dec_sb_pages_kernel(pt_ref, *refs):
    *page_refs, q_ref, u_ref, o_ref = refs
    u = u_ref[...]
    ps = page_refs[0].shape[-1]
    per_chunk = TK // ps
    nchunk = len(page_refs) // per_chunk
    for h in range(q_ref.shape[1]):
        q = q_ref[0, h][:, :HEAD_DIM]
        z = jnp.concatenate([_dot(q, x_ref[0, h].astype(BF16)) for x_ref in page_refs], axis=-1)
        sp = jnp.maximum(z, 0.0) + jnp.log1p(jnp.exp(-jnp.abs(z)))
        lk = -sp
        hi, lo = _split2(lk)
        cols = [slice(c * TK, (c + 1) * TK) for c in range(nchunk)]
        tot = [jnp.sum(lk[:, sl], axis=-1, keepdims=True) for sl in cols]
        between = []
        carry = jnp.zeros((DEC_ROWS, 1), F32)
        for c in reversed(range(nchunk)):
            between.append(_dot(hi[:, cols[c]], u) + _dot(lo[:, cols[c]], u) + carry)
            carry = carry + tot[c]
        a = jnp.exp(z - sp + jnp.concatenate(between[::-1], axis=-1)).astype(BF16)
        o = jnp.zeros((DEC_ROWS, HEAD_DIM), F32)
        for p, x_ref in enumerate(page_refs):
            o = o + _dot_nt(a[:, p * ps:(p + 1) * ps], x_ref[1, h].astype(BF16))
        o_ref[0, h] = jnp.concatenate([jnp.zeros((DEC_ROWS, HEAD_DIM), F32), o], axis=-1)


def _dec_sb_pages(q, pool_t, layer, page_table):
    db, n_kv, _, _ = q.shape
    npg = page_table.shape[1]
    _, _, two, hh, dd, ps = pool_t.shape
    idx = np.arange(TK)
    u = jnp.asarray(idx[:, None] > idx[None, :], BF16)

    def page_map(p):
        return lambda b, pt: (layer, pt[b * npg + p], 0, 0, 0, 0)

    row4 = pl.BlockSpec((1, n_kv, DEC_ROWS, LANES), lambda b, pt: (b, 0, 0, 0))
    grid_spec = pltpu.PrefetchScalarGridSpec(
        num_scalar_prefetch=1,
        grid=(db,),
        in_specs=[pl.BlockSpec((None, None, two, hh, dd, ps), page_map(p)) for p in range(npg)]
        + [row4, pl.BlockSpec((TK, TK), lambda b, pt: (0, 0))],
        out_specs=row4,
    )
    return pl.pallas_call(
        _dec_sb_pages_kernel, grid_spec=grid_spec,
        out_shape=jax.ShapeDtypeStruct((db, n_kv, DEC_ROWS, LANES), F32),
        compiler_params=_cparams("parallel"), name="dec_sb_pages",
    )(page_table.reshape(-1), *([pool_t] * npg), q, u)


def _dec_sb(q, kv):
    db, n_kv, _, _ = q.shape
    idx = np.arange(TK)
    u = jnp.asarray(idx[:, None] > idx[None, :], BF16)
    row4 = pl.BlockSpec((1, n_kv, DEC_ROWS, LANES), lambda b: (b, 0, 0, 0))
    return _dec_call(_dec_sb_kernel, kv, [row4, pl.BlockSpec((TK, TK), lambda b: (0, 0))], [q, u], [row4],
                     [jax.ShapeDtypeStruct((db, n_kv, DEC_ROWS, LANES), F32)], "dec_sb")[0]


def _rank_kernel(s_ref, o_ref, sc_scr, *, qpos, block, kth):
    n = s_ref.shape[2]
    blk = lax.broadcasted_iota(jnp.int32, (NBLK_PAD, n), 0)
    own = qpos // block
    past_score = jnp.where(blk * block <= qpos, s_ref[0], -SEL_BIG)
    score = jnp.where((blk == own) | (blk == 0), SEL_BIG, past_score)
    cols = _rank_nsel(score, sc_scr, min(own + 1, NBLK_PAD), kth, -jnp.inf)
    for c, nsel_t in enumerate(cols):
        o_ref[0, c * LANES:(c + 1) * LANES, :] = nsel_t.T.astype(BF16)


def _rank_select(score_t, qpos, block, kth):
    x, _, n = score_t.shape
    return pl.pallas_call(
        functools.partial(_rank_kernel, qpos=qpos, block=block, kth=kth),
        grid=(x,),
        in_specs=[pl.BlockSpec((1, NBLK_PAD, n), lambda i: (i, 0, 0))],
        out_specs=pl.BlockSpec((1, n, NBLK_PAD), lambda i: (i, 0, 0)),
        out_shape=jax.ShapeDtypeStruct((x, n, NBLK_PAD), BF16),
        scratch_shapes=[pltpu.VMEM((NBLK_PAD, n), F32)],
        compiler_params=_cparams("parallel"),
        name="rank_select",
    )(score_t)


def _dec_q(x, db, n_kv, g):
    x = x.reshape(db, n_kv, g, HEAD_DIM)
    return jnp.pad(x, ((0, 0), (0, 0), (0, DEC_ROWS - g), (0, LANES - HEAD_DIM)))


def _dec_new(x, db, n_heads):
    return jnp.pad(x.reshape(db, n_heads, 1, LANES), ((0, 0), (0, 0), (0, LANES - 1), (0, 0)))


def _dec_bias(tb, n_kv, g, qpos, kpos, win=None):
    t = _bias_table(tb, n_kv * g, np.full(DEC_ROWS, qpos), kpos, win)[:, 0]
    return jnp.pad(t.reshape(n_kv, g, -1), ((0, 0), (0, DEC_ROWS - g), (0, 0)))


def _dec_out(o, db, n_kv, g):
    return o[:, :, :g, HEAD_DIM:].reshape(db, n_kv * g * HEAD_DIM)


def _dec_state(x_t, db, n_heads):
    return x_t.reshape(2, n_heads, HEAD_DIM, db).transpose(3, 0, 1, 2).reshape(db, 1, 2, n_heads, HEAD_DIM)


def _pool_t(pool):
    return pool.transpose(0, 1, 3, 4, 5, 2)


def _even_sample(h, P, j, li, c_cmp, c_slc, c_win, c_sb, page_table):
    db = h.shape[0]
    w_in = P['w_in_even'][j]
    q_gain, k_gain = P['nsa_q_norm'][j], P['nsa_k_norm'][j]
    cut = lambda k: w_in[:, _E_COLS[k][0]:_E_COLS[k][1]]
    hkv, g = NSA_KV_HEADS, G_NSA
    kvw = cut('kv')
    w_row = jnp.concatenate([cut('q'), cut('qs'), _gate_cols(cut('g')),
                             _pair_cols(kvw[:, 256:384], kvw[:, 384:512], hkv),
                             _pair_cols(kvw[:, 512:640], kvw[:, 640:768], hkv),
                             _pair_cols(cut('sb')[:, :256], cut('sb')[:, 256:], SB_KV_HEADS)], axis=1).astype(BF16)
    one = jnp.ones((HEAD_DIM,), F32)
    cs = jnp.concatenate([jnp.tile(q_gain, NSA_HEADS) * SCALE, jnp.full((512,), SCALE, F32), jnp.ones((256,), F32),
                          jnp.tile(jnp.concatenate([k_gain[1], one]), hkv), jnp.tile(jnp.concatenate([k_gain[2], one]), hkv),
                          jnp.ones((512,), F32)])
    row_modes = ('norm',) * 4 + ('scale',) * 4 + ('sigmoid',) * 2 + ('normk',) * 4 + ('scale',) * 4
    w_t, gt = _even_t_weights(w_in, k_gain)
    q, qs, gates, slc_new, win_new, sb_new, cmp_t, slc_t, win_t, sb_t = _proj(
        h, 1, db, P['norm_mix'][li], w_row, cs, row_modes, ((0, 4), (4, 8), (8, 10), (10, 12), (12, 14), (14, 18)),
        (BF16, BF16, F32, F32, F32, F32), w_t, gt, _E_T_MODES, _E_T_OUTS)
    tb = P['rel_bias_table']
    qp = _dec_q(q, db, hkv, g)
    gcol = lambda br: jnp.pad(gates.reshape(db, hkv, LANES)[:, :, br * g:(br + 1) * g], ((0, 0), (0, 0), (0, DEC_ROWS - g)))[..., None]
    past = page_table.shape[1] * PAGE_SIZE
    tp = past + LANES
    kpos = np.where(np.arange(tp) <= past, np.arange(tp), -1)

    ncp = past // CMP_STRIDE
    nc = (past + 1 - CMP_LEN) // CMP_STRIDE + 1
    kcvc = _dec_cmp_mlp(_gather_kv(_pool_t(c_cmp), j, page_table), P['nsa_cmp_w1'][j], P['nsa_cmp_pos'][j],
                        P['nsa_cmp_w2'][j], k_gain[0])
    kpos_c = np.where(np.arange(ncp) < nc, CMP_STRIDE * np.arange(ncp) + CMP_LEN - 1, -1)
    nsb = -(-(past + 1) // SEL_BLOCK)
    o, imp = _dec_attn(qp, kcvc, None, _dec_bias(tb, hkv, g, past, kpos_c), g,
                       acc=jnp.zeros((db, hkv, DEC_ROWS, LANES), F32), gate=gcol(0), c2s_t=_cmp_to_sel_t(ncp, nc, nsb))
    nsel = _rank_select(imp[:, :, 0].transpose(1, 2, 0), past, SEL_BLOCK, N_SEL)
    qa = jnp.broadcast_to(nsel.transpose(1, 0, 2)[:, :, None], (db, hkv, DEC_ROWS, NBLK_PAD))

    o = _dec_attn(qp, (_pool_t(c_slc), j, page_table), _dec_new(slc_new, db, hkv),
                  _dec_bias(tb, hkv, g, past, kpos), g, qa=qa, onehot_t=_block_onehot_neg_t(tp, SEL_BLOCK),
                  acc=o, gate=gcol(1))

    cw_t = _pool_t(c_win)
    wb = c_win.shape[2]
    kpos_w = np.where(np.arange(wb + LANES) <= wb, past - wb + np.arange(wb + LANES), -1)
    o = _dec_attn(qp, _win_rows(cw_t, j), _dec_new(win_new, db, hkv),
                  _dec_bias(tb, hkv, g, past, kpos_w, win=WINDOW), g, acc=o, gate=gcol(2))

    o_b = _dec_sb_pages(_dec_q(qs, db, SB_KV_HEADS, G_SB), _pool_t(c_sb), j, page_table)
    h = _out_proj([_dec_out(o, db, hkv, g), _dec_out(o_b, db, SB_KV_HEADS, G_SB)], P['w_out_even'][j].astype(BF16), h)

    win_state = _win_shift(cw_t, j, win_t[0].T[:, :, None]).transpose(0, 4, 1, 2, 3)
    return h, (_dec_state(cmp_t, db, hkv), _dec_state(slc_t, db, hkv), win_state, _dec_state(sb_t, db, SB_KV_HEADS))


def _odd_sample(h, P, j, li, c_moba, page_table):
    db = h.shape[0]
    hkv, g = MOBA_KV_HEADS, G_MOBA
    w_in = P['w_in_odd'][j]
    w_row = jnp.concatenate([w_in[:, :1024], _pair_cols(w_in[:, 1024:1280], w_in[:, 1280:], hkv)], axis=1).astype(BF16)
    one = jnp.ones((HEAD_DIM,), F32)
    cs = jnp.concatenate([jnp.tile(P['moba_q_norm'][j], MOBA_HEADS) * SCALE, jnp.tile(jnp.concatenate([P['moba_k_norm'][j], one]), hkv)])
    w_t = w_in[:, 1024:].T.astype(BF16)
    gt = jnp.concatenate([jnp.tile(P['moba_k_norm'][j], hkv), jnp.ones((hkv * HEAD_DIM,), F32)])
    q, kv_new, kv_t = _proj(h, 1, db, P['norm_mix'][li], w_row, cs, ('norm',) * 8 + ('normk',) * 4, ((0, 8), (8, 12)),
                            (BF16, F32), w_t, gt, ('norm',) * 4 + ('raw',) * 4, ((0, 8),))
    past = page_table.shape[1] * PAGE_SIZE
    tp = past + LANES
    kpos = np.where(np.arange(tp) <= past, np.arange(tp), -1)
    o = _dec_moba(_dec_q(q, db, hkv, g), (_pool_t(c_moba), j, page_table), _dec_new(kv_new, db, hkv),
                  _dec_bias(P['rel_bias_table'], hkv, g, past, kpos), _block_onehot_neg_t(tp, MOBA_BLOCK), past)
    h = _out_proj([_dec_out(o, db, hkv, g)], P['w_out_odd'][j].astype(BF16), h)
    return h, (_dec_state(kv_t, db, hkv),)


def _trunk_sample(x, p, P, caches, page_table):
    db, t, d = x.shape
    h = x.reshape(db * t, d)
    c_cmp, c_slc, c_win, c_sb, c_moba = caches
    even_states, odd_states = [], []
    for li in range(p.shape[0]):
        j = li // 2
        if li % 2 == 0:
            h, st = _even_sample(h, P, j, li, c_cmp, c_slc, c_win, c_sb, page_table)
            even_states.append(st)
        else:
            h, st = _odd_sample(h, P, j, li, c_moba, page_table)
            odd_states.append(st)
        h = _ffn_and_pe(h, p[li].reshape(db * t, -1), P, li)
    return h.reshape(db, t, d), even_states, odd_states


def kernel(x_prompt, x_sample, cache_nsa_cmp, cache_nsa_slc, cache_nsa_win, cache_sb, cache_moba, page_table,
           p_prompt, p_sample, rel_bias_table, norm_mix, norm_ffn, norm_pe, w_pe_in, w_pe_gate,
           w_in_even, w_out_even, nsa_q_norm, nsa_k_norm, nsa_cmp_pos, nsa_cmp_w1, nsa_cmp_w2,
           w_ffn_gu, w_ffn_down, w_in_odd, w_out_odd, moba_q_norm, moba_k_norm,
           w_router, b_router, w_moe_gu, w_moe_down):
    P = {'rel_bias_table': rel_bias_table, 'norm_mix': norm_mix, 'norm_ffn': norm_ffn, 'norm_pe': norm_pe,
         'w_pe_in': w_pe_in, 'w_pe_gate': w_pe_gate, 'w_in_even': w_in_even, 'w_out_even': w_out_even,
         'nsa_q_norm': nsa_q_norm, 'nsa_k_norm': nsa_k_norm, 'nsa_cmp_pos': nsa_cmp_pos, 'nsa_cmp_w1': nsa_cmp_w1,
         'nsa_cmp_w2': nsa_cmp_w2, 'w_ffn_gu': w_ffn_gu, 'w_ffn_down': w_ffn_down, 'w_in_odd': w_in_odd,
         'w_out_odd': w_out_odd, 'moba_q_norm': moba_q_norm, 'moba_k_norm': moba_k_norm, 'w_router': w_router,
         'b_router': b_router, 'w_moe_gu': w_moe_gu, 'w_moe_down': w_moe_down}
    y_prompt, pe, po = _trunk_prompt(x_prompt, p_prompt, P)
    caches = (cache_nsa_cmp, cache_nsa_slc, cache_nsa_win, cache_sb, cache_moba)
    y_sample, se, so = _trunk_sample(x_sample, p_sample, P, caches, page_table)
    stack = lambda states, k: jnp.stack([s[k] for s in states])
    return (y_prompt, y_sample,
            stack(pe, 0), stack(se, 0), stack(pe, 1), stack(se, 1),
            stack(pe, 2), stack(se, 2), stack(pe, 3), stack(se, 3),
            stack(po, 0), stack(so, 0))
```
